```python
import math
import jax, jax.numpy as jnp
from jax import lax
import numpy as np

D_MODEL = 1024
BATCH = 8
SEQ = 4096
DEPTH = 1

D_MIX = D_MODEL
D_SSM = D_MIX // 2
D_CONV = D_MIX - D_SSM
SSM_GROUP = 16
N_SSM_GROUPS = D_SSM // SSM_GROUP
SSM_STATE = 64
CONV_HEADS = 8
CONV_WIDTH = 3
D_FF = ((8 * D_MODEL // 3 + 127) // 128) * 128
FFN_CONV_WIDTH = 3
N_MOD = 6
D_IN_PROJ = D_SSM + 3 * D_CONV
EPS = 1e-6
DT_MIN = 1e-3
DT_MAX = 1e-1
LAMBDA_RE_MAX = -1e-4

kernel_name = 'hymba_s5_shortconv_convffn_adaln'


def rms_norm(x, g):
    xf = x.astype(jnp.float32)
    y = xf * lax.rsqrt(jnp.mean(xf * xf, axis=-1, keepdims=True) + EPS)
    return (y * g.astype(jnp.float32)).astype(x.dtype)


def head_rms_norm(y, g, n_heads):
    shp = y.shape
    yf = y.astype(jnp.float32).reshape(shp[:-1] + (n_heads, shp[-1] // n_heads))
    yf = yf * lax.rsqrt(jnp.mean(yf * yf, axis=-1, keepdims=True) + EPS)
    return (yf.reshape(shp) * g.astype(jnp.float32)).astype(y.dtype)


def causal_dwconv(x, w):
    k_w = w.shape[0]
    seq = x.shape[1]
    xp = jnp.pad(x, ((0, 0), (k_w - 1, 0), (0, 0)))
    y = xp[:, 0:seq, :] * w[0]
    for k in range(1, k_w):
        y = y + xp[:, k:k + seq, :] * w[k]
    return y


def _s5_binop(e1, e2):
    a1, b1 = e1
    a2, b2 = e2
    return a2 * a1, a2 * b1 + b2


def s5_mixer(u, lam_re, lam_im, log_step, b_re, b_im, c_re, c_im, d_skip, glu_w, glu_b):
    bsz, seq, _ = u.shape
    uf = u.astype(jnp.float32)
    ug = uf.reshape(bsz, seq, N_SSM_GROUPS, SSM_GROUP)
    lam = lax.complex(jnp.minimum(lam_re.astype(jnp.float32), LAMBDA_RE_MAX),
                      lam_im.astype(jnp.float32))
    step = jnp.exp(log_step.astype(jnp.float32))[:, None]
    lam_bar = jnp.exp(lam * step)
    b_c = lax.complex(b_re.astype(jnp.float32), b_im.astype(jnp.float32))
    b_bar = ((lam_bar - 1.0) / lam)[..., None] * b_c
    bu = jnp.einsum('blgh,gph->blgp', ug.astype(jnp.complex64), b_bar)
    a = jnp.broadcast_to(lam_bar, (1, seq) + lam_bar.shape)
    _, states = lax.associative_scan(_s5_binop, (a, bu), axis=1)
    c_c = lax.complex(c_re.astype(jnp.float32), c_im.astype(jnp.float32))
    y = jnp.einsum('blgp,ghp->blgh', states, c_c).real.reshape(bsz, seq, D_SSM)
    y = y + d_skip.astype(jnp.float32) * uf
    z = jax.nn.gelu(y)
    z = z * jax.nn.sigmoid(z @ glu_w.astype(jnp.float32) + glu_b.astype(jnp.float32))
    return z.astype(u.dtype)


def short_conv_mixer(bg, cg, v, conv_w):
    return bg * causal_dwconv(cg * v, conv_w)


def conv_ffn(h, w_up, ffn_conv_w, w_down):
    hid = causal_dwconv(h @ w_up, ffn_conv_w)
    a, v = jnp.split(hid, 2, axis=-1)
    return (jax.nn.silu(a) * v) @ w_down


def setup_inputs(seed: int = 0) -> dict:
    key = jax.random.key(seed)
    ks = jax.random.split(key, 26)
    f32 = jnp.float32

    def nrm(k, shape, s):
        return jax.random.normal(k, shape, f32) * s

    nl = DEPTH
    g_, p_, h_ = N_SSM_GROUPS, SSM_STATE, SSM_GROUP
    n_idx = jnp.arange(SSM_STATE, dtype=f32)
    return {
        'x': nrm(ks[0], (BATCH, SEQ, D_MODEL), 1.0),
        'c': nrm(ks[1], (BATCH, D_MODEL), 1.0),
        'w_ada': nrm(ks[2], (nl, D_MODEL, N_MOD * D_MODEL), 0.5 * D_MODEL ** -0.5),
        'b_ada': nrm(ks[3], (nl, N_MOD * D_MODEL), 0.02),
        'g_pre_mix': 1.0 + nrm(ks[4], (nl, D_MODEL), 0.02),
        'g_post_mix': 1.0 + nrm(ks[5], (nl, D_MODEL), 0.02),
        'w_in': nrm(ks[6], (nl, D_MODEL, D_IN_PROJ), D_MODEL ** -0.5),
        'ssm_lam_re': -0.5 + nrm(ks[7], (nl, g_, p_), 0.01),
        'ssm_lam_im': math.pi * n_idx + nrm(ks[8], (nl, g_, p_), 0.01),
        'ssm_log_step': jax.random.uniform(ks[9], (nl, g_), f32, math.log(DT_MIN), math.log(DT_MAX)),
        'ssm_b_re': nrm(ks[10], (nl, g_, p_, h_), (2 * h_) ** -0.5),
        'ssm_b_im': nrm(ks[11], (nl, g_, p_, h_), (2 * h_) ** -0.5),
        'ssm_c_re': nrm(ks[12], (nl, g_, h_, p_), p_ ** -0.5),
        'ssm_c_im': nrm(ks[13], (nl, g_, h_, p_), p_ ** -0.5),
        'ssm_d': nrm(ks[14], (nl, D_SSM), 1.0),
        'glu_w': nrm(ks[15], (nl, D_SSM, D_SSM), D_SSM ** -0.5),
        'glu_b': nrm(ks[16], (nl, D_SSM), 0.02),
        'g_out_ssm': 1.0 + nrm(ks[17], (nl, D_SSM), 0.02),
        'conv_w': nrm(ks[18], (nl, CONV_WIDTH, D_CONV), CONV_WIDTH ** -0.5),
        'g_out_conv': 1.0 + nrm(ks[19], (nl, D_CONV), 0.02),
        'w_out': nrm(ks[20], (nl, D_MIX, D_MODEL), D_MIX ** -0.5),
        'g_pre_ffn': 1.0 + nrm(ks[21], (nl, D_MODEL), 0.02),
        'g_post_ffn': 1.0 + nrm(ks[22], (nl, D_MODEL), 0.02),
        'w_up': nrm(ks[23], (nl, D_MODEL, 2 * D_FF), D_MODEL ** -0.5),
        'ffn_conv_w': nrm(ks[24], (nl, FFN_CONV_WIDTH, 2 * D_FF), FFN_CONV_WIDTH ** -0.5),
        'w_down': nrm(ks[25], (nl, D_FF, D_MODEL), D_FF ** -0.5),
    }


def reference(x, c, w_ada, b_ada, g_pre_mix, g_post_mix, w_in, ssm_lam_re, ssm_lam_im, ssm_log_step,
              ssm_b_re, ssm_b_im, ssm_c_re, ssm_c_im, ssm_d, glu_w, glu_b, g_out_ssm, conv_w, g_out_conv,
              w_out, g_pre_ffn, g_post_ffn, w_up, ffn_conv_w, w_down):
    c_act = jax.nn.silu(c)
    for i in range(DEPTH):
        mod = (c_act @ w_ada[i] + b_ada[i])[:, None, :]
        sh1, sc1, gt1, sh2, sc2, gt2 = jnp.split(mod, N_MOD, axis=-1)

        h = rms_norm(x, g_pre_mix[i]) * (1.0 + sc1) + sh1
        proj = h @ w_in[i]
        u = proj[..., :D_SSM]
        bg, cg, v = jnp.split(proj[..., D_SSM:], 3, axis=-1)
        y_a = s5_mixer(u, ssm_lam_re[i], ssm_lam_im[i], ssm_log_step[i], ssm_b_re[i], ssm_b_im[i],
                       ssm_c_re[i], ssm_c_im[i], ssm_d[i], glu_w[i], glu_b[i])
        y_b = short_conv_mixer(bg, cg, v, conv_w[i])
        y = jnp.concatenate([head_rms_norm(y_a, g_out_ssm[i], N_SSM_GROUPS),
                             head_rms_norm(y_b, g_out_conv[i], CONV_HEADS)], axis=-1)
        x = x + gt1 * rms_norm(y @ w_out[i], g_post_mix[i])

        h = rms_norm(x, g_pre_ffn[i]) * (1.0 + sc2) + sh2
        x = x + gt2 * rms_norm(conv_ffn(h, w_up[i], ffn_conv_w[i], w_down[i]), g_post_ffn[i])
    return x
```

```python
import functools
import math

import jax
import jax.numpy as jnp
from jax import lax
from jax.experimental import pallas as pl
from jax.experimental.pallas import tpu as pltpu

D_MODEL = 1024
BATCH = 8
SEQ = 4096
D_SSM = 512
D_CONV = 512
SSM_GROUP = 16
N_SSM_GROUPS = 32
SSM_STATE = 64
CONV_HEADS = 8
D_FF = 2816
N_MOD = 6
D_IN_PROJ = 2048
EPS = 1e-6
LAMBDA_RE_MAX = -1e-4

N_STATE = N_SSM_GROUPS * SSM_STATE
SUBLANES = 8
LANES = 128
GROUPS_PER_LANE_TILE = LANES // SSM_GROUP
N_LANE_TILES_U = D_SSM // LANES
STATE_PER_TILE = GROUPS_PER_LANE_TILE * SSM_STATE

T_PRE = 512
T_SSM = 128
T_FFN = 512
FF_CHUNK = 256
N_FF_CHUNKS = D_FF // FF_CHUNK
SCAN_COLS = 512
VMEM_LIMIT = 56 * 1024 * 1024

_f32 = jnp.float32
_bf16 = jnp.bfloat16


def _const_spec(shape):
    nd = len(shape)
    return pl.BlockSpec(shape, lambda *_: (0,) * nd, pipeline_mode=pl.Buffered(1))


def _rms(x, g):
    ms = jnp.mean(x * x, axis=-1, keepdims=True)
    return x * lax.rsqrt(ms + EPS) * g


def _mod_kernel(c_ref, w_ref, b_ref, o_ref):
    c = c_ref[...]
    c_act = c * jax.nn.sigmoid(c)
    o_ref[...] = jnp.dot(c_act, w_ref[...], preferred_element_type=_f32) + b_ref[...]


def _modulation(c, w_ada, b_ada):
    n_tile = D_MODEL
    return pl.pallas_call(
        _mod_kernel,
        grid=(N_MOD * D_MODEL // n_tile,),
        in_specs=[
            pl.BlockSpec((BATCH, D_MODEL), lambda j: (0, 0)),
            pl.BlockSpec((D_MODEL, n_tile), lambda j: (0, j)),
            pl.BlockSpec((1, n_tile), lambda j: (0, j)),
        ],
        out_specs=pl.BlockSpec((BATCH, n_tile), lambda j: (0, j)),
        out_shape=jax.ShapeDtypeStruct((BATCH, N_MOD * D_MODEL), _f32),
        name="adaln_mod",
    )(c, w_ada, b_ada.reshape(1, -1))


def _prep_kernel(lre_ref, lim_ref, lstep_ref, bre_ref, bim_ref, abre_ref, abim_ref, bbre_ref, bbim_ref):
    lre = jnp.minimum(lre_ref[...], LAMBDA_RE_MAX)
    lim = lim_ref[...]
    step = jnp.exp(lstep_ref[...])
    mag = jnp.exp(lre * step)
    ang = lim * step
    are = mag * jnp.cos(ang)
    aim = mag * jnp.sin(ang)
    abre_ref[...] = are
    abim_ref[...] = aim
    nre = are - 1.0
    nim = aim
    den = lre * lre + lim * lim
    cre = (nre * lre + nim * lim) / den
    cim = (nim * lre - nre * lim) / den
    bre = bre_ref[...]
    bim = bim_ref[...]
    bbre_ref[...] = cre * bre - cim * bim
    bbim_ref[...] = cre * bim + cim * bre


def _ssm_prep(lam_re, lam_im, log_step, b_re, b_im):
    col = lambda a: a.reshape(N_STATE, 1)
    lstep = jnp.broadcast_to(log_step[:, None], (N_SSM_GROUPS, SSM_STATE))
    outs = pl.pallas_call(
        _prep_kernel,
        out_shape=(
            jax.ShapeDtypeStruct((N_STATE, 1), _f32),
            jax.ShapeDtypeStruct((N_STATE, 1), _f32),
            jax.ShapeDtypeStruct((N_STATE, SSM_GROUP), _f32),
            jax.ShapeDtypeStruct((N_STATE, SSM_GROUP), _f32),
        ),
        name="s5_discretise",
    )(col(lam_re), col(lam_im), col(lstep), b_re.reshape(N_STATE, SSM_GROUP), b_im.reshape(N_STATE, SSM_GROUP))
    return outs


def _group_block_diag(t):
    eye = jnp.eye(GROUPS_PER_LANE_TILE, dtype=bool)
    nj, ng, r, c = t.shape
    blk = jnp.where(eye[None, :, None, :, None], t[:, :, :, None, :], 0.0)
    return blk.reshape(nj, ng * r, ng * c)


def _premix_kernel(x_ref, mod_ref, g_ref, win_ref, cw_ref, gconv_ref, ones_ref,
                   u_ref, yb_ref, buf_ref):
    ti = pl.program_id(1)

    @pl.when(ti == 0)
    def _():
        buf_ref[0:SUBLANES, :] = jnp.zeros((SUBLANES, D_CONV), _f32)

    x = x_ref[0]
    sh = mod_ref[0, 0:1, :]
    sc = mod_ref[0, 1:2, :]
    h = _rms(x, g_ref[...]) * (1.0 + sc) + sh
    proj = jnp.dot(h.astype(_bf16), win_ref[...], preferred_element_type=_f32)
    u_ref[...] = proj[:, 0:D_SSM].astype(_bf16)
    bg = proj[:, D_SSM:D_SSM + D_CONV]
    cg = proj[:, D_SSM + D_CONV:D_SSM + 2 * D_CONV]
    v = proj[:, D_SSM + 2 * D_CONV:]
    cv = cg * v
    buf_ref[SUBLANES:SUBLANES + T_PRE, :] = cv
    c1 = buf_ref[SUBLANES - 1:SUBLANES - 1 + T_PRE, :]
    c2 = buf_ref[SUBLANES - 2:SUBLANES - 2 + T_PRE, :]
    buf_ref[0:SUBLANES, :] = buf_ref[T_PRE:T_PRE + SUBLANES, :]
    yb = bg * (cw_ref[0:1, :] * c2 + cw_ref[1:2, :] * c1 + cw_ref[2:3, :] * cv)
    ms = jnp.dot((yb * yb).astype(_bf16), ones_ref[...], preferred_element_type=_f32)
    yb_ref[0] = (yb * lax.rsqrt(ms + EPS) * gconv_ref[...]).astype(_bf16)


def _premix(x, mod3, g_pre, w_in_b, conv_w, g_out_conv, ones_conv):
    n_t = SEQ // T_PRE
    return pl.pallas_call(
        _premix_kernel,
        grid=(BATCH, n_t),
        in_specs=[
            pl.BlockSpec((1, T_PRE, D_MODEL), lambda b, t: (b, t, 0)),
            pl.BlockSpec((1, N_MOD, D_MODEL), lambda b, t: (b, 0, 0)),
            _const_spec((1, D_MODEL)),
            _const_spec((D_MODEL, D_IN_PROJ)),
            _const_spec((3, D_CONV)),
            _const_spec((1, D_CONV)),
            _const_spec((D_CONV, D_CONV)),
        ],
        out_specs=[
            pl.BlockSpec((T_PRE, D_SSM), lambda b, t: (t, b)),
            pl.BlockSpec((1, T_PRE, D_CONV), lambda b, t: (b, t, 0)),
        ],
        out_shape=[
            jax.ShapeDtypeStruct((SEQ, BATCH * D_SSM), _bf16),
            jax.ShapeDtypeStruct((BATCH, SEQ, D_CONV), _bf16),
        ],
        scratch_shapes=[pltpu.VMEM((T_PRE + SUBLANES, D_CONV), _f32)],
        compiler_params=pltpu.CompilerParams(
            dimension_semantics=("arbitrary", "arbitrary"), vmem_limit_bytes=VMEM_LIMIT),
        name="premix",
    )(x, mod3, g_pre, w_in_b, conv_w, g_out_conv, ones_conv)


def _ssm_kernel(u_ref, bw_ref, lam_ref, cwre_ref, cwim_ref, d_ref, gluw_ref, glub_ref, g_ref, ones_ref,
                y_ref, xre_ref, xim_ref, st_ref):
    i = pl.program_id(0)

    @pl.when(i == 0)
    def _():
        st_ref[...] = jnp.zeros(st_ref.shape, _f32)

    u = u_ref[...]
    for j in range(N_LANE_TILES_U):
        bu = jnp.dot(u[:, j * LANES:(j + 1) * LANES], bw_ref[j], preferred_element_type=_f32)
        xre_ref[:, j * STATE_PER_TILE:(j + 1) * STATE_PER_TILE] = bu[:, 0:STATE_PER_TILE]
        xim_ref[:, j * STATE_PER_TILE:(j + 1) * STATE_PER_TILE] = bu[:, STATE_PER_TILE:]

    for cb in range(N_STATE // SCAN_COLS):
        cols = slice(cb * SCAN_COLS, (cb + 1) * SCAN_COLS)
        lre = lam_ref[0, :, cols]
        lim = lam_ref[1, :, cols]

        def body(t, carry, cols=cols, lre=lre, lim=lim):
            sre, sim = carry
            r = pl.multiple_of(t * BATCH, BATCH)
            nre = lre * sre - lim * sim + xre_ref[pl.ds(r, BATCH), cols]
            nim = lre * sim + lim * sre + xim_ref[pl.ds(r, BATCH), cols]
            xre_ref[pl.ds(r, BATCH), cols] = nre
            xim_ref[pl.ds(r, BATCH), cols] = nim
            return nre, nim

        sre, sim = lax.fori_loop(0, T_SSM, body, (st_ref[0, :, cols], st_ref[1, :, cols]), unroll=4)
        st_ref[0, :, cols] = sre
        st_ref[1, :, cols] = sim

    ys = []
    for j in range(N_LANE_TILES_U):
        cols = slice(j * STATE_PER_TILE, (j + 1) * STATE_PER_TILE)
        yj = jnp.dot(xre_ref[:, cols].astype(_bf16), cwre_ref[j], preferred_element_type=_f32)
        yj = yj - jnp.dot(xim_ref[:, cols].astype(_bf16), cwim_ref[j], preferred_element_type=_f32)
        ys.append(yj)
    y = jnp.concatenate(ys, axis=-1) + d_ref[...] * u.astype(_f32)
    z = jax.nn.gelu(y)
    gate = jnp.dot(z.astype(_bf16), gluw_ref[...], preferred_element_type=_f32) + glub_ref[...]
    z = z * jax.nn.sigmoid(gate)
    ms = jnp.dot((z * z).astype(_bf16), ones_ref[...], preferred_element_type=_f32)
    y_ref[...] = (z * lax.rsqrt(ms + EPS) * g_ref[...]).astype(_bf16)


def _ssm(u2, bw, lam8, cw_re, cw_im, d_skip, glu_w_b, glu_b, g_out_ssm, ones_ssm):
    rows = T_SSM * BATCH
    return pl.pallas_call(
        _ssm_kernel,
        grid=(SEQ // T_SSM,),
        in_specs=[
            pl.BlockSpec((rows, D_SSM), lambda i: (i, 0)),
            _const_spec((N_LANE_TILES_U, LANES, 2 * STATE_PER_TILE)),
            _const_spec((2, SUBLANES, N_STATE)),
            _const_spec((N_LANE_TILES_U, STATE_PER_TILE, LANES)),
            _const_spec((N_LANE_TILES_U, STATE_PER_TILE, LANES)),
            _const_spec((1, D_SSM)),
            _const_spec((D_SSM, D_SSM)),
            _const_spec((1, D_SSM)),
            _const_spec((1, D_SSM)),
            _const_spec((D_SSM, D_SSM)),
        ],
        out_specs=pl.BlockSpec((rows, D_SSM), lambda i: (i, 0)),
        out_shape=jax.ShapeDtypeStruct((SEQ * BATCH, D_SSM), _bf16),
        scratch_shapes=[
            pltpu.VMEM((rows, N_STATE), _f32),
            pltpu.VMEM((rows, N_STATE), _f32),
            pltpu.VMEM((2, SUBLANES, N_STATE), _f32),
        ],
        compiler_params=pltpu.CompilerParams(
            dimension_semantics=("arbitrary",), vmem_limit_bytes=VMEM_LIMIT),
        name="s5_mixer",
    )(u2, bw, lam8, cw_re, cw_im, d_skip, glu_w_b, glu_b, g_out_ssm, ones_ssm)


def _ffn_kernel(x_ref, ya_ref, yb_ref, mod_ref, wout_ref, gpm_ref, gpf_ref, wup_ref, fcw_ref, wdn_ref, gpo_ref,
                o_ref, buf_ref, halo_ref):
    ti = pl.program_id(1)

    @pl.when(ti == 0)
    def _():
        halo_ref[...] = jnp.zeros(halo_ref.shape, _f32)

    x = x_ref[0]
    gt1 = mod_ref[0, 2:3, :]
    sh2 = mod_ref[0, 3:4, :]
    sc2 = mod_ref[0, 4:5, :]
    gt2 = mod_ref[0, 5:6, :]
    mix = jnp.dot(ya_ref[...], wout_ref[0:D_SSM, :], preferred_element_type=_f32)
    mix = mix + jnp.dot(yb_ref[0], wout_ref[D_SSM:, :], preferred_element_type=_f32)
    x1 = x + gt1 * _rms(mix, gpm_ref[...])
    h2 = (_rms(x1, gpf_ref[...]) * (1.0 + sc2) + sh2).astype(_bf16)

    acc = jnp.zeros((T_FFN, D_MODEL), _f32)
    for c in range(N_FF_CHUNKS):
        cols = slice(c * 2 * FF_CHUNK, (c + 1) * 2 * FF_CHUNK)
        hc = jnp.dot(h2, wup_ref[:, cols], preferred_element_type=_f32)
        buf_ref[0:SUBLANES, :] = halo_ref[:, cols]
        buf_ref[SUBLANES:SUBLANES + T_FFN, :] = hc
        h1 = buf_ref[SUBLANES - 1:SUBLANES - 1 + T_FFN, :]
        h2s = buf_ref[SUBLANES - 2:SUBLANES - 2 + T_FFN, :]
        halo_ref[:, cols] = buf_ref[T_FFN:T_FFN + SUBLANES, :]
        conv = fcw_ref[0:1, cols] * h2s + fcw_ref[1:2, cols] * h1 + fcw_ref[2:3, cols] * hc
        a = conv[:, 0:FF_CHUNK]
        v = conv[:, FF_CHUNK:]
        act = (a * jax.nn.sigmoid(a) * v).astype(_bf16)
        acc = acc + jnp.dot(act, wdn_ref[c * FF_CHUNK:(c + 1) * FF_CHUNK, :], preferred_element_type=_f32)
    o_ref[0] = x1 + gt2 * _rms(acc, gpo_ref[...])


def _ffn(x, ya_tm, yb, mod3, w_out_b, g_post_mix, g_pre_ffn, w_up_b, ffn_cw, w_down_b, g_post_ffn):
    n_t = SEQ // T_FFN
    return pl.pallas_call(
        _ffn_kernel,
        grid=(BATCH, n_t),
        in_specs=[
            pl.BlockSpec((1, T_FFN, D_MODEL), lambda b, t: (b, t, 0)),
            pl.BlockSpec((T_FFN, D_SSM), lambda b, t: (t, b)),
            pl.BlockSpec((1, T_FFN, D_CONV), lambda b, t: (b, t, 0)),
            pl.BlockSpec((1, N_MOD, D_MODEL), lambda b, t: (b, 0, 0)),
            _const_spec((D_MODEL, D_MODEL)),
            _const_spec((1, D_MODEL)),
            _const_spec((1, D_MODEL)),
            _const_spec((D_MODEL, 2 * D_FF)),
            _const_spec((3, 2 * D_FF)),
            _const_spec((D_FF, D_MODEL)),
            _const_spec((1, D_MODEL)),
        ],
        out_specs=pl.BlockSpec((1, T_FFN, D_MODEL), lambda b, t: (b, t, 0)),
        out_shape=jax.ShapeDtypeStruct((BATCH, SEQ, D_MODEL), _f32),
        scratch_shapes=[
            pltpu.VMEM((T_FFN + SUBLANES, 2 * FF_CHUNK), _f32),
            pltpu.VMEM((SUBLANES, 2 * D_FF), _f32),
        ],
        compiler_params=pltpu.CompilerParams(
            dimension_semantics=("arbitrary", "arbitrary"), vmem_limit_bytes=VMEM_LIMIT),
        name="outproj_convffn",
    )(x, ya_tm, yb, mod3, w_out_b, g_post_mix, g_pre_ffn, w_up_b, ffn_cw, w_down_b, g_post_ffn)


def _interleave_ff(w):
    lead = w.shape[:-1]
    w2 = w.reshape(lead + (2, N_FF_CHUNKS, FF_CHUNK))
    return jnp.swapaxes(w2, -3, -2).reshape(lead + (2 * D_FF,))


def _head_mean_matrix(width, head):
    idx = jnp.arange(width) // head
    return jnp.where(idx[:, None] == idx[None, :], 1.0 / head, 0.0).astype(_bf16)


def kernel(x, c, w_ada, b_ada, g_pre_mix, g_post_mix, w_in, ssm_lam_re, ssm_lam_im, ssm_log_step, ssm_b_re, ssm_b_im, ssm_c_re, ssm_c_im, ssm_d, glu_w, glu_b, g_out_ssm, conv_w, g_out_conv, w_out, g_pre_ffn, g_post_ffn, w_up, ffn_conv_w, w_down):
    assert x.shape == (BATCH, SEQ, D_MODEL) and w_ada.shape[0] == 1
    row = lambda a: a.reshape(1, -1)

    mod3 = _modulation(c, w_ada[0], b_ada[0]).reshape(BATCH, N_MOD, D_MODEL)

    ab_re, ab_im, bb_re, bb_im = _ssm_prep(ssm_lam_re[0], ssm_lam_im[0], ssm_log_step[0], ssm_b_re[0], ssm_b_im[0])
    lam8 = jnp.broadcast_to(jnp.stack([ab_re.reshape(1, N_STATE), ab_im.reshape(1, N_STATE)]),
                            (2, SUBLANES, N_STATE))
    tb = lambda a: a.reshape(N_LANE_TILES_U, GROUPS_PER_LANE_TILE, SSM_STATE, SSM_GROUP).transpose(0, 1, 3, 2)
    bw = jnp.concatenate([_group_block_diag(tb(bb_re)), _group_block_diag(tb(bb_im))], axis=-1).astype(_bf16)
    tc = lambda a: a.reshape(N_LANE_TILES_U, GROUPS_PER_LANE_TILE, SSM_GROUP, SSM_STATE).transpose(0, 1, 3, 2)
    cw_re = _group_block_diag(tc(ssm_c_re[0])).astype(_bf16)
    cw_im = _group_block_diag(tc(ssm_c_im[0])).astype(_bf16)

    u_tm, yb = _premix(x, mod3, row(g_pre_mix[0]), w_in[0].astype(_bf16), conv_w[0], row(g_out_conv[0]),
                       _head_mean_matrix(D_CONV, D_CONV // CONV_HEADS))

    ya = _ssm(u_tm.reshape(SEQ * BATCH, D_SSM), bw, lam8, cw_re, cw_im, row(ssm_d[0]), glu_w[0].astype(_bf16),
              row(glu_b[0]), row(g_out_ssm[0]), _head_mean_matrix(D_SSM, SSM_GROUP))

    return _ffn(x, ya.reshape(SEQ, BATCH * D_SSM), yb, mod3, w_out[0].astype(_bf16), row(g_post_mix[0]),
                row(g_pre_ffn[0]), _interleave_ff(w_up[0]).astype(_bf16), _interleave_ff(ffn_conv_w[0]),
                w_down[0].astype(_bf16), row(g_post_ffn[0]))
```

```python
import jax
import jax.numpy as jnp
from jax import lax
from jax.experimental import pallas as pl
from jax.experimental.pallas import tpu as pltpu

D_MODEL = 1024
BATCH = 8
SEQ = 4096
D_SSM = 512
D_CONV = 512
SSM_GROUP = 16
N_SSM_GROUPS = 32
SSM_STATE = 64
CONV_HEADS = 8
D_FF = 2816
N_MOD = 6
D_IN_PROJ = 2048
EPS = 1e-6
LAMBDA_RE_MAX = -1e-4

N_STATE = N_SSM_GROUPS * SSM_STATE
SUBLANES = 8
LANES = 128
GROUPS_PER_LANE_TILE = LANES // SSM_GROUP
N_LANE_TILES_U = D_SSM // LANES
STATE_PER_TILE = GROUPS_PER_LANE_TILE * SSM_STATE

T_PRE = 512
T_SSM = 128
T_FFN = 512
FF_CHUNK = 256
N_FF_CHUNKS = D_FF // FF_CHUNK
SCAN_SLABS = 4
N_SLABS = N_STATE // LANES
SLABS_PER_TILE = STATE_PER_TILE // LANES
ROW_PITCH = T_SSM + SUBLANES
VMEM_LIMIT = 56 * 1024 * 1024

_f32 = jnp.float32
_bf16 = jnp.bfloat16


def _const_spec(shape):
    nd = len(shape)
    return pl.BlockSpec(shape, lambda *_: (0,) * nd, pipeline_mode=pl.Buffered(1))


def _rms(x, g):
    ms = jnp.mean(x * x, axis=-1, keepdims=True)
    return x * lax.rsqrt(ms + EPS) * g


def _mod_kernel(c_ref, w_ref, b_ref, o_ref):
    c = c_ref[...]
    c_act = c * jax.nn.sigmoid(c)
    o_ref[...] = jnp.dot(c_act, w_ref[...], preferred_element_type=_f32) + b_ref[...]


def _modulation(c, w_ada, b_ada):
    n_tile = D_MODEL
    return pl.pallas_call(
        _mod_kernel,
        grid=(N_MOD * D_MODEL // n_tile,),
        in_specs=[
            pl.BlockSpec((BATCH, D_MODEL), lambda j: (0, 0)),
            pl.BlockSpec((D_MODEL, n_tile), lambda j: (0, j)),
            pl.BlockSpec((1, n_tile), lambda j: (0, j)),
        ],
        out_specs=pl.BlockSpec((BATCH, n_tile), lambda j: (0, j)),
        out_shape=jax.ShapeDtypeStruct((BATCH, N_MOD * D_MODEL), _f32),
        name="adaln_mod",
    )(c, w_ada, b_ada.reshape(1, -1))


def _prep_kernel(lre_ref, lim_ref, lstep_ref, bre_ref, bim_ref, abre_ref, abim_ref, bbre_ref, bbim_ref):
    lre = jnp.minimum(lre_ref[...], LAMBDA_RE_MAX)
    lim = lim_ref[...]
    step = jnp.exp(lstep_ref[...])
    mag = jnp.exp(lre * step)
    ang = lim * step
    are = mag * jnp.cos(ang)
    aim = mag * jnp.sin(ang)
    abre_ref[...] = are
    abim_ref[...] = aim
    nre = are - 1.0
    nim = aim
    den = lre * lre + lim * lim
    cre = (nre * lre + nim * lim) / den
    cim = (nim * lre - nre * lim) / den
    bre = bre_ref[...]
    bim = bim_ref[...]
    bbre_ref[...] = cre * bre - cim * bim
    bbim_ref[...] = cre * bim + cim * bre


def _ssm_prep(lam_re, lam_im, log_step, b_re, b_im):
    col = lambda a: a.reshape(N_STATE, 1)
    lstep = jnp.broadcast_to(log_step[:, None], (N_SSM_GROUPS, SSM_STATE))
    outs = pl.pallas_call(
        _prep_kernel,
        out_shape=(
            jax.ShapeDtypeStruct((N_STATE, 1), _f32),
            jax.ShapeDtypeStruct((N_STATE, 1), _f32),
            jax.ShapeDtypeStruct((N_STATE, SSM_GROUP), _f32),
            jax.ShapeDtypeStruct((N_STATE, SSM_GROUP), _f32),
        ),
        name="s5_discretise",
    )(col(lam_re), col(lam_im), col(lstep), b_re.reshape(N_STATE, SSM_GROUP), b_im.reshape(N_STATE, SSM_GROUP))
    return outs


def _group_block_diag(t):
    eye = jnp.eye(GROUPS_PER_LANE_TILE, dtype=bool)
    nj, ng, r, c = t.shape
    blk = jnp.where(eye[None, :, None, :, None], t[:, :, :, None, :], 0.0)
    return blk.reshape(nj, ng * r, ng * c)


def _premix_kernel(x_ref, mod_ref, g_ref, win_ref, cw_ref, gconv_ref, ones_ref,
                   u_ref, yb_ref, buf_ref):
    ti = pl.program_id(1)

    @pl.when(ti == 0)
    def _():
        buf_ref[0:SUBLANES, :] = jnp.zeros((SUBLANES, D_CONV), _f32)

    x = x_ref[0]
    sh = mod_ref[0, 0:1, :]
    sc = mod_ref[0, 1:2, :]
    h = _rms(x, g_ref[...]) * (1.0 + sc) + sh
    proj = jnp.dot(h.astype(_bf16), win_ref[...], preferred_element_type=_f32)
    u_ref[0] = proj[:, 0:D_SSM].astype(_bf16)
    bg = proj[:, D_SSM:D_SSM + D_CONV]
    cg = proj[:, D_SSM + D_CONV:D_SSM + 2 * D_CONV]
    v = proj[:, D_SSM + 2 * D_CONV:]
    cv = cg * v
    buf_ref[SUBLANES:SUBLANES + T_PRE, :] = cv
    c1 = buf_ref[SUBLANES - 1:SUBLANES - 1 + T_PRE, :]
    c2 = buf_ref[SUBLANES - 2:SUBLANES - 2 + T_PRE, :]
    buf_ref[0:SUBLANES, :] = buf_ref[T_PRE:T_PRE + SUBLANES, :]
    yb = bg * (cw_ref[0:1, :] * c2 + cw_ref[1:2, :] * c1 + cw_ref[2:3, :] * cv)
    ms = jnp.dot((yb * yb).astype(_bf16), ones_ref[...], preferred_element_type=_f32)
    yb_ref[0] = (yb * lax.rsqrt(ms + EPS) * gconv_ref[...]).astype(_bf16)


def _premix(x, mod3, g_pre, w_in_b, conv_w, g_out_conv, ones_conv):
    n_t = SEQ // T_PRE
    return pl.pallas_call(
        _premix_kernel,
        grid=(BATCH, n_t),
        in_specs=[
            pl.BlockSpec((1, T_PRE, D_MODEL), lambda b, t: (b, t, 0)),
            pl.BlockSpec((1, N_MOD, D_MODEL), lambda b, t: (b, 0, 0)),
            _const_spec((1, D_MODEL)),
            _const_spec((D_MODEL, D_IN_PROJ)),
            _const_spec((3, D_CONV)),
            _const_spec((1, D_CONV)),
            _const_spec((D_CONV, D_CONV)),
        ],
        out_specs=[
            pl.BlockSpec((1, T_PRE, D_SSM), lambda b, t: (b, t, 0)),
            pl.BlockSpec((1, T_PRE, D_CONV), lambda b, t: (b, t, 0)),
        ],
        out_shape=[
            jax.ShapeDtypeStruct((BATCH, SEQ, D_SSM), _bf16),
            jax.ShapeDtypeStruct((BATCH, SEQ, D_CONV), _bf16),
        ],
        scratch_shapes=[pltpu.VMEM((T_PRE + SUBLANES, D_CONV), _f32)],
        compiler_params=pltpu.CompilerParams(
            dimension_semantics=("arbitrary", "arbitrary"), vmem_limit_bytes=VMEM_LIMIT),
        name="premix",
    )(x, mod3, g_pre, w_in_b, conv_w, g_out_conv, ones_conv)


def _ssm_kernel(u_ref, bw_ref, lam_ref, cwre_ref, cwim_ref, d_ref, gluw_ref, glub_ref, g_ref, ones_ref,
                y_ref, xre_ref, xim_ref, st_ref):
    i = pl.program_id(0)

    @pl.when(i == 0)
    def _():
        st_ref[...] = jnp.zeros(st_ref.shape, _f32)
        xre_ref[...] = jnp.zeros(xre_ref.shape, _f32)
        xim_ref[...] = jnp.zeros(xim_ref.shape, _f32)

    u = u_ref[...].reshape(BATCH * T_SSM, D_SSM)
    for j in range(N_LANE_TILES_U):
        bu = jnp.dot(u[:, j * LANES:(j + 1) * LANES], bw_ref[j], preferred_element_type=_f32)
        for b in range(BATCH):
            src = slice(b * T_SSM, (b + 1) * T_SSM)
            dst = slice(b * ROW_PITCH, b * ROW_PITCH + T_SSM)
            for k in range(SLABS_PER_TILE):
                s = j * SLABS_PER_TILE + k
                xre_ref[s, dst, :] = bu[src, k * LANES:(k + 1) * LANES]
                xim_ref[s, dst, :] = bu[src, STATE_PER_TILE + k * LANES:STATE_PER_TILE + (k + 1) * LANES]

    for cb in range(N_SLABS // SCAN_SLABS):
        slabs = [cb * SCAN_SLABS + k for k in range(SCAN_SLABS)]
        lam = [(lam_ref[0, :, s * LANES:(s + 1) * LANES], lam_ref[1, :, s * LANES:(s + 1) * LANES]) for s in slabs]

        def body(t, carry, slabs=slabs, lam=lam):
            out = []
            for s, (lre, lim), (sre, sim) in zip(slabs, lam, carry):
                rows = pl.ds(t, BATCH, stride=ROW_PITCH)
                nre = lre * sre - lim * sim + xre_ref[s, rows, :]
                nim = lre * sim + lim * sre + xim_ref[s, rows, :]
                xre_ref[s, rows, :] = nre
                xim_ref[s, rows, :] = nim
                out.append((nre, nim))
            return tuple(out)

        init = tuple((st_ref[0, :, s * LANES:(s + 1) * LANES], st_ref[1, :, s * LANES:(s + 1) * LANES])
                     for s in slabs)
        fin = lax.fori_loop(0, T_SSM, body, init, unroll=4)
        for s, (sre, sim) in zip(slabs, fin):
            st_ref[0, :, s * LANES:(s + 1) * LANES] = sre
            st_ref[1, :, s * LANES:(s + 1) * LANES] = sim

    ys = []
    for j in range(N_LANE_TILES_U):
        slabs = range(j * SLABS_PER_TILE, (j + 1) * SLABS_PER_TILE)
        x_re = jnp.concatenate([xre_ref[s] for s in slabs], axis=-1).astype(_bf16)
        x_im = jnp.concatenate([xim_ref[s] for s in slabs], axis=-1).astype(_bf16)
        yj = jnp.dot(x_re, cwre_ref[j], preferred_element_type=_f32)
        ys.append(yj - jnp.dot(x_im, cwim_ref[j], preferred_element_type=_f32))
    y = jnp.concatenate(ys, axis=-1)
    y = jnp.concatenate([y[b * ROW_PITCH:b * ROW_PITCH + T_SSM] for b in range(BATCH)], axis=0)
    y = y + d_ref[...] * u.astype(_f32)
    z = jax.nn.gelu(y)
    gate = jnp.dot(z.astype(_bf16), gluw_ref[...], preferred_element_type=_f32) + glub_ref[...]
    z = z * jax.nn.sigmoid(gate)
    ms = jnp.dot((z * z).astype(_bf16), ones_ref[...], preferred_element_type=_f32)
    out = (z * lax.rsqrt(ms + EPS) * g_ref[...]).astype(_bf16)
    y_ref[...] = out.reshape(BATCH, T_SSM, D_SSM)


def _ssm(u, bw, lam8, cw_re, cw_im, d_skip, glu_w_b, glu_b, g_out_ssm, ones_ssm):
    return pl.pallas_call(
        _ssm_kernel,
        grid=(SEQ // T_SSM,),
        in_specs=[
            pl.BlockSpec((BATCH, T_SSM, D_SSM), lambda i: (0, i, 0)),
            _const_spec((N_LANE_TILES_U, LANES, 2 * STATE_PER_TILE)),
            _const_spec((2, SUBLANES, N_STATE)),
            _const_spec((N_LANE_TILES_U, STATE_PER_TILE, LANES)),
            _const_spec((N_LANE_TILES_U, STATE_PER_TILE, LANES)),
            _const_spec((1, D_SSM)),
            _const_spec((D_SSM, D_SSM)),
            _const_spec((1, D_SSM)),
            _const_spec((1, D_SSM)),
            _const_spec((D_SSM, D_SSM)),
        ],
        out_specs=pl.BlockSpec((BATCH, T_SSM, D_SSM), lambda i: (0, i, 0)),
        out_shape=jax.ShapeDtypeStruct((BATCH, SEQ, D_SSM), _bf16),
        scratch_shapes=[
            pltpu.VMEM((N_SLABS, BATCH * ROW_PITCH, LANES), _f32),
            pltpu.VMEM((N_SLABS, BATCH * ROW_PITCH, LANES), _f32),
            pltpu.VMEM((2, SUBLANES, N_STATE), _f32),
        ],
        compiler_params=pltpu.CompilerParams(
            dimension_semantics=("arbitrary",), vmem_limit_bytes=VMEM_LIMIT),
        name="s5_mixer",
    )(u, bw, lam8, cw_re, cw_im, d_skip, glu_w_b, glu_b, g_out_ssm, ones_ssm)


def _ffn_kernel(x_ref, ya_ref, yb_ref, mod_ref, wout_ref, gpm_ref, gpf_ref, wup_ref, fcw_ref, wdn_ref, gpo_ref,
                o_ref, buf_ref, halo_ref):
    ti = pl.program_id(1)

    @pl.when(ti == 0)
    def _():
        halo_ref[...] = jnp.zeros(halo_ref.shape, _f32)

    x = x_ref[0]
    gt1 = mod_ref[0, 2:3, :]
    sh2 = mod_ref[0, 3:4, :]
    sc2 = mod_ref[0, 4:5, :]
    gt2 = mod_ref[0, 5:6, :]
    mix = jnp.dot(ya_ref[0], wout_ref[0:D_SSM, :], preferred_element_type=_f32)
    mix = mix + jnp.dot(yb_ref[0], wout_ref[D_SSM:, :], preferred_element_type=_f32)
    x1 = x + gt1 * _rms(mix, gpm_ref[...])
    h2 = (_rms(x1, gpf_ref[...]) * (1.0 + sc2) + sh2).astype(_bf16)

    def conv_chunk(cols, half):
        lanes = slice(half * FF_CHUNK, (half + 1) * FF_CHUNK)
        hc = jnp.dot(h2, wup_ref[:, cols], preferred_element_type=_f32)
        buf_ref[0:SUBLANES, lanes] = halo_ref[:, cols]
        buf_ref[SUBLANES:SUBLANES + T_FFN, lanes] = hc
        h1 = buf_ref[SUBLANES - 1:SUBLANES - 1 + T_FFN, lanes]
        h2s = buf_ref[SUBLANES - 2:SUBLANES - 2 + T_FFN, lanes]
        halo_ref[:, cols] = buf_ref[T_FFN:T_FFN + SUBLANES, lanes]
        return fcw_ref[0:1, cols] * h2s + fcw_ref[1:2, cols] * h1 + fcw_ref[2:3, cols] * hc

    acc = jnp.zeros((T_FFN, D_MODEL), _f32)
    for c in range(N_FF_CHUNKS):
        a = conv_chunk(slice(c * FF_CHUNK, (c + 1) * FF_CHUNK), 0)
        v = conv_chunk(slice(D_FF + c * FF_CHUNK, D_FF + (c + 1) * FF_CHUNK), 1)
        act = (a * jax.nn.sigmoid(a) * v).astype(_bf16)
        acc = acc + jnp.dot(act, wdn_ref[c * FF_CHUNK:(c + 1) * FF_CHUNK, :], preferred_element_type=_f32)
    o_ref[0] = x1 + gt2 * _rms(acc, gpo_ref[...])


def _ffn(x, ya, yb, mod3, w_out_b, g_post_mix, g_pre_ffn, w_up_b, ffn_cw, w_down_b, g_post_ffn):
    n_t = SEQ // T_FFN
    return pl.pallas_call(
        _ffn_kernel,
        grid=(BATCH, n_t),
        in_specs=[
            pl.BlockSpec((1, T_FFN, D_MODEL), lambda b, t: (b, t, 0)),
            pl.BlockSpec((1, T_FFN, D_SSM), lambda b, t: (b, t, 0)),
            pl.BlockSpec((1, T_FFN, D_CONV), lambda b, t: (b, t, 0)),
            pl.BlockSpec((1, N_MOD, D_MODEL), lambda b, t: (b, 0, 0)),
            _const_spec((D_MODEL, D_MODEL)),
            _const_spec((1, D_MODEL)),
            _const_spec((1, D_MODEL)),
            _const_spec((D_MODEL, 2 * D_FF)),
            _const_spec((3, 2 * D_FF)),
            _const_spec((D_FF, D_MODEL)),
            _const_spec((1, D_MODEL)),
        ],
        out_specs=pl.BlockSpec((1, T_FFN, D_MODEL), lambda b, t: (b, t, 0)),
        out_shape=jax.ShapeDtypeStruct((BATCH, SEQ, D_MODEL), _f32),
        scratch_shapes=[
            pltpu.VMEM((T_FFN + SUBLANES, 2 * FF_CHUNK), _f32),
            pltpu.VMEM((SUBLANES, 2 * D_FF), _f32),
        ],
        compiler_params=pltpu.CompilerParams(
            dimension_semantics=("arbitrary", "arbitrary"), vmem_limit_bytes=VMEM_LIMIT),
        name="outproj_convffn",
    )(x, ya, yb, mod3, w_out_b, g_post_mix, g_pre_ffn, w_up_b, ffn_cw, w_down_b, g_post_ffn)


def _head_mean_matrix(width, head):
    idx = jnp.arange(width) // head
    return jnp.where(idx[:, None] == idx[None, :], 1.0 / head, 0.0).astype(_bf16)


def kernel(x, c, w_ada, b_ada, g_pre_mix, g_post_mix, w_in, ssm_lam_re, ssm_lam_im, ssm_log_step, ssm_b_re, ssm_b_im, ssm_c_re, ssm_c_im, ssm_d, glu_w, glu_b, g_out_ssm, conv_w, g_out_conv, w_out, g_pre_ffn, g_post_ffn, w_up, ffn_conv_w, w_down):
    assert x.shape == (BATCH, SEQ, D_MODEL) and w_ada.shape[0] == 1
    row = lambda a: a.reshape(1, -1)

    mod3 = _modulation(c, w_ada[0], b_ada[0]).reshape(BATCH, N_MOD, D_MODEL)

    ab_re, ab_im, bb_re, bb_im = _ssm_prep(ssm_lam_re[0], ssm_lam_im[0], ssm_log_step[0], ssm_b_re[0], ssm_b_im[0])
    lam8 = jnp.broadcast_to(jnp.stack([ab_re.reshape(1, N_STATE), ab_im.reshape(1, N_STATE)]),
                            (2, SUBLANES, N_STATE))
    tb = lambda a: a.reshape(N_LANE_TILES_U, GROUPS_PER_LANE_TILE, SSM_STATE, SSM_GROUP).transpose(0, 1, 3, 2)
    bw = jnp.concatenate([_group_block_diag(tb(bb_re)), _group_block_diag(tb(bb_im))], axis=-1).astype(_bf16)
    tc = lambda a: a.reshape(N_LANE_TILES_U, GROUPS_PER_LANE_TILE, SSM_GROUP, SSM_STATE).transpose(0, 1, 3, 2)
    cw_re = _group_block_diag(tc(ssm_c_re[0])).astype(_bf16)
    cw_im = _group_block_diag(tc(ssm_c_im[0])).astype(_bf16)

    u, yb = _premix(x, mod3, row(g_pre_mix[0]), w_in[0].astype(_bf16), conv_w[0], row(g_out_conv[0]),
                    _head_mean_matrix(D_CONV, D_CONV // CONV_HEADS))

    ya = _ssm(u, bw, lam8, cw_re, cw_im, row(ssm_d[0]), glu_w[0].astype(_bf16),
              row(glu_b[0]), row(g_out_ssm[0]), _head_mean_matrix(D_SSM, SSM_GROUP))

    return _ffn(x, ya, yb, mod3, w_out[0].astype(_bf16), row(g_post_mix[0]), row(g_pre_ffn[0]),
                w_up[0].astype(_bf16), ffn_conv_w[0], w_down[0].astype(_bf16), row(g_post_ffn[0]))
```

```python
import jax
import jax.numpy as jnp
from jax import lax
from jax.experimental import pallas as pl
from jax.experimental.pallas import tpu as pltpu

D_MODEL = 1024
BATCH = 8
SEQ = 4096
D_SSM = 512
D_CONV = 512
SSM_GROUP = 16
N_SSM_GROUPS = 32
SSM_STATE = 64
CONV_HEADS = 8
D_FF = 2816
N_MOD = 6
D_IN_PROJ = 2048
EPS = 1e-6
LAMBDA_RE_MAX = -1e-4

N_STATE = N_SSM_GROUPS * SSM_STATE
SUBLANES = 8
LANES = 128
GROUPS_PER_LANE_TILE = LANES // SSM_GROUP
N_LANE_TILES_U = D_SSM // LANES
STATE_PER_TILE = GROUPS_PER_LANE_TILE * SSM_STATE

T_PRE = 512
T_SSM = 128
T_FFN = 512
FF_CHUNK = 256
N_FF_CHUNKS = D_FF // FF_CHUNK
SCAN_SLABS = 4
N_SLABS = N_STATE // LANES
SLABS_PER_TILE = STATE_PER_TILE // LANES
ROW_PITCH = T_SSM + SUBLANES
VMEM_LIMIT = 56 * 1024 * 1024

_f32 = jnp.float32
_bf16 = jnp.bfloat16


def _const_spec(shape):
    nd = len(shape)
    return pl.BlockSpec(shape, lambda *_: (0,) * nd, pipeline_mode=pl.Buffered(1))


def _rms(x, g):
    ms = jnp.mean(x * x, axis=-1, keepdims=True)
    return x * lax.rsqrt(ms + EPS) * g


def _shift_rows(h, halo, k):
    rolled = pltpu.roll(h, k, axis=0)
    row = lax.broadcasted_iota(jnp.int32, halo.shape, 0)
    top = jnp.where(row < k, pltpu.roll(halo, k, axis=0), rolled[0:SUBLANES])
    return jnp.concatenate([top, rolled[SUBLANES:]], axis=0)


def _mod_kernel(c_ref, w_ref, b_ref, o_ref):
    c = c_ref[...]
    c_act = c * jax.nn.sigmoid(c)
    o_ref[...] = jnp.dot(c_act, w_ref[...], preferred_element_type=_f32) + b_ref[...]


def _modulation(c, w_ada, b_ada):
    n_tile = D_MODEL
    return pl.pallas_call(
        _mod_kernel,
        grid=(N_MOD * D_MODEL // n_tile,),
        in_specs=[
            pl.BlockSpec((BATCH, D_MODEL), lambda j: (0, 0)),
            pl.BlockSpec((D_MODEL, n_tile), lambda j: (0, j)),
            pl.BlockSpec((1, n_tile), lambda j: (0, j)),
        ],
        out_specs=pl.BlockSpec((BATCH, n_tile), lambda j: (0, j)),
        out_shape=jax.ShapeDtypeStruct((BATCH, N_MOD * D_MODEL), _f32),
        name="adaln_mod",
    )(c, w_ada, b_ada.reshape(1, -1))


def _prep_kernel(lre_ref, lim_ref, lstep_ref, bre_ref, bim_ref, abre_ref, abim_ref, bbre_ref, bbim_ref):
    lre = jnp.minimum(lre_ref[...], LAMBDA_RE_MAX)
    lim = lim_ref[...]
    step = jnp.exp(lstep_ref[...])
    mag = jnp.exp(lre * step)
    ang = lim * step
    are = mag * jnp.cos(ang)
    aim = mag * jnp.sin(ang)
    abre_ref[...] = are
    abim_ref[...] = aim
    nre = are - 1.0
    nim = aim
    den = lre * lre + lim * lim
    cre = (nre * lre + nim * lim) / den
    cim = (nim * lre - nre * lim) / den
    bre = bre_ref[...]
    bim = bim_ref[...]
    bbre_ref[...] = cre * bre - cim * bim
    bbim_ref[...] = cre * bim + cim * bre


def _ssm_prep(lam_re, lam_im, log_step, b_re, b_im):
    col = lambda a: a.reshape(N_STATE, 1)
    lstep = jnp.broadcast_to(log_step[:, None], (N_SSM_GROUPS, SSM_STATE))
    outs = pl.pallas_call(
        _prep_kernel,
        out_shape=(
            jax.ShapeDtypeStruct((N_STATE, 1), _f32),
            jax.ShapeDtypeStruct((N_STATE, 1), _f32),
            jax.ShapeDtypeStruct((N_STATE, SSM_GROUP), _f32),
            jax.ShapeDtypeStruct((N_STATE, SSM_GROUP), _f32),
        ),
        name="s5_discretise",
    )(col(lam_re), col(lam_im), col(lstep), b_re.reshape(N_STATE, SSM_GROUP), b_im.reshape(N_STATE, SSM_GROUP))
    return outs


def _group_block_diag(t):
    eye = jnp.eye(GROUPS_PER_LANE_TILE, dtype=bool)
    nj, ng, r, c = t.shape
    blk = jnp.where(eye[None, :, None, :, None], t[:, :, :, None, :], 0.0)
    return blk.reshape(nj, ng * r, ng * c)


def _premix_kernel(x_ref, mod_ref, g_ref, win_ref, cw_ref, gconv_ref, ones_ref,
                   u_ref, yb_ref, buf_ref):
    ti = pl.program_id(1)

    @pl.when(ti == 0)
    def _():
        buf_ref[0:SUBLANES, :] = jnp.zeros((SUBLANES, D_CONV), _f32)

    x = x_ref[0]
    sh = mod_ref[0, 0:1, :]
    sc = mod_ref[0, 1:2, :]
    h = _rms(x, g_ref[...]) * (1.0 + sc) + sh
    proj = jnp.dot(h.astype(_bf16), win_ref[...], preferred_element_type=_f32)
    u_ref[0] = proj[:, 0:D_SSM].astype(_bf16)
    bg = proj[:, D_SSM:D_SSM + D_CONV]
    cg = proj[:, D_SSM + D_CONV:D_SSM + 2 * D_CONV]
    v = proj[:, D_SSM + 2 * D_CONV:]
    cv = cg * v
    buf_ref[SUBLANES:SUBLANES + T_PRE, :] = cv
    c1 = buf_ref[SUBLANES - 1:SUBLANES - 1 + T_PRE, :]
    c2 = buf_ref[SUBLANES - 2:SUBLANES - 2 + T_PRE, :]
    buf_ref[0:SUBLANES, :] = buf_ref[T_PRE:T_PRE + SUBLANES, :]
    yb = bg * (cw_ref[0:1, :] * c2 + cw_ref[1:2, :] * c1 + cw_ref[2:3, :] * cv)
    ms = jnp.dot((yb * yb).astype(_bf16), ones_ref[...], preferred_element_type=_f32)
    yb_ref[0] = (yb * lax.rsqrt(ms + EPS) * gconv_ref[...]).astype(_bf16)


def _premix(x, mod3, g_pre, w_in_b, conv_w, g_out_conv, ones_conv):
    n_t = SEQ // T_PRE
    return pl.pallas_call(
        _premix_kernel,
        grid=(BATCH, n_t),
        in_specs=[
            pl.BlockSpec((1, T_PRE, D_MODEL), lambda b, t: (b, t, 0)),
            pl.BlockSpec((1, N_MOD, D_MODEL), lambda b, t: (b, 0, 0)),
            _const_spec((1, D_MODEL)),
            _const_spec((D_MODEL, D_IN_PROJ)),
            _const_spec((3, D_CONV)),
            _const_spec((1, D_CONV)),
            _const_spec((D_CONV, D_CONV)),
        ],
        out_specs=[
            pl.BlockSpec((1, T_PRE, D_SSM), lambda b, t: (b, t, 0)),
            pl.BlockSpec((1, T_PRE, D_CONV), lambda b, t: (b, t, 0)),
        ],
        out_shape=[
            jax.ShapeDtypeStruct((BATCH, SEQ, D_SSM), _bf16),
            jax.ShapeDtypeStruct((BATCH, SEQ, D_CONV), _bf16),
        ],
        scratch_shapes=[pltpu.VMEM((T_PRE + SUBLANES, D_CONV), _f32)],
        compiler_params=pltpu.CompilerParams(
            dimension_semantics=("arbitrary", "arbitrary"), vmem_limit_bytes=VMEM_LIMIT),
        name="premix",
    )(x, mod3, g_pre, w_in_b, conv_w, g_out_conv, ones_conv)


def _ssm_kernel(u_ref, bw_ref, lam_ref, cwre_ref, cwim_ref, d_ref, gluw_ref, glub_ref, g_ref, ones_ref,
                y_ref, xre_ref, xim_ref, st_ref):
    i = pl.program_id(0)

    @pl.when(i == 0)
    def _():
        st_ref[...] = jnp.zeros(st_ref.shape, _f32)
        xre_ref[...] = jnp.zeros(xre_ref.shape, _f32)
        xim_ref[...] = jnp.zeros(xim_ref.shape, _f32)

    u = u_ref[...].reshape(BATCH * T_SSM, D_SSM)
    for j in range(N_LANE_TILES_U):
        bu = jnp.dot(u[:, j * LANES:(j + 1) * LANES], bw_ref[j], preferred_element_type=_f32)
        for b in range(BATCH):
            src = slice(b * T_SSM, (b + 1) * T_SSM)
            dst = slice(b * ROW_PITCH, b * ROW_PITCH + T_SSM)
            for k in range(SLABS_PER_TILE):
                s = j * SLABS_PER_TILE + k
                xre_ref[s, dst, :] = bu[src, k * LANES:(k + 1) * LANES]
                xim_ref[s, dst, :] = bu[src, STATE_PER_TILE + k * LANES:STATE_PER_TILE + (k + 1) * LANES]

    for cb in range(N_SLABS // SCAN_SLABS):
        slabs = [cb * SCAN_SLABS + k for k in range(SCAN_SLABS)]
        lam = [(lam_ref[0, :, s * LANES:(s + 1) * LANES], lam_ref[1, :, s * LANES:(s + 1) * LANES]) for s in slabs]

        def body(t, carry, slabs=slabs, lam=lam):
            out = []
            for s, (lre, lim), (sre, sim) in zip(slabs, lam, carry):
                rows = pl.ds(t, BATCH, stride=ROW_PITCH)
                nre = lre * sre - lim * sim + xre_ref[s, rows, :]
                nim = lre * sim + lim * sre + xim_ref[s, rows, :]
                xre_ref[s, rows, :] = nre
                xim_ref[s, rows, :] = nim
                out.append((nre, nim))
            return tuple(out)

        init = tuple((st_ref[0, :, s * LANES:(s + 1) * LANES], st_ref[1, :, s * LANES:(s + 1) * LANES])
                     for s in slabs)
        fin = lax.fori_loop(0, T_SSM, body, init, unroll=4)
        for s, (sre, sim) in zip(slabs, fin):
            st_ref[0, :, s * LANES:(s + 1) * LANES] = sre
            st_ref[1, :, s * LANES:(s + 1) * LANES] = sim

    ys = []
    for j in range(N_LANE_TILES_U):
        slabs = range(j * SLABS_PER_TILE, (j + 1) * SLABS_PER_TILE)
        x_re = jnp.concatenate([xre_ref[s] for s in slabs], axis=-1).astype(_bf16)
        x_im = jnp.concatenate([xim_ref[s] for s in slabs], axis=-1).astype(_bf16)
        yj = jnp.dot(x_re, cwre_ref[j], preferred_element_type=_f32)
        ys.append(yj - jnp.dot(x_im, cwim_ref[j], preferred_element_type=_f32))
    y = jnp.concatenate(ys, axis=-1)
    y = jnp.concatenate([y[b * ROW_PITCH:b * ROW_PITCH + T_SSM] for b in range(BATCH)], axis=0)
    y = y + d_ref[...] * u.astype(_f32)
    z = jax.nn.gelu(y)
    gate = jnp.dot(z.astype(_bf16), gluw_ref[...], preferred_element_type=_f32) + glub_ref[...]
    z = z * jax.nn.sigmoid(gate)
    ms = jnp.dot((z * z).astype(_bf16), ones_ref[...], preferred_element_type=_f32)
    out = (z * lax.rsqrt(ms + EPS) * g_ref[...]).astype(_bf16)
    y_ref[...] = out.reshape(BATCH, T_SSM, D_SSM)


def _ssm(u, bw, lam8, cw_re, cw_im, d_skip, glu_w_b, glu_b, g_out_ssm, ones_ssm):
    return pl.pallas_call(
        _ssm_kernel,
        grid=(SEQ // T_SSM,),
        in_specs=[
            pl.BlockSpec((BATCH, T_SSM, D_SSM), lambda i: (0, i, 0)),
            _const_spec((N_LANE_TILES_U, LANES, 2 * STATE_PER_TILE)),
            _const_spec((2, SUBLANES, N_STATE)),
            _const_spec((N_LANE_TILES_U, STATE_PER_TILE, LANES)),
            _const_spec((N_LANE_TILES_U, STATE_PER_TILE, LANES)),
            _const_spec((1, D_SSM)),
            _const_spec((D_SSM, D_SSM)),
            _const_spec((1, D_SSM)),
            _const_spec((1, D_SSM)),
            _const_spec((D_SSM, D_SSM)),
        ],
        out_specs=pl.BlockSpec((BATCH, T_SSM, D_SSM), lambda i: (0, i, 0)),
        out_shape=jax.ShapeDtypeStruct((BATCH, SEQ, D_SSM), _bf16),
        scratch_shapes=[
            pltpu.VMEM((N_SLABS, BATCH * ROW_PITCH, LANES), _f32),
            pltpu.VMEM((N_SLABS, BATCH * ROW_PITCH, LANES), _f32),
            pltpu.VMEM((2, SUBLANES, N_STATE), _f32),
        ],
        compiler_params=pltpu.CompilerParams(
            dimension_semantics=("arbitrary",), vmem_limit_bytes=VMEM_LIMIT),
        name="s5_mixer",
    )(u, bw, lam8, cw_re, cw_im, d_skip, glu_w_b, glu_b, g_out_ssm, ones_ssm)


def _ffn_kernel(x_ref, ya_ref, yb_ref, mod_ref, wout_ref, gpm_ref, gpf_ref, wup_ref, fcw_ref, wdn_ref, gpo_ref,
                o_ref, halo_ref, act_ref):
    ti = pl.program_id(1)

    @pl.when(ti == 0)
    def _():
        halo_ref[...] = jnp.zeros(halo_ref.shape, _f32)

    x = x_ref[0]
    gt1 = mod_ref[0, 2:3, :]
    sh2 = mod_ref[0, 3:4, :]
    sc2 = mod_ref[0, 4:5, :]
    gt2 = mod_ref[0, 5:6, :]
    mix = jnp.dot(ya_ref[0], wout_ref[0:D_SSM, :], preferred_element_type=_f32)
    mix = mix + jnp.dot(yb_ref[0], wout_ref[D_SSM:, :], preferred_element_type=_f32)
    x1 = x + gt1 * _rms(mix, gpm_ref[...])
    h2 = (_rms(x1, gpf_ref[...]) * (1.0 + sc2) + sh2).astype(_bf16)

    def up_chunk(c):
        cols_a = slice(c * FF_CHUNK, (c + 1) * FF_CHUNK)
        cols_v = slice(D_FF + c * FF_CHUNK, D_FF + (c + 1) * FF_CHUNK)
        return (jnp.dot(h2, wup_ref[:, cols_a], preferred_element_type=_f32), cols_a,
                jnp.dot(h2, wup_ref[:, cols_v], preferred_element_type=_f32), cols_v)

    def conv(hc, cols):
        halo = halo_ref[:, cols]
        halo_ref[:, cols] = hc[T_FFN - SUBLANES:, :]
        h1 = _shift_rows(hc, halo, 1)
        h2s = _shift_rows(hc, halo, 2)
        return fcw_ref[0:1, cols] * h2s + fcw_ref[1:2, cols] * h1 + fcw_ref[2:3, cols] * hc

    nxt = up_chunk(0)
    for c in range(N_FF_CHUNKS):
        ha, cols_a, hv, cols_v = nxt
        if c + 1 < N_FF_CHUNKS:
            nxt = up_chunk(c + 1)
        a = conv(ha, cols_a)
        v = conv(hv, cols_v)
        act_ref[:, cols_a] = (a * jax.nn.sigmoid(a) * v).astype(_bf16)
    down = jnp.dot(act_ref[...], wdn_ref[...], preferred_element_type=_f32)
    o_ref[0] = x1 + gt2 * _rms(down, gpo_ref[...])


def _ffn(x, ya, yb, mod3, w_out_b, g_post_mix, g_pre_ffn, w_up_b, ffn_cw, w_down_b, g_post_ffn):
    n_t = SEQ // T_FFN
    return pl.pallas_call(
        _ffn_kernel,
        grid=(BATCH, n_t),
        in_specs=[
            pl.BlockSpec((1, T_FFN, D_MODEL), lambda b, t: (b, t, 0)),
            pl.BlockSpec((1, T_FFN, D_SSM), lambda b, t: (b, t, 0)),
            pl.BlockSpec((1, T_FFN, D_CONV), lambda b, t: (b, t, 0)),
            pl.BlockSpec((1, N_MOD, D_MODEL), lambda b, t: (b, 0, 0)),
            _const_spec((D_MODEL, D_MODEL)),
            _const_spec((1, D_MODEL)),
            _const_spec((1, D_MODEL)),
            _const_spec((D_MODEL, 2 * D_FF)),
            _const_spec((3, 2 * D_FF)),
            _const_spec((D_FF, D_MODEL)),
            _const_spec((1, D_MODEL)),
        ],
        out_specs=pl.BlockSpec((1, T_FFN, D_MODEL), lambda b, t: (b, t, 0)),
        out_shape=jax.ShapeDtypeStruct((BATCH, SEQ, D_MODEL), _f32),
        scratch_shapes=[
            pltpu.VMEM((SUBLANES, 2 * D_FF), _f32),
            pltpu.VMEM((T_FFN, D_FF), _bf16),
        ],
        compiler_params=pltpu.CompilerParams(
            dimension_semantics=("arbitrary", "arbitrary"), vmem_limit_bytes=VMEM_LIMIT),
        name="outproj_convffn",
    )(x, ya, yb, mod3, w_out_b, g_post_mix, g_pre_ffn, w_up_b, ffn_cw, w_down_b, g_post_ffn)


def _head_mean_matrix(width, head):
    idx = jnp.arange(width) // head
    return jnp.where(idx[:, None] == idx[None, :], 1.0 / head, 0.0).astype(_bf16)


def kernel(x, c, w_ada, b_ada, g_pre_mix, g_post_mix, w_in, ssm_lam_re, ssm_lam_im, ssm_log_step, ssm_b_re, ssm_b_im, ssm_c_re, ssm_c_im, ssm_d, glu_w, glu_b, g_out_ssm, conv_w, g_out_conv, w_out, g_pre_ffn, g_post_ffn, w_up, ffn_conv_w, w_down):
    assert x.shape == (BATCH, SEQ, D_MODEL) and w_ada.shape[0] == 1
    row = lambda a: a.reshape(1, -1)

    mod3 = _modulation(c, w_ada[0], b_ada[0]).reshape(BATCH, N_MOD, D_MODEL)

    ab_re, ab_im, bb_re, bb_im = _ssm_prep(ssm_lam_re[0], ssm_lam_im[0], ssm_log_step[0], ssm_b_re[0], ssm_b_im[0])
    lam8 = jnp.broadcast_to(jnp.stack([ab_re.reshape(1, N_STATE), ab_im.reshape(1, N_STATE)]),
                            (2, SUBLANES, N_STATE))
    tb = lambda a: a.reshape(N_LANE_TILES_U, GROUPS_PER_LANE_TILE, SSM_STATE, SSM_GROUP).transpose(0, 1, 3, 2)
    bw = jnp.concatenate([_group_block_diag(tb(bb_re)), _group_block_diag(tb(bb_im))], axis=-1).astype(_bf16)
    tc = lambda a: a.reshape(N_LANE_TILES_U, GROUPS_PER_LANE_TILE, SSM_GROUP, SSM_STATE).transpose(0, 1, 3, 2)
    cw_re = _group_block_diag(tc(ssm_c_re[0])).astype(_bf16)
    cw_im = _group_block_diag(tc(ssm_c_im[0])).astype(_bf16)

    u, yb = _premix(x, mod3, row(g_pre_mix[0]), w_in[0].astype(_bf16), conv_w[0], row(g_out_conv[0]),
                    _head_mean_matrix(D_CONV, D_CONV // CONV_HEADS))

    ya = _ssm(u, bw, lam8, cw_re, cw_im, row(ssm_d[0]), glu_w[0].astype(_bf16),
              row(glu_b[0]), row(g_out_ssm[0]), _head_mean_matrix(D_SSM, SSM_GROUP))

    return _ffn(x, ya, yb, mod3, w_out[0].astype(_bf16), row(g_post_mix[0]), row(g_pre_ffn[0]),
                w_up[0].astype(_bf16), ffn_conv_w[0], w_down[0].astype(_bf16), row(g_post_ffn[0]))
```

```python
import jax
import jax.numpy as jnp
from jax import lax
from jax.experimental import pallas as pl
from jax.experimental.pallas import tpu as pltpu

D_MODEL = 1024
BATCH = 8
SEQ = 4096
D_SSM = 512
D_CONV = 512
SSM_GROUP = 16
N_SSM_GROUPS = 32
SSM_STATE = 64
CONV_HEADS = 8
D_FF = 2816
N_MOD = 6
D_IN_PROJ = 2048
EPS = 1e-6
LAMBDA_RE_MAX = -1e-4

N_STATE = N_SSM_GROUPS * SSM_STATE
SUBLANES = 8
LANES = 128
GROUPS_PER_LANE_TILE = LANES // SSM_GROUP
N_LANE_TILES_U = D_SSM // LANES
STATE_PER_TILE = GROUPS_PER_LANE_TILE * SSM_STATE

T_PRE = 512
T_SSM = 256
T_FFN = 512
FF_CHUNK = 256
N_FF_CHUNKS = D_FF // FF_CHUNK
SLABS_PER_TILE = STATE_PER_TILE // LANES
M_CHUNK = 4
N_CHUNKS = T_SSM // M_CHUNK
CHUNK_ROWS = N_CHUNKS * BATCH
CHUNK_LANES = M_CHUNK * LANES
TILE_STATE = 2 * STATE_PER_TILE
VMEM_LIMIT = 56 * 1024 * 1024

_f32 = jnp.float32
_bf16 = jnp.bfloat16


def _const_spec(shape):
    nd = len(shape)
    return pl.BlockSpec(shape, lambda *_: (0,) * nd, pipeline_mode=pl.Buffered(1))


def _rms(x, g):
    ms = jnp.mean(x * x, axis=-1, keepdims=True)
    return x * lax.rsqrt(ms + EPS) * g


def _shift_rows(h, halo, k):
    rolled = pltpu.roll(h, k, axis=0)
    row = lax.broadcasted_iota(jnp.int32, halo.shape, 0)
    top = jnp.where(row < k, pltpu.roll(halo, k, axis=0), rolled[0:SUBLANES])
    return jnp.concatenate([top, rolled[SUBLANES:]], axis=0)


def _mod_kernel(c_ref, w_ref, b_ref, o_ref):
    c = c_ref[...]
    c_act = c * jax.nn.sigmoid(c)
    o_ref[...] = jnp.dot(c_act, w_ref[...], preferred_element_type=_f32) + b_ref[...]


def _modulation(c, w_ada, b_ada):
    n_tile = D_MODEL
    return pl.pallas_call(
        _mod_kernel,
        grid=(N_MOD * D_MODEL // n_tile,),
        in_specs=[
            pl.BlockSpec((BATCH, D_MODEL), lambda j: (0, 0)),
            pl.BlockSpec((D_MODEL, n_tile), lambda j: (0, j)),
            pl.BlockSpec((1, n_tile), lambda j: (0, j)),
        ],
        out_specs=pl.BlockSpec((BATCH, n_tile), lambda j: (0, j)),
        out_shape=jax.ShapeDtypeStruct((BATCH, N_MOD * D_MODEL), _f32),
        name="adaln_mod",
    )(c, w_ada, b_ada.reshape(1, -1))


def _prep_kernel(lre_ref, lim_ref, lstep_ref, bre_ref, bim_ref, cre_ref, cim_ref,
                 lamm_ref, bs_ref, cl_ref, kt_ref):
    lre = jnp.minimum(lre_ref[...], LAMBDA_RE_MAX)
    lim = lim_ref[...]
    step = jnp.exp(lstep_ref[...])
    log_mag = lre * step
    ang = lim * step

    def lam_pow(k):
        mag = jnp.exp(k * log_mag)
        return mag * jnp.cos(k * ang), mag * jnp.sin(k * ang)

    a_re, a_im = lam_pow(1.0)
    n_re = a_re - 1.0
    den = lre * lre + lim * lim
    q_re = (n_re * lre + a_im * lim) / den
    q_im = (a_im * lre - n_re * lim) / den
    b_re, b_im = bre_ref[...], bim_ref[...]
    bb_re = q_re * b_re - q_im * b_im
    bb_im = q_re * b_im + q_im * b_re
    c_re, c_im = cre_ref[...], cim_ref[...]

    lamm_ref[0], lamm_ref[1] = lam_pow(float(M_CHUNK))
    for s in range(M_CHUNK):
        p_re, p_im = lam_pow(float(M_CHUNK - 1 - s))
        bs_ref[s, 0] = p_re * bb_re - p_im * bb_im
        bs_ref[s, 1] = p_re * bb_im + p_im * bb_re
        p_re, p_im = lam_pow(float(s + 1))
        cl_ref[s, 0] = p_re * c_re - p_im * c_im
        cl_ref[s, 1] = -(p_re * c_im + p_im * c_re)
    for tau in range(M_CHUNK):
        p_re, p_im = lam_pow(float(tau))
        ct_re = p_re * c_re - p_im * c_im
        ct_im = p_re * c_im + p_im * c_re
        for h in range(SSM_GROUP):
            prod = ct_re * bb_re[:, h:h + 1, :] - ct_im * bb_im[:, h:h + 1, :]
            kt_ref[tau, h] = jnp.sum(prod, axis=-1)


def _ssm_prep(lam_re, lam_im, log_step, b_re, b_im, c_re, c_im):
    ghp = (N_SSM_GROUPS, SSM_GROUP, SSM_STATE)
    g1p = lambda a: a.reshape(N_SSM_GROUPS, 1, SSM_STATE)
    lstep = jnp.broadcast_to(log_step[:, None], (N_SSM_GROUPS, SSM_STATE))
    return pl.pallas_call(
        _prep_kernel,
        out_shape=(
            jax.ShapeDtypeStruct((2, N_SSM_GROUPS, 1, SSM_STATE), _f32),
            jax.ShapeDtypeStruct((M_CHUNK, 2) + ghp, _f32),
            jax.ShapeDtypeStruct((M_CHUNK, 2) + ghp, _f32),
            jax.ShapeDtypeStruct((M_CHUNK, SSM_GROUP, N_SSM_GROUPS, SSM_GROUP), _f32),
        ),
        name="s5_discretise",
    )(g1p(lam_re), g1p(lam_im), g1p(lstep), b_re.transpose(0, 2, 1), b_im.transpose(0, 2, 1), c_re, c_im)


def _tile_block_diag(t, axis):
    eye = jnp.eye(GROUPS_PER_LANE_TILE, dtype=bool)
    shape = [1] * (t.ndim + 1)
    shape[2] = shape[axis] = GROUPS_PER_LANE_TILE
    return jnp.where(eye.reshape(shape), jnp.expand_dims(t, axis), 0.0)


def _ssm_matrices(lamm, bs, cl, kt):
    nj, ng, m = N_LANE_TILES_U, GROUPS_PER_LANE_TILE, M_CHUNK
    lam8 = jnp.broadcast_to(lamm.reshape(2, 1, N_STATE), (2, SUBLANES, N_STATE))
    t = bs.reshape(m, 2, nj, ng, SSM_GROUP, SSM_STATE).transpose(2, 0, 3, 4, 1, 5)
    bs_w = _tile_block_diag(t, 5).reshape(nj, CHUNK_LANES, TILE_STATE)
    t = cl.reshape(m, 2, nj, ng, SSM_GROUP, SSM_STATE).transpose(2, 1, 3, 5, 0, 4)
    cs_w = _tile_block_diag(t, 5).reshape(nj, TILE_STATE, CHUNK_LANES)
    tau = jnp.arange(m)[None, :] - jnp.arange(m)[:, None]
    t = jnp.where((tau >= 0)[:, :, None, None, None], kt[jnp.maximum(tau, 0)], 0.0)
    t = t.reshape(m, m, SSM_GROUP, nj, ng, SSM_GROUP).transpose(3, 0, 4, 2, 1, 5)
    tz_w = _tile_block_diag(t, 5).reshape(nj, CHUNK_LANES, CHUNK_LANES)
    return lam8, bs_w.astype(_bf16), cs_w.astype(_bf16), tz_w.astype(_bf16)


def _premix_kernel(x_ref, mod_ref, g_ref, win_ref, cw_ref, gconv_ref, ones_ref,
                   u_ref, yb_ref, buf_ref):
    ti = pl.program_id(1)

    @pl.when(ti == 0)
    def _():
        buf_ref[0:SUBLANES, :] = jnp.zeros((SUBLANES, D_CONV), _f32)

    x = x_ref[0]
    sh = mod_ref[0, 0:1, :]
    sc = mod_ref[0, 1:2, :]
    h = _rms(x, g_ref[...]) * (1.0 + sc) + sh
    proj = jnp.dot(h.astype(_bf16), win_ref[...], preferred_element_type=_f32)
    u_ref[0] = proj[:, 0:D_SSM].astype(_bf16)
    bg = proj[:, D_SSM:D_SSM + D_CONV]
    cg = proj[:, D_SSM + D_CONV:D_SSM + 2 * D_CONV]
    v = proj[:, D_SSM + 2 * D_CONV:]
    cv = cg * v
    buf_ref[SUBLANES:SUBLANES + T_PRE, :] = cv
    c1 = buf_ref[SUBLANES - 1:SUBLANES - 1 + T_PRE, :]
    c2 = buf_ref[SUBLANES - 2:SUBLANES - 2 + T_PRE, :]
    buf_ref[0:SUBLANES, :] = buf_ref[T_PRE:T_PRE + SUBLANES, :]
    yb = bg * (cw_ref[0:1, :] * c2 + cw_ref[1:2, :] * c1 + cw_ref[2:3, :] * cv)
    ms = jnp.dot((yb * yb).astype(_bf16), ones_ref[...], preferred_element_type=_f32)
    yb_ref[0] = (yb * lax.rsqrt(ms + EPS) * gconv_ref[...]).astype(_bf16)


def _premix(x, mod3, g_pre, w_in_b, conv_w, g_out_conv, ones_conv):
    n_t = SEQ // T_PRE
    return pl.pallas_call(
        _premix_kernel,
        grid=(BATCH, n_t),
        in_specs=[
            pl.BlockSpec((1, T_PRE, D_MODEL), lambda b, t: (b, t, 0)),
            pl.BlockSpec((1, N_MOD, D_MODEL), lambda b, t: (b, 0, 0)),
            _const_spec((1, D_MODEL)),
            _const_spec((D_MODEL, D_IN_PROJ)),
            _const_spec((3, D_CONV)),
            _const_spec((1, D_CONV)),
            _const_spec((D_CONV, D_CONV)),
        ],
        out_specs=[
            pl.BlockSpec((1, T_PRE, D_SSM), lambda b, t: (b, t, 0)),
            pl.BlockSpec((1, T_PRE, D_CONV), lambda b, t: (b, t, 0)),
        ],
        out_shape=[
            jax.ShapeDtypeStruct((BATCH, SEQ, D_SSM), _bf16),
            jax.ShapeDtypeStruct((BATCH, SEQ, D_CONV), _bf16),
        ],
        scratch_shapes=[pltpu.VMEM((T_PRE + SUBLANES, D_CONV), _f32)],
        compiler_params=pltpu.CompilerParams(
            dimension_semantics=("arbitrary", "arbitrary"), vmem_limit_bytes=VMEM_LIMIT),
        name="premix",
    )(x, mod3, g_pre, w_in_b, conv_w, g_out_conv, ones_conv)


def _ssm_kernel(u_ref, bs_ref, lam_ref, cs_ref, tz_ref, d_ref, gluw_ref, glub_ref, g_ref, ones_ref,
                y_ref, st_ref, u32_ref, ufl_ref, s_ref, yfl_ref, y32_ref):
    i = pl.program_id(0)

    @pl.when(i == 0)
    def _():
        st_ref[...] = jnp.zeros(st_ref.shape, _f32)

    u = u_ref[...].reshape(BATCH * T_SSM, D_SSM)
    for j in range(N_LANE_TILES_U):
        u32_ref[j] = u[:, j * LANES:(j + 1) * LANES].astype(_f32)
    for j in range(N_LANE_TILES_U):
        for s in range(M_CHUNK):
            for b in range(BATCH):
                ufl_ref[j, s, pl.ds(b, N_CHUNKS, stride=BATCH), :] = (
                    u32_ref[j, pl.ds(b * T_SSM + s, N_CHUNKS, stride=M_CHUNK), :])

    def chunk_lhs(j):
        return jnp.concatenate([ufl_ref[j, s] for s in range(M_CHUNK)], axis=-1).astype(_bf16)

    def bproj(j):
        s_ref[:, j * TILE_STATE:(j + 1) * TILE_STATE] = jnp.dot(
            chunk_lhs(j), bs_ref[j], preferred_element_type=_f32)

    def scan(j):
        re = [slice(j * TILE_STATE + k * LANES, j * TILE_STATE + (k + 1) * LANES) for k in range(SLABS_PER_TILE)]
        im = [slice(r.start + STATE_PER_TILE, r.stop + STATE_PER_TILE) for r in re]
        lam = [slice(j * STATE_PER_TILE + k * LANES, j * STATE_PER_TILE + (k + 1) * LANES)
               for k in range(SLABS_PER_TILE)]
        lre = [lam_ref[0, :, ln] for ln in lam]
        lim = [lam_ref[1, :, ln] for ln in lam]
        xre = [st_ref[:, r] for r in re]
        xim = [st_ref[:, r] for r in im]
        for c in range(N_CHUNKS):
            rows = slice(c * BATCH, (c + 1) * BATCH)
            for k in range(SLABS_PER_TILE):
                loc_re, loc_im = s_ref[rows, re[k]], s_ref[rows, im[k]]
                s_ref[rows, re[k]] = xre[k]
                s_ref[rows, im[k]] = xim[k]
                xre[k], xim[k] = (lre[k] * xre[k] - lim[k] * xim[k] + loc_re,
                                  lre[k] * xim[k] + lim[k] * xre[k] + loc_im)
        for k in range(SLABS_PER_TILE):
            st_ref[:, re[k]] = xre[k]
            st_ref[:, im[k]] = xim[k]

    def cproj(j):
        x_in = s_ref[:, j * TILE_STATE:(j + 1) * TILE_STATE].astype(_bf16)
        yf = jnp.dot(x_in, cs_ref[j], preferred_element_type=_f32)
        yf = yf + jnp.dot(chunk_lhs(j), tz_ref[j], preferred_element_type=_f32)
        for s in range(M_CHUNK):
            yfl_ref[j, s] = yf[:, s * LANES:(s + 1) * LANES]

    bproj(0)
    for j in range(N_LANE_TILES_U):
        if j + 1 < N_LANE_TILES_U:
            bproj(j + 1)
        if j > 0:
            cproj(j - 1)
        scan(j)
    cproj(N_LANE_TILES_U - 1)

    for j in range(N_LANE_TILES_U):
        for s in range(M_CHUNK):
            for b in range(BATCH):
                y32_ref[j, pl.ds(b * T_SSM + s, N_CHUNKS, stride=M_CHUNK), :] = (
                    yfl_ref[j, s, pl.ds(b, N_CHUNKS, stride=BATCH), :])
    y = jnp.concatenate([y32_ref[j] for j in range(N_LANE_TILES_U)], axis=-1)
    u32 = jnp.concatenate([u32_ref[j] for j in range(N_LANE_TILES_U)], axis=-1)
    y = y + d_ref[...] * u32
    z = jax.nn.gelu(y)
    gate = jnp.dot(z.astype(_bf16), gluw_ref[...], preferred_element_type=_f32) + glub_ref[...]
    z = z * jax.nn.sigmoid(gate)
    ms = jnp.dot((z * z).astype(_bf16), ones_ref[...], preferred_element_type=_f32)
    out = (z * lax.rsqrt(ms + EPS) * g_ref[...]).astype(_bf16)
    y_ref[...] = out.reshape(BATCH, T_SSM, D_SSM)


def _ssm(u, bs_w, lam8, cs_w, tz_w, d_skip, glu_w_b, glu_b, g_out_ssm, ones_ssm):
    rows = BATCH * T_SSM
    return pl.pallas_call(
        _ssm_kernel,
        grid=(SEQ // T_SSM,),
        in_specs=[
            pl.BlockSpec((BATCH, T_SSM, D_SSM), lambda i: (0, i, 0)),
            _const_spec((N_LANE_TILES_U, CHUNK_LANES, TILE_STATE)),
            _const_spec((2, SUBLANES, N_STATE)),
            _const_spec((N_LANE_TILES_U, TILE_STATE, CHUNK_LANES)),
            _const_spec((N_LANE_TILES_U, CHUNK_LANES, CHUNK_LANES)),
            _const_spec((1, D_SSM)),
            _const_spec((D_SSM, D_SSM)),
            _const_spec((1, D_SSM)),
            _const_spec((1, D_SSM)),
            _const_spec((D_SSM, D_SSM)),
        ],
        out_specs=pl.BlockSpec((BATCH, T_SSM, D_SSM), lambda i: (0, i, 0)),
        out_shape=jax.ShapeDtypeStruct((BATCH, SEQ, D_SSM), _bf16),
        scratch_shapes=[
            pltpu.VMEM((SUBLANES, N_LANE_TILES_U * TILE_STATE), _f32),
            pltpu.VMEM((N_LANE_TILES_U, rows, LANES), _f32),
            pltpu.VMEM((N_LANE_TILES_U, M_CHUNK, CHUNK_ROWS, LANES), _f32),
            pltpu.VMEM((CHUNK_ROWS, N_LANE_TILES_U * TILE_STATE), _f32),
            pltpu.VMEM((N_LANE_TILES_U, M_CHUNK, CHUNK_ROWS, LANES), _f32),
            pltpu.VMEM((N_LANE_TILES_U, rows, LANES), _f32),
        ],
        compiler_params=pltpu.CompilerParams(
            dimension_semantics=("arbitrary",), vmem_limit_bytes=VMEM_LIMIT),
        name="s5_mixer",
    )(u, bs_w, lam8, cs_w, tz_w, d_skip, glu_w_b, glu_b, g_out_ssm, ones_ssm)


def _ffn_kernel(x_ref, ya_ref, yb_ref, mod_ref, wout_ref, gpm_ref, gpf_ref, wup_ref, fcw_ref, wdn_ref, gpo_ref,
                o_ref, halo_ref, act_ref):
    ti = pl.program_id(1)

    @pl.when(ti == 0)
    def _():
        halo_ref[...] = jnp.zeros(halo_ref.shape, _f32)

    x = x_ref[0]
    gt1 = mod_ref[0, 2:3, :]
    sh2 = mod_ref[0, 3:4, :]
    sc2 = mod_ref[0, 4:5, :]
    gt2 = mod_ref[0, 5:6, :]
    mix = jnp.dot(ya_ref[0], wout_ref[0:D_SSM, :], preferred_element_type=_f32)
    mix = mix + jnp.dot(yb_ref[0], wout_ref[D_SSM:, :], preferred_element_type=_f32)
    x1 = x + gt1 * _rms(mix, gpm_ref[...])
    h2 = (_rms(x1, gpf_ref[...]) * (1.0 + sc2) + sh2).astype(_bf16)

    def up_chunk(c):
        cols_a = slice(c * FF_CHUNK, (c + 1) * FF_CHUNK)
        cols_v = slice(D_FF + c * FF_CHUNK, D_FF + (c + 1) * FF_CHUNK)
        return (jnp.dot(h2, wup_ref[:, cols_a], preferred_element_type=_f32), cols_a,
                jnp.dot(h2, wup_ref[:, cols_v], preferred_element_type=_f32), cols_v)

    def conv(hc, cols):
        halo = halo_ref[:, cols]
        halo_ref[:, cols] = hc[T_FFN - SUBLANES:, :]
        h1 = _shift_rows(hc, halo, 1)
        h2s = _shift_rows(hc, halo, 2)
        return fcw_ref[0:1, cols] * h2s + fcw_ref[1:2, cols] * h1 + fcw_ref[2:3, cols] * hc

    nxt = up_chunk(0)
    for c in range(N_FF_CHUNKS):
        ha, cols_a, hv, cols_v = nxt
        if c + 1 < N_FF_CHUNKS:
            nxt = up_chunk(c + 1)
        a = conv(ha, cols_a)
        v = conv(hv, cols_v)
        act_ref[:, cols_a] = (a * jax.nn.sigmoid(a) * v).astype(_bf16)
    down = jnp.dot(act_ref[...], wdn_ref[...], preferred_element_type=_f32)
    o_ref[0] = x1 + gt2 * _rms(down, gpo_ref[...])


def _ffn(x, ya, yb, mod3, w_out_b, g_post_mix, g_pre_ffn, w_up_b, ffn_cw, w_down_b, g_post_ffn):
    n_t = SEQ // T_FFN
    return pl.pallas_call(
        _ffn_kernel,
        grid=(BATCH, n_t),
        in_specs=[
            pl.BlockSpec((1, T_FFN, D_MODEL), lambda b, t: (b, t, 0)),
            pl.BlockSpec((1, T_FFN, D_SSM), lambda b, t: (b, t, 0)),
            pl.BlockSpec((1, T_FFN, D_CONV), lambda b, t: (b, t, 0)),
            pl.BlockSpec((1, N_MOD, D_MODEL), lambda b, t: (b, 0, 0)),
            _const_spec((D_MODEL, D_MODEL)),
            _const_spec((1, D_MODEL)),
            _const_spec((1, D_MODEL)),
            _const_spec((D_MODEL, 2 * D_FF)),
            _const_spec((3, 2 * D_FF)),
            _const_spec((D_FF, D_MODEL)),
            _const_spec((1, D_MODEL)),
        ],
        out_specs=pl.BlockSpec((1, T_FFN, D_MODEL), lambda b, t: (b, t, 0)),
        out_shape=jax.ShapeDtypeStruct((BATCH, SEQ, D_MODEL), _f32),
        scratch_shapes=[
            pltpu.VMEM((SUBLANES, 2 * D_FF), _f32),
            pltpu.VMEM((T_FFN, D_FF), _bf16),
        ],
        compiler_params=pltpu.CompilerParams(
            dimension_semantics=("arbitrary", "arbitrary"), vmem_limit_bytes=VMEM_LIMIT),
        name="outproj_convffn",
    )(x, ya, yb, mod3, w_out_b, g_post_mix, g_pre_ffn, w_up_b, ffn_cw, w_down_b, g_post_ffn)


def _head_mean_matrix(width, head):
    idx = jnp.arange(width) // head
    return jnp.where(idx[:, None] == idx[None, :], 1.0 / head, 0.0).astype(_bf16)


def kernel(x, c, w_ada, b_ada, g_pre_mix, g_post_mix, w_in, ssm_lam_re, ssm_lam_im, ssm_log_step, ssm_b_re, ssm_b_im, ssm_c_re, ssm_c_im, ssm_d, glu_w, glu_b, g_out_ssm, conv_w, g_out_conv, w_out, g_pre_ffn, g_post_ffn, w_up, ffn_conv_w, w_down):
    assert x.shape == (BATCH, SEQ, D_MODEL) and w_ada.shape[0] == 1
    row = lambda a: a.reshape(1, -1)

    mod3 = _modulation(c, w_ada[0], b_ada[0]).reshape(BATCH, N_MOD, D_MODEL)

    lam8, bs_w, cs_w, tz_w = _ssm_matrices(*_ssm_prep(
        ssm_lam_re[0], ssm_lam_im[0], ssm_log_step[0], ssm_b_re[0], ssm_b_im[0], ssm_c_re[0], ssm_c_im[0]))

    u, yb = _premix(x, mod3, row(g_pre_mix[0]), w_in[0].astype(_bf16), conv_w[0], row(g_out_conv[0]),
                    _head_mean_matrix(D_CONV, D_CONV // CONV_HEADS))

    ya = _ssm(u, bs_w, lam8, cs_w, tz_w, row(ssm_d[0]), glu_w[0].astype(_bf16),
              row(glu_b[0]), row(g_out_ssm[0]), _head_mean_matrix(D_SSM, SSM_GROUP))

    return _ffn(x, ya, yb, mod3, w_out[0].astype(_bf16), row(g_post_mix[0]), row(g_pre_ffn[0]),
                w_up[0].astype(_bf16), ffn_conv_w[0], w_down[0].astype(_bf16), row(g_post_ffn[0]))
```

```python
import jax
import jax.numpy as jnp
from jax import lax
from jax.experimental import pallas as pl
from jax.experimental.pallas import tpu as pltpu

D_MODEL = 1024
BATCH = 8
SEQ = 4096
D_SSM = 512
D_CONV = 512
SSM_GROUP = 16
N_SSM_GROUPS = 32
SSM_STATE = 64
CONV_HEADS = 8
D_FF = 2816
N_MOD = 6
D_IN_PROJ = 2048
EPS = 1e-6
LAMBDA_RE_MAX = -1e-4

N_STATE = N_SSM_GROUPS * SSM_STATE
SUBLANES = 8
LANES = 128
GROUPS_PER_LANE_TILE = LANES // SSM_GROUP
N_LANE_TILES_U = D_SSM // LANES
STATE_PER_TILE = GROUPS_PER_LANE_TILE * SSM_STATE

T_PRE = 512
T_SSM = 256
T_FFN = 512
FF_CHUNK = 256
N_FF_CHUNKS = D_FF // FF_CHUNK
SLABS_PER_TILE = STATE_PER_TILE // LANES
M_CHUNK = 4
N_CHUNKS = T_SSM // M_CHUNK
CHUNK_ROWS = N_CHUNKS * BATCH
CHUNK_LANES = M_CHUNK * LANES
TILE_STATE = 2 * STATE_PER_TILE
VMEM_LIMIT = 56 * 1024 * 1024

_f32 = jnp.float32
_bf16 = jnp.bfloat16


def _const_spec(shape):
    nd = len(shape)
    return pl.BlockSpec(shape, lambda *_: (0,) * nd, pipeline_mode=pl.Buffered(1))


def _rms(x, g):
    ms = jnp.mean(x * x, axis=-1, keepdims=True)
    return x * lax.rsqrt(ms + EPS) * g


def _shift_rows(h, halo, k):
    rolled = pltpu.roll(h, k, axis=0)
    row = lax.broadcasted_iota(jnp.int32, halo.shape, 0)
    top = jnp.where(row < k, pltpu.roll(halo, k, axis=0), rolled[0:SUBLANES])
    return jnp.concatenate([top, rolled[SUBLANES:]], axis=0)


def _mod_kernel(c_ref, w_ref, b_ref, o_ref):
    c = c_ref[...]
    c_act = c * jax.nn.sigmoid(c)
    o_ref[...] = jnp.dot(c_act, w_ref[...], preferred_element_type=_f32) + b_ref[...]


def _modulation(c, w_ada, b_ada):
    n_tile = D_MODEL
    return pl.pallas_call(
        _mod_kernel,
        grid=(N_MOD * D_MODEL // n_tile,),
        in_specs=[
            pl.BlockSpec((BATCH, D_MODEL), lambda j: (0, 0)),
            pl.BlockSpec((D_MODEL, n_tile), lambda j: (0, j)),
            pl.BlockSpec((1, n_tile), lambda j: (0, j)),
        ],
        out_specs=pl.BlockSpec((BATCH, n_tile), lambda j: (0, j)),
        out_shape=jax.ShapeDtypeStruct((BATCH, N_MOD * D_MODEL), _f32),
        name="adaln_mod",
    )(c, w_ada, b_ada.reshape(1, -1))


def _prep_kernel(lre_ref, lim_ref, lst_ref, bre_ref, bim_ref, cre_ref, cim_ref,
                 lam8_ref, bs_ref, cs_ref, tz_ref):
    lre = jnp.minimum(lre_ref[0], LAMBDA_RE_MAX)
    lim = lim_ref[0]
    step = jnp.exp(lst_ref[0])
    log_mag, ang = lre * step, lim * step

    def lam_pow(k):
        mag = jnp.exp(k * log_mag)
        return mag * jnp.cos(k * ang), mag * jnp.sin(k * ang)

    pw = [lam_pow(float(k)) for k in range(M_CHUNK + 1)]

    def block_diag(blk):
        wide = jnp.concatenate([blk] * GROUPS_PER_LANE_TILE, axis=-1)
        r = lax.broadcasted_iota(jnp.int32, wide.shape, 0) // SSM_GROUP
        c = lax.broadcasted_iota(jnp.int32, wide.shape, 1) // SSM_STATE
        return jnp.where(r == c, wide, 0.0)

    a_re, a_im = pw[1]
    n_re = a_re - 1.0
    den = lre * lre + lim * lim
    q_re = (n_re * lre + a_im * lim) / den
    q_im = (a_im * lre - n_re * lim) / den
    b_re, b_im = block_diag(bre_ref[0]), block_diag(bim_ref[0])
    bb_re = q_re * b_re - q_im * b_im
    bb_im = q_re * b_im + q_im * b_re
    c_re, c_im = block_diag(cre_ref[0]), block_diag(cim_ref[0])

    lam8_ref[0] = jnp.broadcast_to(pw[M_CHUNK][0], (SUBLANES, STATE_PER_TILE))
    lam8_ref[1] = jnp.broadcast_to(pw[M_CHUNK][1], (SUBLANES, STATE_PER_TILE))
    for s in range(M_CHUNK):
        rows = slice(s * LANES, (s + 1) * LANES)
        p_re, p_im = pw[M_CHUNK - 1 - s]
        bs_ref[0, rows, 0:STATE_PER_TILE] = (p_re * bb_re - p_im * bb_im).astype(_bf16)
        bs_ref[0, rows, STATE_PER_TILE:] = (p_re * bb_im + p_im * bb_re).astype(_bf16)
        p_re, p_im = pw[s + 1]
        cs_ref[0, 0:STATE_PER_TILE, rows] = (p_re * c_re - p_im * c_im).T.astype(_bf16)
        cs_ref[0, STATE_PER_TILE:, rows] = (-(p_re * c_im + p_im * c_re)).T.astype(_bf16)
    tz_ref[...] = jnp.zeros(tz_ref.shape, _bf16)
    contract_p = (((1,), (1,)), ((), ()))
    for tau in range(M_CHUNK):
        p_re, p_im = pw[tau]
        ct_re = p_re * c_re - p_im * c_im
        ct_im = p_re * c_im + p_im * c_re
        k_tau = (lax.dot_general(bb_re, ct_re, contract_p, precision=lax.Precision.HIGHEST,
                                 preferred_element_type=_f32)
                 - lax.dot_general(bb_im, ct_im, contract_p, precision=lax.Precision.HIGHEST,
                                   preferred_element_type=_f32))
        for s0 in range(M_CHUNK - tau):
            tz_ref[0, s0 * LANES:(s0 + 1) * LANES, (s0 + tau) * LANES:(s0 + tau + 1) * LANES] = k_tau.astype(_bf16)


def _ssm_prep(lam_re, lam_im, log_step, b_re, b_im, c_re, c_im):
    nj = N_LANE_TILES_U
    lstep = jnp.broadcast_to(log_step[:, None], (N_SSM_GROUPS, SSM_STATE))
    as_row = lambda a: a.reshape(nj, 1, STATE_PER_TILE)
    b_t = lambda a: a.transpose(0, 2, 1).reshape(nj, LANES, SSM_STATE)
    c_t = lambda a: a.reshape(nj, LANES, SSM_STATE)
    tile_spec = lambda r, c: pl.BlockSpec((1, r, c), lambda j: (j, 0, 0))
    return pl.pallas_call(
        _prep_kernel,
        grid=(nj,),
        in_specs=[tile_spec(1, STATE_PER_TILE)] * 3 + [tile_spec(LANES, SSM_STATE)] * 4,
        out_specs=[
            pl.BlockSpec((2, SUBLANES, STATE_PER_TILE), lambda j: (0, 0, j)),
            tile_spec(CHUNK_LANES, TILE_STATE),
            tile_spec(TILE_STATE, CHUNK_LANES),
            tile_spec(CHUNK_LANES, CHUNK_LANES),
        ],
        out_shape=(
            jax.ShapeDtypeStruct((2, SUBLANES, N_STATE), _f32),
            jax.ShapeDtypeStruct((nj, CHUNK_LANES, TILE_STATE), _bf16),
            jax.ShapeDtypeStruct((nj, TILE_STATE, CHUNK_LANES), _bf16),
            jax.ShapeDtypeStruct((nj, CHUNK_LANES, CHUNK_LANES), _bf16),
        ),
        name="s5_discretise",
    )(as_row(lam_re), as_row(lam_im), as_row(lstep), b_t(b_re), b_t(b_im), c_t(c_re), c_t(c_im))


def _premix_kernel(x_ref, mod_ref, g_ref, win_ref, cw_ref, gconv_ref, ones_ref,
                   u_ref, yb_ref, buf_ref):
    ti = pl.program_id(1)

    @pl.when(ti == 0)
    def _():
        buf_ref[0:SUBLANES, :] = jnp.zeros((SUBLANES, D_CONV), _f32)

    x = x_ref[0]
    sh = mod_ref[0, 0:1, :]
    sc = mod_ref[0, 1:2, :]
    h = _rms(x, g_ref[...]) * (1.0 + sc) + sh
    proj = jnp.dot(h.astype(_bf16), win_ref[...], preferred_element_type=_f32)
    u_ref[0] = proj[:, 0:D_SSM].astype(_bf16)
    bg = proj[:, D_SSM:D_SSM + D_CONV]
    cg = proj[:, D_SSM + D_CONV:D_SSM + 2 * D_CONV]
    v = proj[:, D_SSM + 2 * D_CONV:]
    cv = cg * v
    buf_ref[SUBLANES:SUBLANES + T_PRE, :] = cv
    c1 = buf_ref[SUBLANES - 1:SUBLANES - 1 + T_PRE, :]
    c2 = buf_ref[SUBLANES - 2:SUBLANES - 2 + T_PRE, :]
    buf_ref[0:SUBLANES, :] = buf_ref[T_PRE:T_PRE + SUBLANES, :]
    yb = bg * (cw_ref[0:1, :] * c2 + cw_ref[1:2, :] * c1 + cw_ref[2:3, :] * cv)
    ms = jnp.dot((yb * yb).astype(_bf16), ones_ref[...], preferred_element_type=_f32)
    yb_ref[0] = (yb * lax.rsqrt(ms + EPS) * gconv_ref[...]).astype(_bf16)


def _premix(x, mod3, g_pre, w_in_b, conv_w, g_out_conv, ones_conv):
    n_t = SEQ // T_PRE
    return pl.pallas_call(
        _premix_kernel,
        grid=(BATCH, n_t),
        in_specs=[
            pl.BlockSpec((1, T_PRE, D_MODEL), lambda b, t: (b, t, 0)),
            pl.BlockSpec((1, N_MOD, D_MODEL), lambda b, t: (b, 0, 0)),
            _const_spec((1, D_MODEL)),
            _const_spec((D_MODEL, D_IN_PROJ)),
            _const_spec((3, D_CONV)),
            _const_spec((1, D_CONV)),
            _const_spec((D_CONV, D_CONV)),
        ],
        out_specs=[
            pl.BlockSpec((1, T_PRE, D_SSM), lambda b, t: (b, t, 0)),
            pl.BlockSpec((1, T_PRE, D_CONV), lambda b, t: (b, t, 0)),
        ],
        out_shape=[
            jax.ShapeDtypeStruct((BATCH, SEQ, D_SSM), _bf16),
            jax.ShapeDtypeStruct((BATCH, SEQ, D_CONV), _bf16),
        ],
        scratch_shapes=[pltpu.VMEM((T_PRE + SUBLANES, D_CONV), _f32)],
        compiler_params=pltpu.CompilerParams(
            dimension_semantics=("arbitrary", "arbitrary"), vmem_limit_bytes=VMEM_LIMIT),
        name="premix",
    )(x, mod3, g_pre, w_in_b, conv_w, g_out_conv, ones_conv)


def _ssm_kernel(u_ref, bs_ref, lam_ref, cs_ref, tz_ref, d_ref, gluw_ref, glub_ref, g_ref, ones_ref,
                y_ref, st_ref, u32_ref, ufl_ref, s_ref, yfl_ref, y32_ref):
    i = pl.program_id(0)

    @pl.when(i == 0)
    def _():
        st_ref[...] = jnp.zeros(st_ref.shape, _f32)

    u = u_ref[...].reshape(BATCH * T_SSM, D_SSM)
    for j in range(N_LANE_TILES_U):
        u32_ref[j] = u[:, j * LANES:(j + 1) * LANES].astype(_f32)
    for j in range(N_LANE_TILES_U):
        for s in range(M_CHUNK):
            for b in range(BATCH):
                ufl_ref[j, s, pl.ds(b, N_CHUNKS, stride=BATCH), :] = (
                    u32_ref[j, pl.ds(b * T_SSM + s, N_CHUNKS, stride=M_CHUNK), :])

    def chunk_lhs(j):
        return jnp.concatenate([ufl_ref[j, s] for s in range(M_CHUNK)], axis=-1).astype(_bf16)

    def bproj(j):
        s_ref[:, j * TILE_STATE:(j + 1) * TILE_STATE] = jnp.dot(
            chunk_lhs(j), bs_ref[j], preferred_element_type=_f32)

    def scan(j):
        re = [slice(j * TILE_STATE + k * LANES, j * TILE_STATE + (k + 1) * LANES) for k in range(SLABS_PER_TILE)]
        im = [slice(r.start + STATE_PER_TILE, r.stop + STATE_PER_TILE) for r in re]
        lam = [slice(j * STATE_PER_TILE + k * LANES, j * STATE_PER_TILE + (k + 1) * LANES)
               for k in range(SLABS_PER_TILE)]
        lre = [lam_ref[0, :, ln] for ln in lam]
        lim = [lam_ref[1, :, ln] for ln in lam]
        xre = [st_ref[:, r] for r in re]
        xim = [st_ref[:, r] for r in im]
        for c in range(N_CHUNKS):
            rows = slice(c * BATCH, (c + 1) * BATCH)
            for k in range(SLABS_PER_TILE):
                loc_re, loc_im = s_ref[rows, re[k]], s_ref[rows, im[k]]
                s_ref[rows, re[k]] = xre[k]
                s_ref[rows, im[k]] = xim[k]
                xre[k], xim[k] = (lre[k] * xre[k] - lim[k] * xim[k] + loc_re,
                                  lre[k] * xim[k] + lim[k] * xre[k] + loc_im)
        for k in range(SLABS_PER_TILE):
            st_ref[:, re[k]] = xre[k]
            st_ref[:, im[k]] = xim[k]

    def cproj(j):
        x_in = s_ref[:, j * TILE_STATE:(j + 1) * TILE_STATE].astype(_bf16)
        yf = jnp.dot(x_in, cs_ref[j], preferred_element_type=_f32)
        yf = yf + jnp.dot(chunk_lhs(j), tz_ref[j], preferred_element_type=_f32)
        for s in range(M_CHUNK):
            yfl_ref[j, s] = yf[:, s * LANES:(s + 1) * LANES]

    bproj(0)
    for j in range(N_LANE_TILES_U):
        if j + 1 < N_LANE_TILES_U:
            bproj(j + 1)
        if j > 0:
            cproj(j - 1)
        scan(j)
    cproj(N_LANE_TILES_U - 1)

    for j in range(N_LANE_TILES_U):
        for s in range(M_CHUNK):
            for b in range(BATCH):
                y32_ref[j, pl.ds(b * T_SSM + s, N_CHUNKS, stride=M_CHUNK), :] = (
                    yfl_ref[j, s, pl.ds(b, N_CHUNKS, stride=BATCH), :])
    y = jnp.concatenate([y32_ref[j] for j in range(N_LANE_TILES_U)], axis=-1)
    u32 = jnp.concatenate([u32_ref[j] for j in range(N_LANE_TILES_U)], axis=-1)
    y = y + d_ref[...] * u32
    z = jax.nn.gelu(y)
    gate = jnp.dot(z.astype(_bf16), gluw_ref[...], preferred_element_type=_f32) + glub_ref[...]
    z = z * jax.nn.sigmoid(gate)
    ms = jnp.dot((z * z).astype(_bf16), ones_ref[...], preferred_element_type=_f32)
    out = (z * lax.rsqrt(ms + EPS) * g_ref[...]).astype(_bf16)
    y_ref[...] = out.reshape(BATCH, T_SSM, D_SSM)


def _ssm(u, bs_w, lam8, cs_w, tz_w, d_skip, glu_w_b, glu_b, g_out_ssm, ones_ssm):
    rows = BATCH * T_SSM
    return pl.pallas_call(
        _ssm_kernel,
        grid=(SEQ // T_SSM,),
        in_specs=[
            pl.BlockSpec((BATCH, T_SSM, D_SSM), lambda i: (0, i, 0)),
            _const_spec((N_LANE_TILES_U, CHUNK_LANES, TILE_STATE)),
            _const_spec((2, SUBLANES, N_STATE)),
            _const_spec((N_LANE_TILES_U, TILE_STATE, CHUNK_LANES)),
            _const_spec((N_LANE_TILES_U, CHUNK_LANES, CHUNK_LANES)),
            _const_spec((1, D_SSM)),
            _const_spec((D_SSM, D_SSM)),
            _const_spec((1, D_SSM)),
            _const_spec((1, D_SSM)),
            _const_spec((D_SSM, D_SSM)),
        ],
        out_specs=pl.BlockSpec((BATCH, T_SSM, D_SSM), lambda i: (0, i, 0)),
        out_shape=jax.ShapeDtypeStruct((BATCH, SEQ, D_SSM), _bf16),
        scratch_shapes=[
            pltpu.VMEM((SUBLANES, N_LANE_TILES_U * TILE_STATE), _f32),
            pltpu.VMEM((N_LANE_TILES_U, rows, LANES), _f32),
            pltpu.VMEM((N_LANE_TILES_U, M_CHUNK, CHUNK_ROWS, LANES), _f32),
            pltpu.VMEM((CHUNK_ROWS, N_LANE_TILES_U * TILE_STATE), _f32),
            pltpu.VMEM((N_LANE_TILES_U, M_CHUNK, CHUNK_ROWS, LANES), _f32),
            pltpu.VMEM((N_LANE_TILES_U, rows, LANES), _f32),
        ],
        compiler_params=pltpu.CompilerParams(
            dimension_semantics=("arbitrary",), vmem_limit_bytes=VMEM_LIMIT),
        name="s5_mixer",
    )(u, bs_w, lam8, cs_w, tz_w, d_skip, glu_w_b, glu_b, g_out_ssm, ones_ssm)


def _ffn_kernel(x_ref, ya_ref, yb_ref, mod_ref, wout_ref, gpm_ref, gpf_ref, wup_ref, fcw_ref, wdn_ref, gpo_ref,
                o_ref, halo_ref, act_ref):
    ti = pl.program_id(1)

    @pl.when(ti == 0)
    def _():
        halo_ref[...] = jnp.zeros(halo_ref.shape, _f32)

    x = x_ref[0]
    gt1 = mod_ref[0, 2:3, :]
    sh2 = mod_ref[0, 3:4, :]
    sc2 = mod_ref[0, 4:5, :]
    gt2 = mod_ref[0, 5:6, :]
    mix = jnp.dot(ya_ref[0], wout_ref[0:D_SSM, :], preferred_element_type=_f32)
    mix = mix + jnp.dot(yb_ref[0], wout_ref[D_SSM:, :], preferred_element_type=_f32)
    x1 = x + gt1 * _rms(mix, gpm_ref[...])
    h2 = (_rms(x1, gpf_ref[...]) * (1.0 + sc2) + sh2).astype(_bf16)

    def up_chunk(c):
        cols_a = slice(c * FF_CHUNK, (c + 1) * FF_CHUNK)
        cols_v = slice(D_FF + c * FF_CHUNK, D_FF + (c + 1) * FF_CHUNK)
        return (jnp.dot(h2, wup_ref[:, cols_a], preferred_element_type=_f32), cols_a,
                jnp.dot(h2, wup_ref[:, cols_v], preferred_element_type=_f32), cols_v)

    def conv(hc, cols):
        halo = halo_ref[:, cols]
        halo_ref[:, cols] = hc[T_FFN - SUBLANES:, :]
        h1 = _shift_rows(hc, halo, 1)
        h2s = _shift_rows(hc, halo, 2)
        return fcw_ref[0:1, cols] * h2s + fcw_ref[1:2, cols] * h1 + fcw_ref[2:3, cols] * hc

    nxt = up_chunk(0)
    for c in range(N_FF_CHUNKS):
        ha, cols_a, hv, cols_v = nxt
        if c + 1 < N_FF_CHUNKS:
            nxt = up_chunk(c + 1)
        a = conv(ha, cols_a)
        v = conv(hv, cols_v)
        act_ref[:, cols_a] = (a * jax.nn.sigmoid(a) * v).astype(_bf16)
    down = jnp.dot(act_ref[...], wdn_ref[...], preferred_element_type=_f32)
    o_ref[0] = x1 + gt2 * _rms(down, gpo_ref[...])


def _ffn(x, ya, yb, mod3, w_out_b, g_post_mix, g_pre_ffn, w_up_b, ffn_cw, w_down_b, g_post_ffn):
    n_t = SEQ // T_FFN
    return pl.pallas_call(
        _ffn_kernel,
        grid=(BATCH, n_t),
        in_specs=[
            pl.BlockSpec((1, T_FFN, D_MODEL), lambda b, t: (b, t, 0)),
            pl.BlockSpec((1, T_FFN, D_SSM), lambda b, t: (b, t, 0)),
            pl.BlockSpec((1, T_FFN, D_CONV), lambda b, t: (b, t, 0)),
            pl.BlockSpec((1, N_MOD, D_MODEL), lambda b, t: (b, 0, 0)),
            _const_spec((D_MODEL, D_MODEL)),
            _const_spec((1, D_MODEL)),
            _const_spec((1, D_MODEL)),
            _const_spec((D_MODEL, 2 * D_FF)),
            _const_spec((3, 2 * D_FF)),
            _const_spec((D_FF, D_MODEL)),
            _const_spec((1, D_MODEL)),
        ],
        out_specs=pl.BlockSpec((1, T_FFN, D_MODEL), lambda b, t: (b, t, 0)),
        out_shape=jax.ShapeDtypeStruct((BATCH, SEQ, D_MODEL), _f32),
        scratch_shapes=[
            pltpu.VMEM((SUBLANES, 2 * D_FF), _f32),
            pltpu.VMEM((T_FFN, D_FF), _bf16),
        ],
        compiler_params=pltpu.CompilerParams(
            dimension_semantics=("arbitrary", "arbitrary"), vmem_limit_bytes=VMEM_LIMIT),
        name="outproj_convffn",
    )(x, ya, yb, mod3, w_out_b, g_post_mix, g_pre_ffn, w_up_b, ffn_cw, w_down_b, g_post_ffn)


def _head_mean_matrix(width, head):
    idx = jnp.arange(width) // head
    return jnp.where(idx[:, None] == idx[None, :], 1.0 / head, 0.0).astype(_bf16)


def kernel(x, c, w_ada, b_ada, g_pre_mix, g_post_mix, w_in, ssm_lam_re, ssm_lam_im, ssm_log_step, ssm_b_re, ssm_b_im, ssm_c_re, ssm_c_im, ssm_d, glu_w, glu_b, g_out_ssm, conv_w, g_out_conv, w_out, g_pre_ffn, g_post_ffn, w_up, ffn_conv_w, w_down):
    assert x.shape == (BATCH, SEQ, D_MODEL) and w_ada.shape[0] == 1
    row = lambda a: a.reshape(1, -1)

    mod3 = _modulation(c, w_ada[0], b_ada[0]).reshape(BATCH, N_MOD, D_MODEL)

    lam8, bs_w, cs_w, tz_w = _ssm_prep(
        ssm_lam_re[0], ssm_lam_im[0], ssm_log_step[0], ssm_b_re[0], ssm_b_im[0], ssm_c_re[0], ssm_c_im[0])

    u, yb = _premix(x, mod3, row(g_pre_mix[0]), w_in[0].astype(_bf16), conv_w[0], row(g_out_conv[0]),
                    _head_mean_matrix(D_CONV, D_CONV // CONV_HEADS))

    ya = _ssm(u, bs_w, lam8, cs_w, tz_w, row(ssm_d[0]), glu_w[0].astype(_bf16),
              row(glu_b[0]), row(g_out_ssm[0]), _head_mean_matrix(D_SSM, SSM_GROUP))

    return _ffn(x, ya, yb, mod3, w_out[0].astype(_bf16), row(g_post_mix[0]), row(g_pre_ffn[0]),
                w_up[0].astype(_bf16), ffn_conv_w[0], w_down[0].astype(_bf16), row(g_post_ffn[0]))
```

```python
import jax
import jax.numpy as jnp
from jax import lax
from jax.experimental import pallas as pl
from jax.experimental.pallas import tpu as pltpu

D_MODEL = 1024
BATCH = 8
SEQ = 4096
D_SSM = 512
D_CONV = 512
SSM_GROUP = 16
N_SSM_GROUPS = 32
SSM_STATE = 64
CONV_HEADS = 8
D_FF = 2816
N_MOD = 6
D_IN_PROJ = 2048
EPS = 1e-6
LAMBDA_RE_MAX = -1e-4

N_STATE = N_SSM_GROUPS * SSM_STATE
SUBLANES = 8
LANES = 128
GROUPS_PER_LANE_TILE = LANES // SSM_GROUP
N_LANE_TILES_U = D_SSM // LANES
STATE_PER_TILE = GROUPS_PER_LANE_TILE * SSM_STATE

T_PRE = 512
T_SSM = 256
T_FFN = 512
FF_CHUNK = 256
N_FF_CHUNKS = D_FF // FF_CHUNK
FFN_SLABS_PER_CHUNK = 2 * FF_CHUNK // LANES
SLABS_PER_TILE = STATE_PER_TILE // LANES
M_CHUNK = 4
N_CHUNKS = T_SSM // M_CHUNK
CHUNK_ROWS = N_CHUNKS * BATCH
CHUNK_LANES = M_CHUNK * LANES
TILE_STATE = 2 * STATE_PER_TILE
VMEM_LIMIT = 56 * 1024 * 1024

_f32 = jnp.float32
_bf16 = jnp.bfloat16


def _const_spec(shape):
    nd = len(shape)
    return pl.BlockSpec(shape, lambda *_: (0,) * nd, pipeline_mode=pl.Buffered(1))


def _rms(x, g):
    ms = jnp.mean(x * x, axis=-1, keepdims=True)
    return x * lax.rsqrt(ms + EPS) * g


def _causal_conv3(h, halo, w, slab_ref, slab0):
    rows = h.shape[0]
    outs = []
    for k in range(h.shape[1] // LANES):
        lanes = slice(k * LANES, (k + 1) * LANES)
        slab_ref[slab0 + k, 0:SUBLANES, :] = halo[:, lanes]
        slab_ref[slab0 + k, SUBLANES:SUBLANES + rows, :] = h[:, lanes]
        h1 = slab_ref[slab0 + k, pl.ds(SUBLANES - 1, rows, stride=1), :]
        h2 = slab_ref[slab0 + k, pl.ds(SUBLANES - 2, rows, stride=1), :]
        outs.append(w[0:1, lanes] * h2 + w[1:2, lanes] * h1 + w[2:3, lanes] * h[:, lanes])
    return jnp.concatenate(outs, axis=-1)


def _mod_kernel(c_ref, w_ref, b_ref, o_ref):
    c = c_ref[...]
    c_act = c * jax.nn.sigmoid(c)
    o_ref[...] = jnp.dot(c_act, w_ref[...], preferred_element_type=_f32) + b_ref[...]


def _modulation(c, w_ada, b_ada):
    n_tile = D_MODEL
    return pl.pallas_call(
        _mod_kernel,
        grid=(N_MOD * D_MODEL // n_tile,),
        in_specs=[
            pl.BlockSpec((BATCH, D_MODEL), lambda j: (0, 0)),
            pl.BlockSpec((D_MODEL, n_tile), lambda j: (0, j)),
            pl.BlockSpec((1, n_tile), lambda j: (0, j)),
        ],
        out_specs=pl.BlockSpec((BATCH, n_tile), lambda j: (0, j)),
        out_shape=jax.ShapeDtypeStruct((BATCH, N_MOD * D_MODEL), _f32),
        name="adaln_mod",
    )(c, w_ada, b_ada.reshape(1, -1))


def _prep_kernel(lre_ref, lim_ref, lst_ref, bre_ref, bim_ref, cre_ref, cim_ref,
                 lam8_ref, bs_ref, cs_ref, tz_ref):
    lre = jnp.minimum(lre_ref[0], LAMBDA_RE_MAX)
    lim = lim_ref[0]
    step = jnp.exp(lst_ref[0])
    log_mag, ang = lre * step, lim * step

    def lam_pow(k):
        mag = jnp.exp(k * log_mag)
        return mag * jnp.cos(k * ang), mag * jnp.sin(k * ang)

    pw = [lam_pow(float(k)) for k in range(M_CHUNK + 1)]

    def block_diag(blk):
        wide = jnp.concatenate([blk] * GROUPS_PER_LANE_TILE, axis=-1)
        r = lax.broadcasted_iota(jnp.int32, wide.shape, 0) // SSM_GROUP
        c = lax.broadcasted_iota(jnp.int32, wide.shape, 1) // SSM_STATE
        return jnp.where(r == c, wide, 0.0)

    a_re, a_im = pw[1]
    n_re = a_re - 1.0
    den = lre * lre + lim * lim
    q_re = (n_re * lre + a_im * lim) / den
    q_im = (a_im * lre - n_re * lim) / den
    b_re, b_im = block_diag(bre_ref[0]), block_diag(bim_ref[0])
    bb_re = q_re * b_re - q_im * b_im
    bb_im = q_re * b_im + q_im * b_re
    c_re, c_im = block_diag(cre_ref[0]), block_diag(cim_ref[0])

    lam8_ref[0] = jnp.broadcast_to(pw[M_CHUNK][0], (SUBLANES, STATE_PER_TILE))
    lam8_ref[1] = jnp.broadcast_to(pw[M_CHUNK][1], (SUBLANES, STATE_PER_TILE))
    for s in range(M_CHUNK):
        rows = slice(s * LANES, (s + 1) * LANES)
        p_re, p_im = pw[M_CHUNK - 1 - s]
        bs_ref[0, rows, 0:STATE_PER_TILE] = (p_re * bb_re - p_im * bb_im).astype(_bf16)
        bs_ref[0, rows, STATE_PER_TILE:] = (p_re * bb_im + p_im * bb_re).astype(_bf16)
        p_re, p_im = pw[s + 1]
        cs_ref[0, 0:STATE_PER_TILE, rows] = (p_re * c_re - p_im * c_im).T.astype(_bf16)
        cs_ref[0, STATE_PER_TILE:, rows] = (-(p_re * c_im + p_im * c_re)).T.astype(_bf16)
    tz_ref[...] = jnp.zeros(tz_ref.shape, _bf16)
    contract_p = (((1,), (1,)), ((), ()))
    for tau in range(M_CHUNK):
        p_re, p_im = pw[tau]
        ct_re = p_re * c_re - p_im * c_im
        ct_im = p_re * c_im + p_im * c_re
        k_tau = (lax.dot_general(bb_re, ct_re, contract_p, precision=lax.Precision.HIGHEST,
                                 preferred_element_type=_f32)
                 - lax.dot_general(bb_im, ct_im, contract_p, precision=lax.Precision.HIGHEST,
                                   preferred_element_type=_f32))
        for s0 in range(M_CHUNK - tau):
            tz_ref[0, s0 * LANES:(s0 + 1) * LANES, (s0 + tau) * LANES:(s0 + tau + 1) * LANES] = k_tau.astype(_bf16)


def _ssm_prep(lam_re, lam_im, log_step, b_re, b_im, c_re, c_im):
    nj = N_LANE_TILES_U
    lstep = jnp.broadcast_to(log_step[:, None], (N_SSM_GROUPS, SSM_STATE))
    as_row = lambda a: a.reshape(nj, 1, STATE_PER_TILE)
    b_t = lambda a: a.transpose(0, 2, 1).reshape(nj, LANES, SSM_STATE)
    c_t = lambda a: a.reshape(nj, LANES, SSM_STATE)
    tile_spec = lambda r, c: pl.BlockSpec((1, r, c), lambda j: (j, 0, 0))
    return pl.pallas_call(
        _prep_kernel,
        grid=(nj,),
        in_specs=[tile_spec(1, STATE_PER_TILE)] * 3 + [tile_spec(LANES, SSM_STATE)] * 4,
        out_specs=[
            pl.BlockSpec((2, SUBLANES, STATE_PER_TILE), lambda j: (0, 0, j)),
            tile_spec(CHUNK_LANES, TILE_STATE),
            tile_spec(TILE_STATE, CHUNK_LANES),
            tile_spec(CHUNK_LANES, CHUNK_LANES),
        ],
        out_shape=(
            jax.ShapeDtypeStruct((2, SUBLANES, N_STATE), _f32),
            jax.ShapeDtypeStruct((nj, CHUNK_LANES, TILE_STATE), _bf16),
            jax.ShapeDtypeStruct((nj, TILE_STATE, CHUNK_LANES), _bf16),
            jax.ShapeDtypeStruct((nj, CHUNK_LANES, CHUNK_LANES), _bf16),
        ),
        name="s5_discretise",
    )(as_row(lam_re), as_row(lam_im), as_row(lstep), b_t(b_re), b_t(b_im), c_t(c_re), c_t(c_im))


def _premix_kernel(x_ref, mod_ref, g_ref, win_ref, cw_ref, gconv_ref, ones_ref,
                   u_ref, yb_ref, halo_ref, slab_ref):
    ti = pl.program_id(1)

    @pl.when(ti == 0)
    def _():
        halo_ref[...] = jnp.zeros(halo_ref.shape, _f32)

    x = x_ref[0]
    sh = mod_ref[0, 0:1, :]
    sc = mod_ref[0, 1:2, :]
    h = _rms(x, g_ref[...]) * (1.0 + sc) + sh
    proj = jnp.dot(h.astype(_bf16), win_ref[...], preferred_element_type=_f32)
    u_ref[0] = proj[:, 0:D_SSM].astype(_bf16)
    bg = proj[:, D_SSM:D_SSM + D_CONV]
    cg = proj[:, D_SSM + D_CONV:D_SSM + 2 * D_CONV]
    v = proj[:, D_SSM + 2 * D_CONV:]
    cv = cg * v
    halo = halo_ref[...]
    halo_ref[...] = cv[T_PRE - SUBLANES:, :]
    yb = bg * _causal_conv3(cv, halo, cw_ref, slab_ref, 0)
    ms = jnp.dot((yb * yb).astype(_bf16), ones_ref[...], preferred_element_type=_f32)
    yb_ref[0] = (yb * lax.rsqrt(ms + EPS) * gconv_ref[...]).astype(_bf16)


def _premix(x, mod3, g_pre, w_in_b, conv_w, g_out_conv, ones_conv):
    n_t = SEQ // T_PRE
    return pl.pallas_call(
        _premix_kernel,
        grid=(BATCH, n_t),
        in_specs=[
            pl.BlockSpec((1, T_PRE, D_MODEL), lambda b, t: (b, t, 0)),
            pl.BlockSpec((1, N_MOD, D_MODEL), lambda b, t: (b, 0, 0)),
            _const_spec((1, D_MODEL)),
            _const_spec((D_MODEL, D_IN_PROJ)),
            _const_spec((3, D_CONV)),
            _const_spec((1, D_CONV)),
            _const_spec((D_CONV, D_CONV)),
        ],
        out_specs=[
            pl.BlockSpec((1, T_PRE, D_SSM), lambda b, t: (b, t, 0)),
            pl.BlockSpec((1, T_PRE, D_CONV), lambda b, t: (b, t, 0)),
        ],
        out_shape=[
            jax.ShapeDtypeStruct((BATCH, SEQ, D_SSM), _bf16),
            jax.ShapeDtypeStruct((BATCH, SEQ, D_CONV), _bf16),
        ],
        scratch_shapes=[
            pltpu.VMEM((SUBLANES, D_CONV), _f32),
            pltpu.VMEM((D_CONV // LANES, T_PRE + SUBLANES, LANES), _f32),
        ],
        compiler_params=pltpu.CompilerParams(
            dimension_semantics=("arbitrary", "arbitrary"), vmem_limit_bytes=VMEM_LIMIT),
        name="premix",
    )(x, mod3, g_pre, w_in_b, conv_w, g_out_conv, ones_conv)


def _ssm_kernel(u_ref, bs_ref, lam_ref, cs_ref, tz_ref, d_ref, gluw_ref, glub_ref, g_ref, ones_ref,
                y_ref, st_ref, u32_ref, ufl_ref, s_ref, yfl_ref, y32_ref):
    i = pl.program_id(0)

    @pl.when(i == 0)
    def _():
        st_ref[...] = jnp.zeros(st_ref.shape, _f32)

    u = u_ref[...].reshape(BATCH * T_SSM, D_SSM)
    for j in range(N_LANE_TILES_U):
        u32_ref[j] = u[:, j * LANES:(j + 1) * LANES].astype(_f32)
    for j in range(N_LANE_TILES_U):
        for s in range(M_CHUNK):
            for b in range(BATCH):
                ufl_ref[j, s, pl.ds(b, N_CHUNKS, stride=BATCH), :] = (
                    u32_ref[j, pl.ds(b * T_SSM + s, N_CHUNKS, stride=M_CHUNK), :])

    def chunk_lhs(j):
        return jnp.concatenate([ufl_ref[j, s] for s in range(M_CHUNK)], axis=-1).astype(_bf16)

    def bproj(j):
        s_ref[:, j * TILE_STATE:(j + 1) * TILE_STATE] = jnp.dot(
            chunk_lhs(j), bs_ref[j], preferred_element_type=_f32)

    def scan(j):
        re = [slice(j * TILE_STATE + k * LANES, j * TILE_STATE + (k + 1) * LANES) for k in range(SLABS_PER_TILE)]
        im = [slice(r.start + STATE_PER_TILE, r.stop + STATE_PER_TILE) for r in re]
        lam = [slice(j * STATE_PER_TILE + k * LANES, j * STATE_PER_TILE + (k + 1) * LANES)
               for k in range(SLABS_PER_TILE)]
        lre = [lam_ref[0, :, ln] for ln in lam]
        lim = [lam_ref[1, :, ln] for ln in lam]
        xre = [st_ref[:, r] for r in re]
        xim = [st_ref[:, r] for r in im]
        for c in range(N_CHUNKS):
            rows = slice(c * BATCH, (c + 1) * BATCH)
            for k in range(SLABS_PER_TILE):
                loc_re, loc_im = s_ref[rows, re[k]], s_ref[rows, im[k]]
                s_ref[rows, re[k]] = xre[k]
                s_ref[rows, im[k]] = xim[k]
                xre[k], xim[k] = (lre[k] * xre[k] - lim[k] * xim[k] + loc_re,
                                  lre[k] * xim[k] + lim[k] * xre[k] + loc_im)
        for k in range(SLABS_PER_TILE):
            st_ref[:, re[k]] = xre[k]
            st_ref[:, im[k]] = xim[k]

    def cproj(j):
        x_in = s_ref[:, j * TILE_STATE:(j + 1) * TILE_STATE].astype(_bf16)
        yf = jnp.dot(x_in, cs_ref[j], preferred_element_type=_f32)
        yf = yf + jnp.dot(chunk_lhs(j), tz_ref[j], preferred_element_type=_f32)
        for s in range(M_CHUNK):
            yfl_ref[j, s] = yf[:, s * LANES:(s + 1) * LANES]

    bproj(0)
    for j in range(N_LANE_TILES_U):
        if j + 1 < N_LANE_TILES_U:
            bproj(j + 1)
        if j > 0:
            cproj(j - 1)
        scan(j)
    cproj(N_LANE_TILES_U - 1)

    for j in range(N_LANE_TILES_U):
        for s in range(M_CHUNK):
            for b in range(BATCH):
                y32_ref[j, pl.ds(b * T_SSM + s, N_CHUNKS, stride=M_CHUNK), :] = (
                    yfl_ref[j, s, pl.ds(b, N_CHUNKS, stride=BATCH), :])
    y = jnp.concatenate([y32_ref[j] for j in range(N_LANE_TILES_U)], axis=-1)
    u32 = jnp.concatenate([u32_ref[j] for j in range(N_LANE_TILES_U)], axis=-1)
    y = y + d_ref[...] * u32
    z = jax.nn.gelu(y)
    gate = jnp.dot(z.astype(_bf16), gluw_ref[...], preferred_element_type=_f32) + glub_ref[...]
    z = z * jax.nn.sigmoid(gate)
    ms = jnp.dot((z * z).astype(_bf16), ones_ref[...], preferred_element_type=_f32)
    out = (z * lax.rsqrt(ms + EPS) * g_ref[...]).astype(_bf16)
    y_ref[...] = out.reshape(BATCH, T_SSM, D_SSM)


def _ssm(u, bs_w, lam8, cs_w, tz_w, d_skip, glu_w_b, glu_b, g_out_ssm, ones_ssm):
    rows = BATCH * T_SSM
    return pl.pallas_call(
        _ssm_kernel,
        grid=(SEQ // T_SSM,),
        in_specs=[
            pl.BlockSpec((BATCH, T_SSM, D_SSM), lambda i: (0, i, 0)),
            _const_spec((N_LANE_TILES_U, CHUNK_LANES, TILE_STATE)),
            _const_spec((2, SUBLANES, N_STATE)),
            _const_spec((N_LANE_TILES_U, TILE_STATE, CHUNK_LANES)),
            _const_spec((N_LANE_TILES_U, CHUNK_LANES, CHUNK_LANES)),
            _const_spec((1, D_SSM)),
            _const_spec((D_SSM, D_SSM)),
            _const_spec((1, D_SSM)),
            _const_spec((1, D_SSM)),
            _const_spec((D_SSM, D_SSM)),
        ],
        out_specs=pl.BlockSpec((BATCH, T_SSM, D_SSM), lambda i: (0, i, 0)),
        out_shape=jax.ShapeDtypeStruct((BATCH, SEQ, D_SSM), _bf16),
        scratch_shapes=[
            pltpu.VMEM((SUBLANES, N_LANE_TILES_U * TILE_STATE), _f32),
            pltpu.VMEM((N_LANE_TILES_U, rows, LANES), _f32),
            pltpu.VMEM((N_LANE_TILES_U, M_CHUNK, CHUNK_ROWS, LANES), _f32),
            pltpu.VMEM((CHUNK_ROWS, N_LANE_TILES_U * TILE_STATE), _f32),
            pltpu.VMEM((N_LANE_TILES_U, M_CHUNK, CHUNK_ROWS, LANES), _f32),
            pltpu.VMEM((N_LANE_TILES_U, rows, LANES), _f32),
        ],
        compiler_params=pltpu.CompilerParams(
            dimension_semantics=("arbitrary",), vmem_limit_bytes=VMEM_LIMIT),
        name="s5_mixer",
    )(u, bs_w, lam8, cs_w, tz_w, d_skip, glu_w_b, glu_b, g_out_ssm, ones_ssm)


def _ffn_kernel(x_ref, ya_ref, yb_ref, mod_ref, wout_ref, gpm_ref, gpf_ref, wup_ref, fcw_ref, wdn_ref, gpo_ref,
                o_ref, halo_ref, act_ref, slab_ref):
    ti = pl.program_id(1)

    @pl.when(ti == 0)
    def _():
        halo_ref[...] = jnp.zeros(halo_ref.shape, _f32)

    x = x_ref[0]
    gt1 = mod_ref[0, 2:3, :]
    sh2 = mod_ref[0, 3:4, :]
    sc2 = mod_ref[0, 4:5, :]
    gt2 = mod_ref[0, 5:6, :]
    mix = jnp.dot(ya_ref[0], wout_ref[0:D_SSM, :], preferred_element_type=_f32)
    mix = mix + jnp.dot(yb_ref[0], wout_ref[D_SSM:, :], preferred_element_type=_f32)
    x1 = x + gt1 * _rms(mix, gpm_ref[...])
    h2 = (_rms(x1, gpf_ref[...]) * (1.0 + sc2) + sh2).astype(_bf16)

    def up_chunk(c):
        cols_a = slice(c * FF_CHUNK, (c + 1) * FF_CHUNK)
        cols_v = slice(D_FF + c * FF_CHUNK, D_FF + (c + 1) * FF_CHUNK)
        return (jnp.dot(h2, wup_ref[:, cols_a], preferred_element_type=_f32), cols_a,
                jnp.dot(h2, wup_ref[:, cols_v], preferred_element_type=_f32), cols_v)

    def conv(hc, cols, slab0):
        halo = halo_ref[:, cols]
        halo_ref[:, cols] = hc[T_FFN - SUBLANES:, :]
        return _causal_conv3(hc, halo, fcw_ref[:, cols], slab_ref, slab0)

    nxt = up_chunk(0)
    for c in range(N_FF_CHUNKS):
        ha, cols_a, hv, cols_v = nxt
        if c + 1 < N_FF_CHUNKS:
            nxt = up_chunk(c + 1)
        slab0 = (c % 2) * FFN_SLABS_PER_CHUNK
        a = conv(ha, cols_a, slab0)
        v = conv(hv, cols_v, slab0 + FF_CHUNK // LANES)
        act_ref[:, cols_a] = (a * jax.nn.sigmoid(a) * v).astype(_bf16)
    down = jnp.dot(act_ref[...], wdn_ref[...], preferred_element_type=_f32)
    o_ref[0] = x1 + gt2 * _rms(down, gpo_ref[...])


def _ffn(x, ya, yb, mod3, w_out_b, g_post_mix, g_pre_ffn, w_up_b, ffn_cw, w_down_b, g_post_ffn):
    n_t = SEQ // T_FFN
    return pl.pallas_call(
        _ffn_kernel,
        grid=(BATCH, n_t),
        in_specs=[
            pl.BlockSpec((1, T_FFN, D_MODEL), lambda b, t: (b, t, 0)),
            pl.BlockSpec((1, T_FFN, D_SSM), lambda b, t: (b, t, 0)),
            pl.BlockSpec((1, T_FFN, D_CONV), lambda b, t: (b, t, 0)),
            pl.BlockSpec((1, N_MOD, D_MODEL), lambda b, t: (b, 0, 0)),
            _const_spec((D_MODEL, D_MODEL)),
            _const_spec((1, D_MODEL)),
            _const_spec((1, D_MODEL)),
            _const_spec((D_MODEL, 2 * D_FF)),
            _const_spec((3, 2 * D_FF)),
            _const_spec((D_FF, D_MODEL)),
            _const_spec((1, D_MODEL)),
        ],
        out_specs=pl.BlockSpec((1, T_FFN, D_MODEL), lambda b, t: (b, t, 0)),
        out_shape=jax.ShapeDtypeStruct((BATCH, SEQ, D_MODEL), _f32),
        scratch_shapes=[
            pltpu.VMEM((SUBLANES, 2 * D_FF), _f32),
            pltpu.VMEM((T_FFN, D_FF), _bf16),
            pltpu.VMEM((2 * FFN_SLABS_PER_CHUNK, T_FFN + SUBLANES, LANES), _f32),
        ],
        compiler_params=pltpu.CompilerParams(
            dimension_semantics=("arbitrary", "arbitrary"), vmem_limit_bytes=VMEM_LIMIT),
        name="outproj_convffn",
    )(x, ya, yb, mod3, w_out_b, g_post_mix, g_pre_ffn, w_up_b, ffn_cw, w_down_b, g_post_ffn)


def _head_mean_matrix(width, head):
    idx = jnp.arange(width) // head
    return jnp.where(idx[:, None] == idx[None, :], 1.0 / head, 0.0).astype(_bf16)


def kernel(x, c, w_ada, b_ada, g_pre_mix, g_post_mix, w_in, ssm_lam_re, ssm_lam_im, ssm_log_step, ssm_b_re, ssm_b_im, ssm_c_re, ssm_c_im, ssm_d, glu_w, glu_b, g_out_ssm, conv_w, g_out_conv, w_out, g_pre_ffn, g_post_ffn, w_up, ffn_conv_w, w_down):
    assert x.shape == (BATCH, SEQ, D_MODEL) and w_ada.shape[0] == 1
    row = lambda a: a.reshape(1, -1)

    mod3 = _modulation(c, w_ada[0], b_ada[0]).reshape(BATCH, N_MOD, D_MODEL)

    lam8, bs_w, cs_w, tz_w = _ssm_prep(
        ssm_lam_re[0], ssm_lam_im[0], ssm_log_step[0], ssm_b_re[0], ssm_b_im[0], ssm_c_re[0], ssm_c_im[0])

    u, yb = _premix(x, mod3, row(g_pre_mix[0]), w_in[0].astype(_bf16), conv_w[0], row(g_out_conv[0]),
                    _head_mean_matrix(D_CONV, D_CONV // CONV_HEADS))

    ya = _ssm(u, bs_w, lam8, cs_w, tz_w, row(ssm_d[0]), glu_w[0].astype(_bf16),
              row(glu_b[0]), row(g_out_ssm[0]), _head_mean_matrix(D_SSM, SSM_GROUP))

    return _ffn(x, ya, yb, mod3, w_out[0].astype(_bf16), row(g_post_mix[0]), row(g_pre_ffn[0]),
                w_up[0].astype(_bf16), ffn_conv_w[0], w_down[0].astype(_bf16), row(g_post_ffn[0]))
```

```python
import jax
import jax.numpy as jnp
from jax import lax
from jax.experimental import pallas as pl
from jax.experimental.pallas import tpu as pltpu

D_MODEL = 1024
BATCH = 8
SEQ = 4096
D_SSM = 512
D_CONV = 512
SSM_GROUP = 16
N_SSM_GROUPS = 32
SSM_STATE = 64
CONV_HEADS = 8
D_FF = 2816
N_MOD = 6
D_IN_PROJ = 2048
EPS = 1e-6
LAMBDA_RE_MAX = -1e-4

N_STATE = N_SSM_GROUPS * SSM_STATE
SUBLANES = 8
LANES = 128
GROUPS_PER_LANE_TILE = LANES // SSM_GROUP
N_LANE_TILES_U = D_SSM // LANES
STATE_PER_TILE = GROUPS_PER_LANE_TILE * SSM_STATE

T_PRE = 1024
T_SSM = 256
T_FFN = 1024
FF_CHUNK = 256
N_FF_CHUNKS = D_FF // FF_CHUNK
FFN_SLABS_PER_CHUNK = 2 * FF_CHUNK // LANES
SLABS_PER_TILE = STATE_PER_TILE // LANES
M_CHUNK = 4
N_CHUNKS = T_SSM // M_CHUNK
CHUNK_ROWS = N_CHUNKS * BATCH
CHUNK_LANES = M_CHUNK * LANES
TILE_STATE = 2 * STATE_PER_TILE
VMEM_LIMIT = 60 * 1024 * 1024

_f32 = jnp.float32
_bf16 = jnp.bfloat16


def _const_spec(shape):
    nd = len(shape)
    return pl.BlockSpec(shape, lambda *_: (0,) * nd, pipeline_mode=pl.Buffered(1))


def _rms(x, g):
    ms = jnp.mean(x * x, axis=-1, keepdims=True)
    return x * lax.rsqrt(ms + EPS) * g


def _causal_conv3(h, halo, w, slab_ref, slab0):
    rows = h.shape[0]
    outs = []
    for k in range(h.shape[1] // LANES):
        lanes = slice(k * LANES, (k + 1) * LANES)
        slab_ref[slab0 + k, 0:SUBLANES, :] = halo[:, lanes]
        slab_ref[slab0 + k, SUBLANES:SUBLANES + rows, :] = h[:, lanes]
        h1 = slab_ref[slab0 + k, pl.ds(SUBLANES - 1, rows, stride=1), :]
        h2 = slab_ref[slab0 + k, pl.ds(SUBLANES - 2, rows, stride=1), :]
        outs.append(w[0:1, lanes] * h2 + w[1:2, lanes] * h1 + w[2:3, lanes] * h[:, lanes])
    return jnp.concatenate(outs, axis=-1)


def _mod_kernel(c_ref, w_ref, b_ref, o_ref):
    c = c_ref[...]
    c_act = c * jax.nn.sigmoid(c)
    o_ref[...] = jnp.dot(c_act, w_ref[...], preferred_element_type=_f32) + b_ref[...]


def _modulation(c, w_ada, b_ada):
    n_tile = D_MODEL
    return pl.pallas_call(
        _mod_kernel,
        grid=(N_MOD * D_MODEL // n_tile,),
        in_specs=[
            pl.BlockSpec((BATCH, D_MODEL), lambda j: (0, 0)),
            pl.BlockSpec((D_MODEL, n_tile), lambda j: (0, j)),
            pl.BlockSpec((1, n_tile), lambda j: (0, j)),
        ],
        out_specs=pl.BlockSpec((BATCH, n_tile), lambda j: (0, j)),
        out_shape=jax.ShapeDtypeStruct((BATCH, N_MOD * D_MODEL), _f32),
        name="adaln_mod",
    )(c, w_ada, b_ada.reshape(1, -1))


def _prep_kernel(lre_ref, lim_ref, lst_ref, bre_ref, bim_ref, cre_ref, cim_ref,
                 lam8_ref, bs_ref, cs_ref, tz_ref):
    lre = jnp.minimum(lre_ref[0], LAMBDA_RE_MAX)
    lim = lim_ref[0]
    step = jnp.exp(lst_ref[0])
    log_mag, ang = lre * step, lim * step

    def lam_pow(k):
        mag = jnp.exp(k * log_mag)
        return mag * jnp.cos(k * ang), mag * jnp.sin(k * ang)

    pw = [lam_pow(float(k)) for k in range(M_CHUNK + 1)]

    def block_diag(blk):
        wide = jnp.concatenate([blk] * GROUPS_PER_LANE_TILE, axis=-1)
        r = lax.broadcasted_iota(jnp.int32, wide.shape, 0) // SSM_GROUP
        c = lax.broadcasted_iota(jnp.int32, wide.shape, 1) // SSM_STATE
        return jnp.where(r == c, wide, 0.0)

    a_re, a_im = pw[1]
    n_re = a_re - 1.0
    den = lre * lre + lim * lim
    q_re = (n_re * lre + a_im * lim) / den
    q_im = (a_im * lre - n_re * lim) / den
    b_re, b_im = block_diag(bre_ref[0]), block_diag(bim_ref[0])
    bb_re = q_re * b_re - q_im * b_im
    bb_im = q_re * b_im + q_im * b_re
    c_re, c_im = block_diag(cre_ref[0]), block_diag(cim_ref[0])

    lam8_ref[0] = jnp.broadcast_to(pw[M_CHUNK][0], (SUBLANES, STATE_PER_TILE))
    lam8_ref[1] = jnp.broadcast_to(pw[M_CHUNK][1], (SUBLANES, STATE_PER_TILE))
    for s in range(M_CHUNK):
        rows = slice(s * LANES, (s + 1) * LANES)
        p_re, p_im = pw[M_CHUNK - 1 - s]
        bs_ref[0, rows, 0:STATE_PER_TILE] = (p_re * bb_re - p_im * bb_im).astype(_bf16)
        bs_ref[0, rows, STATE_PER_TILE:] = (p_re * bb_im + p_im * bb_re).astype(_bf16)
        p_re, p_im = pw[s + 1]
        cs_ref[0, 0:STATE_PER_TILE, rows] = (p_re * c_re - p_im * c_im).T.astype(_bf16)
        cs_ref[0, STATE_PER_TILE:, rows] = (-(p_re * c_im + p_im * c_re)).T.astype(_bf16)
    tz_ref[...] = jnp.zeros(tz_ref.shape, _bf16)
    contract_p = (((1,), (1,)), ((), ()))
    for tau in range(M_CHUNK):
        p_re, p_im = pw[tau]
        ct_re = p_re * c_re - p_im * c_im
        ct_im = p_re * c_im + p_im * c_re
        k_tau = (lax.dot_general(bb_re, ct_re, contract_p, precision=lax.Precision.HIGHEST,
                                 preferred_element_type=_f32)
                 - lax.dot_general(bb_im, ct_im, contract_p, precision=lax.Precision.HIGHEST,
                                   preferred_element_type=_f32))
        for s0 in range(M_CHUNK - tau):
            tz_ref[0, s0 * LANES:(s0 + 1) * LANES, (s0 + tau) * LANES:(s0 + tau + 1) * LANES] = k_tau.astype(_bf16)


def _ssm_prep(lam_re, lam_im, log_step, b_re, b_im, c_re, c_im):
    nj = N_LANE_TILES_U
    lstep = jnp.broadcast_to(log_step[:, None], (N_SSM_GROUPS, SSM_STATE))
    as_row = lambda a: a.reshape(nj, 1, STATE_PER_TILE)
    b_t = lambda a: a.transpose(0, 2, 1).reshape(nj, LANES, SSM_STATE)
    c_t = lambda a: a.reshape(nj, LANES, SSM_STATE)
    tile_spec = lambda r, c: pl.BlockSpec((1, r, c), lambda j: (j, 0, 0))
    return pl.pallas_call(
        _prep_kernel,
        grid=(nj,),
        in_specs=[tile_spec(1, STATE_PER_TILE)] * 3 + [tile_spec(LANES, SSM_STATE)] * 4,
        out_specs=[
            pl.BlockSpec((2, SUBLANES, STATE_PER_TILE), lambda j: (0, 0, j)),
            tile_spec(CHUNK_LANES, TILE_STATE),
            tile_spec(TILE_STATE, CHUNK_LANES),
            tile_spec(CHUNK_LANES, CHUNK_LANES),
        ],
        out_shape=(
            jax.ShapeDtypeStruct((2, SUBLANES, N_STATE), _f32),
            jax.ShapeDtypeStruct((nj, CHUNK_LANES, TILE_STATE), _bf16),
            jax.ShapeDtypeStruct((nj, TILE_STATE, CHUNK_LANES), _bf16),
            jax.ShapeDtypeStruct((nj, CHUNK_LANES, CHUNK_LANES), _bf16),
        ),
        name="s5_discretise",
    )(as_row(lam_re), as_row(lam_im), as_row(lstep), b_t(b_re), b_t(b_im), c_t(c_re), c_t(c_im))


def _premix_kernel(x_ref, mod_ref, g_ref, win_ref, cw_ref, gconv_ref, ones_ref,
                   u_ref, yb_ref, halo_ref, slab_ref):
    ti = pl.program_id(1)

    @pl.when(ti == 0)
    def _():
        halo_ref[...] = jnp.zeros(halo_ref.shape, _f32)

    x = x_ref[0]
    sh = mod_ref[0, 0:1, :]
    sc = mod_ref[0, 1:2, :]
    h = _rms(x, g_ref[...]) * (1.0 + sc) + sh
    proj = jnp.dot(h.astype(_bf16), win_ref[...], preferred_element_type=_f32)
    u_ref[0] = proj[:, 0:D_SSM].astype(_bf16)
    bg = proj[:, D_SSM:D_SSM + D_CONV]
    cg = proj[:, D_SSM + D_CONV:D_SSM + 2 * D_CONV]
    v = proj[:, D_SSM + 2 * D_CONV:]
    cv = cg * v
    halo = halo_ref[...]
    halo_ref[...] = cv[T_PRE - SUBLANES:, :]
    yb = bg * _causal_conv3(cv, halo, cw_ref, slab_ref, 0)
    ms = jnp.dot((yb * yb).astype(_bf16), ones_ref[...], preferred_element_type=_f32)
    yb_ref[0] = (yb * lax.rsqrt(ms + EPS) * gconv_ref[...]).astype(_bf16)


def _premix(x, mod3, g_pre, w_in_b, conv_w, g_out_conv, ones_conv):
    n_t = SEQ // T_PRE
    return pl.pallas_call(
        _premix_kernel,
        grid=(BATCH, n_t),
        in_specs=[
            pl.BlockSpec((1, T_PRE, D_MODEL), lambda b, t: (b, t, 0)),
            pl.BlockSpec((1, N_MOD, D_MODEL), lambda b, t: (b, 0, 0)),
            _const_spec((1, D_MODEL)),
            _const_spec((D_MODEL, D_IN_PROJ)),
            _const_spec((3, D_CONV)),
            _const_spec((1, D_CONV)),
            _const_spec((D_CONV, D_CONV)),
        ],
        out_specs=[
            pl.BlockSpec((1, T_PRE, D_SSM), lambda b, t: (b, t, 0)),
            pl.BlockSpec((1, T_PRE, D_CONV), lambda b, t: (b, t, 0)),
        ],
        out_shape=[
            jax.ShapeDtypeStruct((BATCH, SEQ, D_SSM), _bf16),
            jax.ShapeDtypeStruct((BATCH, SEQ, D_CONV), _bf16),
        ],
        scratch_shapes=[
            pltpu.VMEM((SUBLANES, D_CONV), _f32),
            pltpu.VMEM((D_CONV // LANES, T_PRE + SUBLANES, LANES), _f32),
        ],
        compiler_params=pltpu.CompilerParams(
            dimension_semantics=("arbitrary", "arbitrary"), vmem_limit_bytes=VMEM_LIMIT),
        name="premix",
    )(x, mod3, g_pre, w_in_b, conv_w, g_out_conv, ones_conv)


def _ssm_kernel(u_ref, bs_ref, lam_ref, cs_ref, tz_ref, d_ref, gluw_ref, glub_ref, g_ref, ones_ref,
                y_ref, st_ref, u32_ref, ufl_ref, s_ref, yfl_ref, y32_ref):
    i = pl.program_id(0)

    @pl.when(i == 0)
    def _():
        st_ref[...] = jnp.zeros(st_ref.shape, _f32)

    u = u_ref[...].reshape(BATCH * T_SSM, D_SSM)
    for j in range(N_LANE_TILES_U):
        u32_ref[j] = u[:, j * LANES:(j + 1) * LANES].astype(_f32)
    for j in range(N_LANE_TILES_U):
        for s in range(M_CHUNK):
            for b in range(BATCH):
                ufl_ref[j, s, pl.ds(b, N_CHUNKS, stride=BATCH), :] = (
                    u32_ref[j, pl.ds(b * T_SSM + s, N_CHUNKS, stride=M_CHUNK), :])

    def chunk_lhs(j):
        return jnp.concatenate([ufl_ref[j, s] for s in range(M_CHUNK)], axis=-1).astype(_bf16)

    def bproj(j):
        s_ref[:, j * TILE_STATE:(j + 1) * TILE_STATE] = jnp.dot(
            chunk_lhs(j), bs_ref[j], preferred_element_type=_f32)

    def scan(j):
        re = [slice(j * TILE_STATE + k * LANES, j * TILE_STATE + (k + 1) * LANES) for k in range(SLABS_PER_TILE)]
        im = [slice(r.start + STATE_PER_TILE, r.stop + STATE_PER_TILE) for r in re]
        lam = [slice(j * STATE_PER_TILE + k * LANES, j * STATE_PER_TILE + (k + 1) * LANES)
               for k in range(SLABS_PER_TILE)]
        lre = [lam_ref[0, :, ln] for ln in lam]
        lim = [lam_ref[1, :, ln] for ln in lam]
        xre = [st_ref[:, r] for r in re]
        xim = [st_ref[:, r] for r in im]
        for c in range(N_CHUNKS):
            rows = slice(c * BATCH, (c + 1) * BATCH)
            for k in range(SLABS_PER_TILE):
                loc_re, loc_im = s_ref[rows, re[k]], s_ref[rows, im[k]]
                s_ref[rows, re[k]] = xre[k]
                s_ref[rows, im[k]] = xim[k]
                xre[k], xim[k] = (lre[k] * xre[k] - lim[k] * xim[k] + loc_re,
                                  lre[k] * xim[k] + lim[k] * xre[k] + loc_im)
        for k in range(SLABS_PER_TILE):
            st_ref[:, re[k]] = xre[k]
            st_ref[:, im[k]] = xim[k]

    def cproj(j):
        x_in = s_ref[:, j * TILE_STATE:(j + 1) * TILE_STATE].astype(_bf16)
        yf = jnp.dot(x_in, cs_ref[j], preferred_element_type=_f32)
        yf = yf + jnp.dot(chunk_lhs(j), tz_ref[j], preferred_element_type=_f32)
        for s in range(M_CHUNK):
            yfl_ref[j, s] = yf[:, s * LANES:(s + 1) * LANES]

    bproj(0)
    for j in range(N_LANE_TILES_U):
        if j + 1 < N_LANE_TILES_U:
            bproj(j + 1)
        if j > 0:
            cproj(j - 1)
        scan(j)
    cproj(N_LANE_TILES_U - 1)

    for j in range(N_LANE_TILES_U):
        for s in range(M_CHUNK):
            for b in range(BATCH):
                y32_ref[j, pl.ds(b * T_SSM + s, N_CHUNKS, stride=M_CHUNK), :] = (
                    yfl_ref[j, s, pl.ds(b, N_CHUNKS, stride=BATCH), :])
    y = jnp.concatenate([y32_ref[j] for j in range(N_LANE_TILES_U)], axis=-1)
    u32 = jnp.concatenate([u32_ref[j] for j in range(N_LANE_TILES_U)], axis=-1)
    y = y + d_ref[...] * u32
    z = jax.nn.gelu(y)
    gate = jnp.dot(z.astype(_bf16), gluw_ref[...], preferred_element_type=_f32) + glub_ref[...]
    z = z * jax.nn.sigmoid(gate)
    ms = jnp.dot((z * z).astype(_bf16), ones_ref[...], preferred_element_type=_f32)
    out = (z * lax.rsqrt(ms + EPS) * g_ref[...]).astype(_bf16)
    y_ref[...] = out.reshape(BATCH, T_SSM, D_SSM)


def _ssm(u, bs_w, lam8, cs_w, tz_w, d_skip, glu_w_b, glu_b, g_out_ssm, ones_ssm):
    rows = BATCH * T_SSM
    return pl.pallas_call(
        _ssm_kernel,
        grid=(SEQ // T_SSM,),
        in_specs=[
            pl.BlockSpec((BATCH, T_SSM, D_SSM), lambda i: (0, i, 0)),
            _const_spec((N_LANE_TILES_U, CHUNK_LANES, TILE_STATE)),
            _const_spec((2, SUBLANES, N_STATE)),
            _const_spec((N_LANE_TILES_U, TILE_STATE, CHUNK_LANES)),
            _const_spec((N_LANE_TILES_U, CHUNK_LANES, CHUNK_LANES)),
            _const_spec((1, D_SSM)),
            _const_spec((D_SSM, D_SSM)),
            _const_spec((1, D_SSM)),
            _const_spec((1, D_SSM)),
            _const_spec((D_SSM, D_SSM)),
        ],
        out_specs=pl.BlockSpec((BATCH, T_SSM, D_SSM), lambda i: (0, i, 0)),
        out_shape=jax.ShapeDtypeStruct((BATCH, SEQ, D_SSM), _bf16),
        scratch_shapes=[
            pltpu.VMEM((SUBLANES, N_LANE_TILES_U * TILE_STATE), _f32),
            pltpu.VMEM((N_LANE_TILES_U, rows, LANES), _f32),
            pltpu.VMEM((N_LANE_TILES_U, M_CHUNK, CHUNK_ROWS, LANES), _f32),
            pltpu.VMEM((CHUNK_ROWS, N_LANE_TILES_U * TILE_STATE), _f32),
            pltpu.VMEM((N_LANE_TILES_U, M_CHUNK, CHUNK_ROWS, LANES), _f32),
            pltpu.VMEM((N_LANE_TILES_U, rows, LANES), _f32),
        ],
        compiler_params=pltpu.CompilerParams(
            dimension_semantics=("arbitrary",), vmem_limit_bytes=VMEM_LIMIT),
        name="s5_mixer",
    )(u, bs_w, lam8, cs_w, tz_w, d_skip, glu_w_b, glu_b, g_out_ssm, ones_ssm)


def _ffn_kernel(x_ref, ya_ref, yb_ref, mod_ref, wout_ref, gpm_ref, gpf_ref, wup_ref, fcw_ref, wdn_ref, gpo_ref,
                o_ref, halo_ref, act_ref, slab_ref):
    ti = pl.program_id(1)

    @pl.when(ti == 0)
    def _():
        halo_ref[...] = jnp.zeros(halo_ref.shape, _f32)

    x = x_ref[0]
    gt1 = mod_ref[0, 2:3, :]
    sh2 = mod_ref[0, 3:4, :]
    sc2 = mod_ref[0, 4:5, :]
    gt2 = mod_ref[0, 5:6, :]
    mix = jnp.dot(ya_ref[0], wout_ref[0:D_SSM, :], preferred_element_type=_f32)
    mix = mix + jnp.dot(yb_ref[0], wout_ref[D_SSM:, :], preferred_element_type=_f32)
    x1 = x + gt1 * _rms(mix, gpm_ref[...])
    h2 = (_rms(x1, gpf_ref[...]) * (1.0 + sc2) + sh2).astype(_bf16)

    def up_chunk(c):
        cols_a = slice(c * FF_CHUNK, (c + 1) * FF_CHUNK)
        cols_v = slice(D_FF + c * FF_CHUNK, D_FF + (c + 1) * FF_CHUNK)
        return (jnp.dot(h2, wup_ref[:, cols_a], preferred_element_type=_f32), cols_a,
                jnp.dot(h2, wup_ref[:, cols_v], preferred_element_type=_f32), cols_v)

    def conv(hc, cols, slab0):
        halo = halo_ref[:, cols]
        halo_ref[:, cols] = hc[T_FFN - SUBLANES:, :]
        return _causal_conv3(hc, halo, fcw_ref[:, cols], slab_ref, slab0)

    nxt = up_chunk(0)
    for c in range(N_FF_CHUNKS):
        ha, cols_a, hv, cols_v = nxt
        if c + 1 < N_FF_CHUNKS:
            nxt = up_chunk(c + 1)
        slab0 = (c % 2) * FFN_SLABS_PER_CHUNK
        a = conv(ha, cols_a, slab0)
        v = conv(hv, cols_v, slab0 + FF_CHUNK // LANES)
        act_ref[:, cols_a] = (a * jax.nn.sigmoid(a) * v).astype(_bf16)
    down = jnp.dot(act_ref[...], wdn_ref[...], preferred_element_type=_f32)
    o_ref[0] = x1 + gt2 * _rms(down, gpo_ref[...])


def _ffn(x, ya, yb, mod3, w_out_b, g_post_mix, g_pre_ffn, w_up_b, ffn_cw, w_down_b, g_post_ffn):
    n_t = SEQ // T_FFN
    return pl.pallas_call(
        _ffn_kernel,
        grid=(BATCH, n_t),
        in_specs=[
            pl.BlockSpec((1, T_FFN, D_MODEL), lambda b, t: (b, t, 0)),
            pl.BlockSpec((1, T_FFN, D_SSM), lambda b, t: (b, t, 0)),
            pl.BlockSpec((1, T_FFN, D_CONV), lambda b, t: (b, t, 0)),
            pl.BlockSpec((1, N_MOD, D_MODEL), lambda b, t: (b, 0, 0)),
            _const_spec((D_MODEL, D_MODEL)),
            _const_spec((1, D_MODEL)),
            _const_spec((1, D_MODEL)),
            _const_spec((D_MODEL, 2 * D_FF)),
            _const_spec((3, 2 * D_FF)),
            _const_spec((D_FF, D_MODEL)),
            _const_spec((1, D_MODEL)),
        ],
        out_specs=pl.BlockSpec((1, T_FFN, D_MODEL), lambda b, t: (b, t, 0)),
        out_shape=jax.ShapeDtypeStruct((BATCH, SEQ, D_MODEL), _f32),
        scratch_shapes=[
            pltpu.VMEM((SUBLANES, 2 * D_FF), _f32),
            pltpu.VMEM((T_FFN, D_FF), _bf16),
            pltpu.VMEM((2 * FFN_SLABS_PER_CHUNK, T_FFN + SUBLANES, LANES), _f32),
        ],
        compiler_params=pltpu.CompilerParams(
            dimension_semantics=("arbitrary", "arbitrary"), vmem_limit_bytes=VMEM_LIMIT),
        name="outproj_convffn",
    )(x, ya, yb, mod3, w_out_b, g_post_mix, g_pre_ffn, w_up_b, ffn_cw, w_down_b, g_post_ffn)


def _head_mean_matrix(width, head):
    idx = jnp.arange(width) // head
    return jnp.where(idx[:, None] == idx[None, :], 1.0 / head, 0.0).astype(_bf16)


def kernel(x, c, w_ada, b_ada, g_pre_mix, g_post_mix, w_in, ssm_lam_re, ssm_lam_im, ssm_log_step, ssm_b_re, ssm_b_im, ssm_c_re, ssm_c_im, ssm_d, glu_w, glu_b, g_out_ssm, conv_w, g_out_conv, w_out, g_pre_ffn, g_post_ffn, w_up, ffn_conv_w, w_down):
    assert x.shape == (BATCH, SEQ, D_MODEL) and w_ada.shape[0] == 1
    row = lambda a: a.reshape(1, -1)

    mod3 = _modulation(c, w_ada[0], b_ada[0]).reshape(BATCH, N_MOD, D_MODEL)

    lam8, bs_w, cs_w, tz_w = _ssm_prep(
        ssm_lam_re[0], ssm_lam_im[0], ssm_log_step[0], ssm_b_re[0], ssm_b_im[0], ssm_c_re[0], ssm_c_im[0])

    u, yb = _premix(x, mod3, row(g_pre_mix[0]), w_in[0].astype(_bf16), conv_w[0], row(g_out_conv[0]),
                    _head_mean_matrix(D_CONV, D_CONV // CONV_HEADS))

    ya = _ssm(u, bs_w, lam8, cs_w, tz_w, row(ssm_d[0]), glu_w[0].astype(_bf16),
              row(glu_b[0]), row(g_out_ssm[0]), _head_mean_matrix(D_SSM, SSM_GROUP))

    return _ffn(x, ya, yb, mod3, w_out[0].astype(_bf16), row(g_post_mix[0]), row(g_pre_ffn[0]),
                w_up[0].astype(_bf16), ffn_conv_w[0], w_down[0].astype(_bf16), row(g_post_ffn[0]))
```

```python
import jax
import jax.numpy as jnp
from jax import lax
from jax.experimental import pallas as pl
from jax.experimental.pallas import tpu as pltpu

D_MODEL = 1024
BATCH = 8
SEQ = 4096
D_SSM = 512
D_CONV = 512
SSM_GROUP = 16
N_SSM_GROUPS = 32
SSM_STATE = 64
CONV_HEADS = 8
D_FF = 2816
N_MOD = 6
D_IN_PROJ = 2048
EPS = 1e-6
LAMBDA_RE_MAX = -1e-4

N_STATE = N_SSM_GROUPS * SSM_STATE
SUBLANES = 8
LANES = 128
GROUPS_PER_LANE_TILE = LANES // SSM_GROUP
N_LANE_TILES_U = D_SSM // LANES
STATE_PER_TILE = GROUPS_PER_LANE_TILE * SSM_STATE

T_PRE = 1024
T_SSM = 256
T_FFN = 1024
FF_CHUNK = 256
N_FF_CHUNKS = D_FF // FF_CHUNK
FFN_SLABS_PER_CHUNK = 2 * FF_CHUNK // LANES
M_CHUNK = 4
N_CHUNKS = T_SSM // M_CHUNK
CHUNK_ROWS = N_CHUNKS * BATCH
GROUPS_PER_BLOCK = GROUPS_PER_LANE_TILE // 2
BLOCKS_PER_TILE = GROUPS_PER_LANE_TILE // GROUPS_PER_BLOCK
N_BLOCKS = N_SSM_GROUPS // GROUPS_PER_BLOCK
BLOCK_LANES = GROUPS_PER_BLOCK * SSM_GROUP
BLOCK_STATE = GROUPS_PER_BLOCK * SSM_STATE
BLOCK_CHUNK = M_CHUNK * BLOCK_LANES
BLOCK_COLS = 2 * BLOCK_STATE
STEP_PAIRS = M_CHUNK // 2
VMEM_LIMIT = 60 * 1024 * 1024

_f32 = jnp.float32
_bf16 = jnp.bfloat16


def _const_spec(shape):
    nd = len(shape)
    return pl.BlockSpec(shape, lambda *_: (0,) * nd, pipeline_mode=pl.Buffered(1))


def _rms(x, g):
    ms = jnp.mean(x * x, axis=-1, keepdims=True)
    return x * lax.rsqrt(ms + EPS) * g


def _causal_conv3(h, halo, w, slab_ref, slab0):
    rows = h.shape[0]
    outs = []
    for k in range(h.shape[1] // LANES):
        lanes = slice(k * LANES, (k + 1) * LANES)
        slab_ref[slab0 + k, 0:SUBLANES, :] = halo[:, lanes]
        slab_ref[slab0 + k, SUBLANES:SUBLANES + rows, :] = h[:, lanes]
        h1 = slab_ref[slab0 + k, pl.ds(SUBLANES - 1, rows, stride=1), :]
        h2 = slab_ref[slab0 + k, pl.ds(SUBLANES - 2, rows, stride=1), :]
        outs.append(w[0:1, lanes] * h2 + w[1:2, lanes] * h1 + w[2:3, lanes] * h[:, lanes])
    return jnp.concatenate(outs, axis=-1)


def _mod_kernel(c_ref, w_ref, b_ref, o_ref):
    c = c_ref[...]
    c_act = c * jax.nn.sigmoid(c)
    o_ref[...] = jnp.dot(c_act, w_ref[...], preferred_element_type=_f32) + b_ref[...]


def _modulation(c, w_ada, b_ada):
    n_tile = D_MODEL
    return pl.pallas_call(
        _mod_kernel,
        grid=(N_MOD * D_MODEL // n_tile,),
        in_specs=[
            pl.BlockSpec((BATCH, D_MODEL), lambda j: (0, 0)),
            pl.BlockSpec((D_MODEL, n_tile), lambda j: (0, j)),
            pl.BlockSpec((1, n_tile), lambda j: (0, j)),
        ],
        out_specs=pl.BlockSpec((BATCH, n_tile), lambda j: (0, j)),
        out_shape=jax.ShapeDtypeStruct((BATCH, N_MOD * D_MODEL), _f32),
        name="adaln_mod",
    )(c, w_ada, b_ada.reshape(1, -1))


def _prep_kernel(lre_ref, lim_ref, lst_ref, bre_ref, bim_ref, cre_ref, cim_ref,
                 lam8_ref, bs_ref, cs_ref, tz_ref):
    lre = jnp.minimum(lre_ref[0], LAMBDA_RE_MAX)
    lim = lim_ref[0]
    step = jnp.exp(lst_ref[0])
    log_mag, ang = lre * step, lim * step

    def lam_pow(k):
        mag = jnp.exp(k * log_mag)
        return mag * jnp.cos(k * ang), mag * jnp.sin(k * ang)

    pw = [lam_pow(float(k)) for k in range(M_CHUNK + 1)]

    def block_diag(blk):
        wide = jnp.concatenate([blk] * GROUPS_PER_BLOCK, axis=-1)
        r = lax.broadcasted_iota(jnp.int32, wide.shape, 0) // SSM_GROUP
        c = lax.broadcasted_iota(jnp.int32, wide.shape, 1) // SSM_STATE
        return jnp.where(r == c, wide, 0.0)

    a_re, a_im = pw[1]
    n_re = a_re - 1.0
    den = lre * lre + lim * lim
    q_re = (n_re * lre + a_im * lim) / den
    q_im = (a_im * lre - n_re * lim) / den
    b_re, b_im = block_diag(bre_ref[0]), block_diag(bim_ref[0])
    bb_re = q_re * b_re - q_im * b_im
    bb_im = q_re * b_im + q_im * b_re
    c_re, c_im = block_diag(cre_ref[0]), block_diag(cim_ref[0])

    lam8_ref[0] = jnp.broadcast_to(pw[M_CHUNK][0], (SUBLANES, BLOCK_STATE))
    lam8_ref[1] = jnp.broadcast_to(pw[M_CHUNK][1], (SUBLANES, BLOCK_STATE))
    for s in range(M_CHUNK):
        rows = slice(s * BLOCK_LANES, (s + 1) * BLOCK_LANES)
        p_re, p_im = pw[M_CHUNK - 1 - s]
        bs_ref[0, rows, 0:BLOCK_STATE] = (p_re * bb_re - p_im * bb_im).astype(_bf16)
        bs_ref[0, rows, BLOCK_STATE:] = (p_re * bb_im + p_im * bb_re).astype(_bf16)
        p_re, p_im = pw[s + 1]
        cs_ref[0, 0:BLOCK_STATE, rows] = (p_re * c_re - p_im * c_im).T.astype(_bf16)
        cs_ref[0, BLOCK_STATE:, rows] = (-(p_re * c_im + p_im * c_re)).T.astype(_bf16)
    tz_ref[...] = jnp.zeros(tz_ref.shape, _bf16)
    contract_p = (((1,), (1,)), ((), ()))
    for tau in range(M_CHUNK):
        p_re, p_im = pw[tau]
        ct_re = p_re * c_re - p_im * c_im
        ct_im = p_re * c_im + p_im * c_re
        k_tau = (lax.dot_general(bb_re, ct_re, contract_p, precision=lax.Precision.HIGHEST,
                                 preferred_element_type=_f32)
                 - lax.dot_general(bb_im, ct_im, contract_p, precision=lax.Precision.HIGHEST,
                                   preferred_element_type=_f32))
        for s0 in range(M_CHUNK - tau):
            tz_ref[0, s0 * BLOCK_LANES:(s0 + 1) * BLOCK_LANES,
                   (s0 + tau) * BLOCK_LANES:(s0 + tau + 1) * BLOCK_LANES] = k_tau.astype(_bf16)


def _ssm_prep(lam_re, lam_im, log_step, b_re, b_im, c_re, c_im):
    nb = N_BLOCKS
    lstep = jnp.broadcast_to(log_step[:, None], (N_SSM_GROUPS, SSM_STATE))
    as_row = lambda a: a.reshape(nb, 1, BLOCK_STATE)
    b_t = lambda a: a.transpose(0, 2, 1).reshape(nb, BLOCK_LANES, SSM_STATE)
    c_t = lambda a: a.reshape(nb, BLOCK_LANES, SSM_STATE)
    tile_spec = lambda r, c: pl.BlockSpec((1, r, c), lambda j: (j, 0, 0))
    return pl.pallas_call(
        _prep_kernel,
        grid=(nb,),
        in_specs=[tile_spec(1, BLOCK_STATE)] * 3 + [tile_spec(BLOCK_LANES, SSM_STATE)] * 4,
        out_specs=[
            pl.BlockSpec((2, SUBLANES, BLOCK_STATE), lambda j: (0, 0, j)),
            tile_spec(BLOCK_CHUNK, BLOCK_COLS),
            tile_spec(BLOCK_COLS, BLOCK_CHUNK),
            tile_spec(BLOCK_CHUNK, BLOCK_CHUNK),
        ],
        out_shape=(
            jax.ShapeDtypeStruct((2, SUBLANES, N_STATE), _f32),
            jax.ShapeDtypeStruct((nb, BLOCK_CHUNK, BLOCK_COLS), _bf16),
            jax.ShapeDtypeStruct((nb, BLOCK_COLS, BLOCK_CHUNK), _bf16),
            jax.ShapeDtypeStruct((nb, BLOCK_CHUNK, BLOCK_CHUNK), _bf16),
        ),
        name="s5_discretise",
    )(as_row(lam_re), as_row(lam_im), as_row(lstep), b_t(b_re), b_t(b_im), c_t(c_re), c_t(c_im))


def _premix_kernel(x_ref, mod_ref, g_ref, win_ref, cw_ref, gconv_ref, ones_ref,
                   u_ref, yb_ref, halo_ref, slab_ref):
    ti = pl.program_id(1)

    @pl.when(ti == 0)
    def _():
        halo_ref[...] = jnp.zeros(halo_ref.shape, _f32)

    x = x_ref[0]
    sh = mod_ref[0, 0:1, :]
    sc = mod_ref[0, 1:2, :]
    h = _rms(x, g_ref[...]) * (1.0 + sc) + sh
    proj = jnp.dot(h.astype(_bf16), win_ref[...], preferred_element_type=_f32)
    u_ref[0] = proj[:, 0:D_SSM].astype(_bf16)
    bg = proj[:, D_SSM:D_SSM + D_CONV]
    cg = proj[:, D_SSM + D_CONV:D_SSM + 2 * D_CONV]
    v = proj[:, D_SSM + 2 * D_CONV:]
    cv = cg * v
    halo = halo_ref[...]
    halo_ref[...] = cv[T_PRE - SUBLANES:, :]
    yb = bg * _causal_conv3(cv, halo, cw_ref, slab_ref, 0)
    ms = jnp.dot((yb * yb).astype(_bf16), ones_ref[...], preferred_element_type=_f32)
    yb_ref[0] = (yb * lax.rsqrt(ms + EPS) * gconv_ref[...]).astype(_bf16)


def _premix(x, mod3, g_pre, w_in_b, conv_w, g_out_conv, ones_conv):
    n_t = SEQ // T_PRE
    return pl.pallas_call(
        _premix_kernel,
        grid=(BATCH, n_t),
        in_specs=[
            pl.BlockSpec((1, T_PRE, D_MODEL), lambda b, t: (b, t, 0)),
            pl.BlockSpec((1, N_MOD, D_MODEL), lambda b, t: (b, 0, 0)),
            _const_spec((1, D_MODEL)),
            _const_spec((D_MODEL, D_IN_PROJ)),
            _const_spec((3, D_CONV)),
            _const_spec((1, D_CONV)),
            _const_spec((D_CONV, D_CONV)),
        ],
        out_specs=[
            pl.BlockSpec((1, T_PRE, D_SSM), lambda b, t: (b, t, 0)),
            pl.BlockSpec((1, T_PRE, D_CONV), lambda b, t: (b, t, 0)),
        ],
        out_shape=[
            jax.ShapeDtypeStruct((BATCH, SEQ, D_SSM), _bf16),
            jax.ShapeDtypeStruct((BATCH, SEQ, D_CONV), _bf16),
        ],
        scratch_shapes=[
            pltpu.VMEM((SUBLANES, D_CONV), _f32),
            pltpu.VMEM((D_CONV // LANES, T_PRE + SUBLANES, LANES), _f32),
        ],
        compiler_params=pltpu.CompilerParams(
            dimension_semantics=("arbitrary", "arbitrary"), vmem_limit_bytes=VMEM_LIMIT),
        name="premix",
    )(x, mod3, g_pre, w_in_b, conv_w, g_out_conv, ones_conv)


def _ssm_kernel(u_ref, bs_ref, lam_ref, cs_ref, tz_ref, d_ref, gluw_ref, glub_ref, g_ref, ones_ref,
                y_ref, st_ref, u32_ref, ufl_ref, s_ref, yfl_ref, y32_ref):
    i = pl.program_id(0)

    @pl.when(i == 0)
    def _():
        st_ref[...] = jnp.zeros(st_ref.shape, _f32)

    low_half = lax.broadcasted_iota(jnp.int32, (N_CHUNKS, LANES), 1) < BLOCK_LANES
    swap_halves = lambda v: pltpu.roll(v, BLOCK_LANES, axis=1)

    u = u_ref[...].reshape(BATCH * T_SSM, D_SSM)
    for j in range(N_LANE_TILES_U):
        u32_ref[j] = u[:, j * LANES:(j + 1) * LANES].astype(_f32)
    for j in range(N_LANE_TILES_U):
        for q in range(STEP_PAIRS):
            for b in range(BATCH):
                v0 = u32_ref[j, pl.ds(b * T_SSM + 2 * q, N_CHUNKS, stride=M_CHUNK), :]
                v1 = u32_ref[j, pl.ds(b * T_SSM + 2 * q + 1, N_CHUNKS, stride=M_CHUNK), :]
                dst = pl.ds(b, N_CHUNKS, stride=BATCH)
                ufl_ref[BLOCKS_PER_TILE * j, q, dst, :] = jnp.where(low_half, v0, swap_halves(v1))
                ufl_ref[BLOCKS_PER_TILE * j + 1, q, dst, :] = jnp.where(low_half, swap_halves(v0), v1)

    def chunk_lhs(blk):
        return jnp.concatenate([ufl_ref[blk, q] for q in range(STEP_PAIRS)], axis=-1).astype(_bf16)

    def bproj(j):
        for blk in range(BLOCKS_PER_TILE * j, BLOCKS_PER_TILE * (j + 1)):
            s_ref[:, blk * BLOCK_COLS:(blk + 1) * BLOCK_COLS] = jnp.dot(
                chunk_lhs(blk), bs_ref[blk], preferred_element_type=_f32)

    def scan(j):
        re, lam = [], []
        for blk in range(BLOCKS_PER_TILE * j, BLOCKS_PER_TILE * (j + 1)):
            for k in range(BLOCK_STATE // LANES):
                re.append(slice(blk * BLOCK_COLS + k * LANES, blk * BLOCK_COLS + (k + 1) * LANES))
                lam.append(slice(blk * BLOCK_STATE + k * LANES, blk * BLOCK_STATE + (k + 1) * LANES))
        im = [slice(r.start + BLOCK_STATE, r.stop + BLOCK_STATE) for r in re]
        n = len(re)
        lre = [lam_ref[0, :, ln] for ln in lam]
        lim = [lam_ref[1, :, ln] for ln in lam]
        xre = [st_ref[:, r] for r in re]
        xim = [st_ref[:, r] for r in im]
        for c in range(N_CHUNKS):
            rows = slice(c * BATCH, (c + 1) * BATCH)
            for k in range(n):
                loc_re, loc_im = s_ref[rows, re[k]], s_ref[rows, im[k]]
                s_ref[rows, re[k]] = xre[k]
                s_ref[rows, im[k]] = xim[k]
                xre[k], xim[k] = (lre[k] * xre[k] - lim[k] * xim[k] + loc_re,
                                  lre[k] * xim[k] + lim[k] * xre[k] + loc_im)
        for k in range(n):
            st_ref[:, re[k]] = xre[k]
            st_ref[:, im[k]] = xim[k]

    def cproj(j):
        for blk in range(BLOCKS_PER_TILE * j, BLOCKS_PER_TILE * (j + 1)):
            x_in = s_ref[:, blk * BLOCK_COLS:(blk + 1) * BLOCK_COLS].astype(_bf16)
            yf = jnp.dot(x_in, cs_ref[blk], preferred_element_type=_f32)
            yf = yf + jnp.dot(chunk_lhs(blk), tz_ref[blk], preferred_element_type=_f32)
            for q in range(STEP_PAIRS):
                yfl_ref[blk, q] = yf[:, q * LANES:(q + 1) * LANES]

    bproj(0)
    for j in range(N_LANE_TILES_U):
        if j + 1 < N_LANE_TILES_U:
            bproj(j + 1)
        if j > 0:
            cproj(j - 1)
        scan(j)
    cproj(N_LANE_TILES_U - 1)

    for j in range(N_LANE_TILES_U):
        for q in range(STEP_PAIRS):
            for b in range(BATCH):
                src = pl.ds(b, N_CHUNKS, stride=BATCH)
                lo = yfl_ref[BLOCKS_PER_TILE * j, q, src, :]
                hi = yfl_ref[BLOCKS_PER_TILE * j + 1, q, src, :]
                y32_ref[j, pl.ds(b * T_SSM + 2 * q, N_CHUNKS, stride=M_CHUNK), :] = (
                    jnp.where(low_half, lo, swap_halves(hi)))
                y32_ref[j, pl.ds(b * T_SSM + 2 * q + 1, N_CHUNKS, stride=M_CHUNK), :] = (
                    jnp.where(low_half, swap_halves(lo), hi))
    y = jnp.concatenate([y32_ref[j] for j in range(N_LANE_TILES_U)], axis=-1)
    u32 = jnp.concatenate([u32_ref[j] for j in range(N_LANE_TILES_U)], axis=-1)
    y = y + d_ref[...] * u32
    z = jax.nn.gelu(y)
    gate = jnp.dot(z.astype(_bf16), gluw_ref[...], preferred_element_type=_f32) + glub_ref[...]
    z = z * jax.nn.sigmoid(gate)
    ms = jnp.dot((z * z).astype(_bf16), ones_ref[...], preferred_element_type=_f32)
    out = (z * lax.rsqrt(ms + EPS) * g_ref[...]).astype(_bf16)
    y_ref[...] = out.reshape(BATCH, T_SSM, D_SSM)


def _ssm(u, bs_w, lam8, cs_w, tz_w, d_skip, glu_w_b, glu_b, g_out_ssm, ones_ssm):
    rows = BATCH * T_SSM
    return pl.pallas_call(
        _ssm_kernel,
        grid=(SEQ // T_SSM,),
        in_specs=[
            pl.BlockSpec((BATCH, T_SSM, D_SSM), lambda i: (0, i, 0)),
            _const_spec((N_BLOCKS, BLOCK_CHUNK, BLOCK_COLS)),
            _const_spec((2, SUBLANES, N_STATE)),
            _const_spec((N_BLOCKS, BLOCK_COLS, BLOCK_CHUNK)),
            _const_spec((N_BLOCKS, BLOCK_CHUNK, BLOCK_CHUNK)),
            _const_spec((1, D_SSM)),
            _const_spec((D_SSM, D_SSM)),
            _const_spec((1, D_SSM)),
            _const_spec((1, D_SSM)),
            _const_spec((D_SSM, D_SSM)),
        ],
        out_specs=pl.BlockSpec((BATCH, T_SSM, D_SSM), lambda i: (0, i, 0)),
        out_shape=jax.ShapeDtypeStruct((BATCH, SEQ, D_SSM), _bf16),
        scratch_shapes=[
            pltpu.VMEM((SUBLANES, N_BLOCKS * BLOCK_COLS), _f32),
            pltpu.VMEM((N_LANE_TILES_U, rows, LANES), _f32),
            pltpu.VMEM((N_BLOCKS, STEP_PAIRS, CHUNK_ROWS, LANES), _f32),
            pltpu.VMEM((CHUNK_ROWS, N_BLOCKS * BLOCK_COLS), _f32),
            pltpu.VMEM((N_BLOCKS, STEP_PAIRS, CHUNK_ROWS, LANES), _f32),
            pltpu.VMEM((N_LANE_TILES_U, rows, LANES), _f32),
        ],
        compiler_params=pltpu.CompilerParams(
            dimension_semantics=("arbitrary",), vmem_limit_bytes=VMEM_LIMIT),
        name="s5_mixer",
    )(u, bs_w, lam8, cs_w, tz_w, d_skip, glu_w_b, glu_b, g_out_ssm, ones_ssm)


def _ffn_kernel(x_ref, ya_ref, yb_ref, mod_ref, wout_ref, gpm_ref, gpf_ref, wup_ref, fcw_ref, wdn_ref, gpo_ref,
                o_ref, halo_ref, act_ref, slab_ref):
    ti = pl.program_id(1)

    @pl.when(ti == 0)
    def _():
        halo_ref[...] = jnp.zeros(halo_ref.shape, _f32)

    x = x_ref[0]
    gt1 = mod_ref[0, 2:3, :]
    sh2 = mod_ref[0, 3:4, :]
    sc2 = mod_ref[0, 4:5, :]
    gt2 = mod_ref[0, 5:6, :]
    mix = jnp.dot(ya_ref[0], wout_ref[0:D_SSM, :], preferred_element_type=_f32)
    mix = mix + jnp.dot(yb_ref[0], wout_ref[D_SSM:, :], preferred_element_type=_f32)
    x1 = x + gt1 * _rms(mix, gpm_ref[...])
    h2 = (_rms(x1, gpf_ref[...]) * (1.0 + sc2) + sh2).astype(_bf16)

    def up_chunk(c):
        cols_a = slice(c * FF_CHUNK, (c + 1) * FF_CHUNK)
        cols_v = slice(D_FF + c * FF_CHUNK, D_FF + (c + 1) * FF_CHUNK)
        return (jnp.dot(h2, wup_ref[:, cols_a], preferred_element_type=_f32), cols_a,
                jnp.dot(h2, wup_ref[:, cols_v], preferred_element_type=_f32), cols_v)

    def conv(hc, cols, slab0):
        halo = halo_ref[:, cols]
        halo_ref[:, cols] = hc[T_FFN - SUBLANES:, :]
        return _causal_conv3(hc, halo, fcw_ref[:, cols], slab_ref, slab0)

    nxt = up_chunk(0)
    for c in range(N_FF_CHUNKS):
        ha, cols_a, hv, cols_v = nxt
        if c + 1 < N_FF_CHUNKS:
            nxt = up_chunk(c + 1)
        slab0 = (c % 2) * FFN_SLABS_PER_CHUNK
        a = conv(ha, cols_a, slab0)
        v = conv(hv, cols_v, slab0 + FF_CHUNK // LANES)
        act_ref[:, cols_a] = (a * jax.nn.sigmoid(a) * v).astype(_bf16)
    down = jnp.dot(act_ref[...], wdn_ref[...], preferred_element_type=_f32)
    o_ref[0] = x1 + gt2 * _rms(down, gpo_ref[...])


def _ffn(x, ya, yb, mod3, w_out_b, g_post_mix, g_pre_ffn, w_up_b, ffn_cw, w_down_b, g_post_ffn):
    n_t = SEQ // T_FFN
    return pl.pallas_call(
        _ffn_kernel,
        grid=(BATCH, n_t),
        in_specs=[
            pl.BlockSpec((1, T_FFN, D_MODEL), lambda b, t: (b, t, 0)),
            pl.BlockSpec((1, T_FFN, D_SSM), lambda b, t: (b, t, 0)),
            pl.BlockSpec((1, T_FFN, D_CONV), lambda b, t: (b, t, 0)),
            pl.BlockSpec((1, N_MOD, D_MODEL), lambda b, t: (b, 0, 0)),
            _const_spec((D_MODEL, D_MODEL)),
            _const_spec((1, D_MODEL)),
            _const_spec((1, D_MODEL)),
            _const_spec((D_MODEL, 2 * D_FF)),
            _const_spec((3, 2 * D_FF)),
            _const_spec((D_FF, D_MODEL)),
            _const_spec((1, D_MODEL)),
        ],
        out_specs=pl.BlockSpec((1, T_FFN, D_MODEL), lambda b, t: (b, t, 0)),
        out_shape=jax.ShapeDtypeStruct((BATCH, SEQ, D_MODEL), _f32),
        scratch_shapes=[
            pltpu.VMEM((SUBLANES, 2 * D_FF), _f32),
            pltpu.VMEM((T_FFN, D_FF), _bf16),
            pltpu.VMEM((2 * FFN_SLABS_PER_CHUNK, T_FFN + SUBLANES, LANES), _f32),
        ],
        compiler_params=pltpu.CompilerParams(
            dimension_semantics=("arbitrary", "arbitrary"), vmem_limit_bytes=VMEM_LIMIT),
        name="outproj_convffn",
    )(x, ya, yb, mod3, w_out_b, g_post_mix, g_pre_ffn, w_up_b, ffn_cw, w_down_b, g_post_ffn)


def _head_mean_matrix(width, head):
    idx = jnp.arange(width) // head
    return jnp.where(idx[:, None] == idx[None, :], 1.0 / head, 0.0).astype(_bf16)


def kernel(x, c, w_ada, b_ada, g_pre_mix, g_post_mix, w_in, ssm_lam_re, ssm_lam_im, ssm_log_step, ssm_b_re, ssm_b_im, ssm_c_re, ssm_c_im, ssm_d, glu_w, glu_b, g_out_ssm, conv_w, g_out_conv, w_out, g_pre_ffn, g_post_ffn, w_up, ffn_conv_w, w_down):
    assert x.shape == (BATCH, SEQ, D_MODEL) and w_ada.shape[0] == 1
    row = lambda a: a.reshape(1, -1)

    mod3 = _modulation(c, w_ada[0], b_ada[0]).reshape(BATCH, N_MOD, D_MODEL)

    lam8, bs_w, cs_w, tz_w = _ssm_prep(
        ssm_lam_re[0], ssm_lam_im[0], ssm_log_step[0], ssm_b_re[0], ssm_b_im[0], ssm_c_re[0], ssm_c_im[0])

    u, yb = _premix(x, mod3, row(g_pre_mix[0]), w_in[0].astype(_bf16), conv_w[0], row(g_out_conv[0]),
                    _head_mean_matrix(D_CONV, D_CONV // CONV_HEADS))

    ya = _ssm(u, bs_w, lam8, cs_w, tz_w, row(ssm_d[0]), glu_w[0].astype(_bf16),
              row(glu_b[0]), row(g_out_ssm[0]), _head_mean_matrix(D_SSM, SSM_GROUP))

    return _ffn(x, ya, yb, mod3, w_out[0].astype(_bf16), row(g_post_mix[0]), row(g_pre_ffn[0]),
                w_up[0].astype(_bf16), ffn_conv_w[0], w_down[0].astype(_bf16), row(g_post_ffn[0]))
```

```python
import jax
import jax.numpy as jnp
from jax import lax
from jax.experimental import pallas as pl
from jax.experimental.pallas import tpu as pltpu

D_MODEL = 1024
BATCH = 8
SEQ = 4096
D_SSM = 512
D_CONV = 512
SSM_GROUP = 16
N_SSM_GROUPS = 32
SSM_STATE = 64
CONV_HEADS = 8
D_FF = 2816
N_MOD = 6
D_IN_PROJ = 2048
EPS = 1e-6
LAMBDA_RE_MAX = -1e-4

N_STATE = N_SSM_GROUPS * SSM_STATE
SUBLANES = 8
LANES = 128
GROUPS_PER_LANE_TILE = LANES // SSM_GROUP
N_LANE_TILES_U = D_SSM // LANES
STATE_PER_TILE = GROUPS_PER_LANE_TILE * SSM_STATE

T_PRE = 1024
T_SSM = 256
T_FFN = 1024
FF_CHUNK = 256
N_FF_CHUNKS = D_FF // FF_CHUNK
FFN_SLABS_PER_CHUNK = 2 * FF_CHUNK // LANES
M_CHUNK = 4
N_CHUNKS = T_SSM // M_CHUNK
CHUNK_ROWS = N_CHUNKS * BATCH
GROUPS_PER_BLOCK = GROUPS_PER_LANE_TILE // 2
BLOCKS_PER_TILE = GROUPS_PER_LANE_TILE // GROUPS_PER_BLOCK
N_BLOCKS = N_SSM_GROUPS // GROUPS_PER_BLOCK
BLOCK_LANES = GROUPS_PER_BLOCK * SSM_GROUP
BLOCK_STATE = GROUPS_PER_BLOCK * SSM_STATE
BLOCK_CHUNK = M_CHUNK * BLOCK_LANES
BLOCK_COLS = 2 * BLOCK_STATE
STEP_PAIRS = M_CHUNK // 2
VMEM_LIMIT = 60 * 1024 * 1024

_f32 = jnp.float32
_bf16 = jnp.bfloat16


def _const_spec(shape):
    nd = len(shape)
    return pl.BlockSpec(shape, lambda *_: (0,) * nd, pipeline_mode=pl.Buffered(1))


def _rms(x, g):
    ms = jnp.mean(x * x, axis=-1, keepdims=True)
    return x * lax.rsqrt(ms + EPS) * g


def _causal_conv3(h, halo, w, slab_ref, slab0):
    rows = h.shape[0]
    outs = []
    for k in range(h.shape[1] // LANES):
        lanes = slice(k * LANES, (k + 1) * LANES)
        slab_ref[slab0 + k, 0:SUBLANES, :] = halo[:, lanes]
        slab_ref[slab0 + k, SUBLANES:SUBLANES + rows, :] = h[:, lanes]
        h1 = slab_ref[slab0 + k, pl.ds(SUBLANES - 1, rows, stride=1), :]
        h2 = slab_ref[slab0 + k, pl.ds(SUBLANES - 2, rows, stride=1), :]
        outs.append(w[0:1, lanes] * h2 + w[1:2, lanes] * h1 + w[2:3, lanes] * h[:, lanes])
    return jnp.concatenate(outs, axis=-1)


def _mod_kernel(c_ref, w_ref, b_ref, o_ref):
    c = c_ref[...]
    c_act = c * jax.nn.sigmoid(c)
    o_ref[...] = jnp.dot(c_act, w_ref[...], preferred_element_type=_f32) + b_ref[...]


def _modulation(c, w_ada, b_ada):
    n_tile = D_MODEL
    return pl.pallas_call(
        _mod_kernel,
        grid=(N_MOD * D_MODEL // n_tile,),
        in_specs=[
            pl.BlockSpec((BATCH, D_MODEL), lambda j: (0, 0)),
            pl.BlockSpec((D_MODEL, n_tile), lambda j: (0, j)),
            pl.BlockSpec((1, n_tile), lambda j: (0, j)),
        ],
        out_specs=pl.BlockSpec((BATCH, n_tile), lambda j: (0, j)),
        out_shape=jax.ShapeDtypeStruct((BATCH, N_MOD * D_MODEL), _f32),
        name="adaln_mod",
    )(c, w_ada, b_ada.reshape(1, -1))


def _prep_kernel(lre_ref, lim_ref, lst_ref, bre_ref, bim_ref, cre_ref, cim_ref, d_ref,
                 lam8_ref, bs_ref, cs_ref, tz_ref):
    lre = jnp.minimum(lre_ref[0], LAMBDA_RE_MAX)
    lim = lim_ref[0]
    step = jnp.exp(lst_ref[0])
    log_mag, ang = lre * step, lim * step

    def lam_pow(k):
        mag = jnp.exp(k * log_mag)
        return mag * jnp.cos(k * ang), mag * jnp.sin(k * ang)

    pw = [lam_pow(float(k)) for k in range(M_CHUNK + 1)]

    def block_diag(blk):
        wide = jnp.concatenate([blk] * GROUPS_PER_BLOCK, axis=-1)
        r = lax.broadcasted_iota(jnp.int32, wide.shape, 0) // SSM_GROUP
        c = lax.broadcasted_iota(jnp.int32, wide.shape, 1) // SSM_STATE
        return jnp.where(r == c, wide, 0.0)

    a_re, a_im = pw[1]
    n_re = a_re - 1.0
    den = lre * lre + lim * lim
    q_re = (n_re * lre + a_im * lim) / den
    q_im = (a_im * lre - n_re * lim) / den
    b_re, b_im = block_diag(bre_ref[0]), block_diag(bim_ref[0])
    bb_re = q_re * b_re - q_im * b_im
    bb_im = q_re * b_im + q_im * b_re
    c_re, c_im = block_diag(cre_ref[0]), block_diag(cim_ref[0])

    lam8_ref[0] = jnp.broadcast_to(pw[M_CHUNK][0], (SUBLANES, BLOCK_STATE))
    lam8_ref[1] = jnp.broadcast_to(pw[M_CHUNK][1], (SUBLANES, BLOCK_STATE))
    for s in range(M_CHUNK):
        rows = slice(s * BLOCK_LANES, (s + 1) * BLOCK_LANES)
        p_re, p_im = pw[M_CHUNK - 1 - s]
        bs_ref[0, rows, 0:BLOCK_STATE] = (p_re * bb_re - p_im * bb_im).astype(_bf16)
        bs_ref[0, rows, BLOCK_STATE:] = (p_re * bb_im + p_im * bb_re).astype(_bf16)
        p_re, p_im = pw[s + 1]
        cs_ref[0, 0:BLOCK_STATE, rows] = (p_re * c_re - p_im * c_im).T.astype(_bf16)
        cs_ref[0, BLOCK_STATE:, rows] = (-(p_re * c_im + p_im * c_re)).T.astype(_bf16)
    tz_ref[...] = jnp.zeros(tz_ref.shape, _bf16)
    contract_p = (((1,), (1,)), ((), ()))
    for tau in range(M_CHUNK):
        p_re, p_im = pw[tau]
        ct_re = p_re * c_re - p_im * c_im
        ct_im = p_re * c_im + p_im * c_re
        k_tau = (lax.dot_general(bb_re, ct_re, contract_p, precision=lax.Precision.HIGHEST,
                                 preferred_element_type=_f32)
                 - lax.dot_general(bb_im, ct_im, contract_p, precision=lax.Precision.HIGHEST,
                                   preferred_element_type=_f32))
        if tau == 0:
            r = lax.broadcasted_iota(jnp.int32, k_tau.shape, 0)
            c = lax.broadcasted_iota(jnp.int32, k_tau.shape, 1)
            k_tau = k_tau + jnp.where(r == c, d_ref[0], 0.0)
        for s0 in range(M_CHUNK - tau):
            tz_ref[0, s0 * BLOCK_LANES:(s0 + 1) * BLOCK_LANES,
                   (s0 + tau) * BLOCK_LANES:(s0 + tau + 1) * BLOCK_LANES] = k_tau.astype(_bf16)


def _ssm_prep(lam_re, lam_im, log_step, b_re, b_im, c_re, c_im, d_skip):
    nb = N_BLOCKS
    lstep = jnp.broadcast_to(log_step[:, None], (N_SSM_GROUPS, SSM_STATE))
    as_row = lambda a: a.reshape(nb, 1, BLOCK_STATE)
    b_t = lambda a: a.transpose(0, 2, 1).reshape(nb, BLOCK_LANES, SSM_STATE)
    c_t = lambda a: a.reshape(nb, BLOCK_LANES, SSM_STATE)
    tile_spec = lambda r, c: pl.BlockSpec((1, r, c), lambda j: (j, 0, 0))
    return pl.pallas_call(
        _prep_kernel,
        grid=(nb,),
        in_specs=[tile_spec(1, BLOCK_STATE)] * 3 + [tile_spec(BLOCK_LANES, SSM_STATE)] * 4
        + [tile_spec(1, BLOCK_LANES)],
        out_specs=[
            pl.BlockSpec((2, SUBLANES, BLOCK_STATE), lambda j: (0, 0, j)),
            tile_spec(BLOCK_CHUNK, BLOCK_COLS),
            tile_spec(BLOCK_COLS, BLOCK_CHUNK),
            tile_spec(BLOCK_CHUNK, BLOCK_CHUNK),
        ],
        out_shape=(
            jax.ShapeDtypeStruct((2, SUBLANES, N_STATE), _f32),
            jax.ShapeDtypeStruct((nb, BLOCK_CHUNK, BLOCK_COLS), _bf16),
            jax.ShapeDtypeStruct((nb, BLOCK_COLS, BLOCK_CHUNK), _bf16),
            jax.ShapeDtypeStruct((nb, BLOCK_CHUNK, BLOCK_CHUNK), _bf16),
        ),
        name="s5_discretise",
    )(as_row(lam_re), as_row(lam_im), as_row(lstep), b_t(b_re), b_t(b_im), c_t(c_re), c_t(c_im),
      d_skip.reshape(nb, 1, BLOCK_LANES))


def _premix_kernel(x_ref, mod_ref, g_ref, win_ref, cw_ref, gconv_ref, ones_ref,
                   u_ref, yb_ref, halo_ref, slab_ref):
    ti = pl.program_id(1)

    @pl.when(ti == 0)
    def _():
        halo_ref[...] = jnp.zeros(halo_ref.shape, _f32)

    x = x_ref[0]
    sh = mod_ref[0, 0:1, :]
    sc = mod_ref[0, 1:2, :]
    h = _rms(x, g_ref[...] * (1.0 + sc)) + sh
    proj = jnp.dot(h.astype(_bf16), win_ref[...], preferred_element_type=_f32)
    u_ref[0] = proj[:, 0:D_SSM].astype(_bf16)
    bg = proj[:, D_SSM:D_SSM + D_CONV]
    cg = proj[:, D_SSM + D_CONV:D_SSM + 2 * D_CONV]
    v = proj[:, D_SSM + 2 * D_CONV:]
    cv = cg * v
    halo = halo_ref[...]
    halo_ref[...] = cv[T_PRE - SUBLANES:, :]
    yb = bg * _causal_conv3(cv, halo, cw_ref, slab_ref, 0)
    ms = jnp.dot((yb * yb).astype(_bf16), ones_ref[...], preferred_element_type=_f32)
    yb_ref[0] = (yb * lax.rsqrt(ms + EPS) * gconv_ref[...]).astype(_bf16)


def _premix(x, mod3, g_pre, w_in_b, conv_w, g_out_conv, ones_conv):
    n_t = SEQ // T_PRE
    return pl.pallas_call(
        _premix_kernel,
        grid=(BATCH, n_t),
        in_specs=[
            pl.BlockSpec((1, T_PRE, D_MODEL), lambda b, t: (b, t, 0)),
            pl.BlockSpec((1, N_MOD, D_MODEL), lambda b, t: (b, 0, 0)),
            _const_spec((1, D_MODEL)),
            _const_spec((D_MODEL, D_IN_PROJ)),
            _const_spec((3, D_CONV)),
            _const_spec((1, D_CONV)),
            _const_spec((D_CONV, D_CONV)),
        ],
        out_specs=[
            pl.BlockSpec((1, T_PRE, D_SSM), lambda b, t: (b, t, 0)),
            pl.BlockSpec((1, T_PRE, D_CONV), lambda b, t: (b, t, 0)),
        ],
        out_shape=[
            jax.ShapeDtypeStruct((BATCH, SEQ, D_SSM), _bf16),
            jax.ShapeDtypeStruct((BATCH, SEQ, D_CONV), _bf16),
        ],
        scratch_shapes=[
            pltpu.VMEM((SUBLANES, D_CONV), _f32),
            pltpu.VMEM((D_CONV // LANES, T_PRE + SUBLANES, LANES), _f32),
        ],
        compiler_params=pltpu.CompilerParams(
            dimension_semantics=("arbitrary", "arbitrary"), vmem_limit_bytes=VMEM_LIMIT),
        name="premix",
    )(x, mod3, g_pre, w_in_b, conv_w, g_out_conv, ones_conv)


def _ssm_kernel(u_ref, bs_ref, lam_ref, cs_ref, tz_ref, gluw_ref, glub_ref, g_ref, ones_ref,
                y_ref, st_ref, u32_ref, ufl_ref, s_ref, yfl_ref, y32_ref):
    i = pl.program_id(0)

    @pl.when(i == 0)
    def _():
        st_ref[...] = jnp.zeros(st_ref.shape, _f32)

    low_half = lax.broadcasted_iota(jnp.int32, (N_CHUNKS, LANES), 1) < BLOCK_LANES
    swap_halves = lambda v: pltpu.roll(v, BLOCK_LANES, axis=1)

    u = u_ref[...].reshape(BATCH * T_SSM, D_SSM)
    for j in range(N_LANE_TILES_U):
        u32_ref[j] = u[:, j * LANES:(j + 1) * LANES].astype(_f32)
    for j in range(N_LANE_TILES_U):
        for q in range(STEP_PAIRS):
            for b in range(BATCH):
                v0 = u32_ref[j, pl.ds(b * T_SSM + 2 * q, N_CHUNKS, stride=M_CHUNK), :]
                v1 = u32_ref[j, pl.ds(b * T_SSM + 2 * q + 1, N_CHUNKS, stride=M_CHUNK), :]
                dst = pl.ds(b, N_CHUNKS, stride=BATCH)
                ufl_ref[BLOCKS_PER_TILE * j, q, dst, :] = jnp.where(low_half, v0, swap_halves(v1))
                ufl_ref[BLOCKS_PER_TILE * j + 1, q, dst, :] = jnp.where(low_half, swap_halves(v0), v1)

    def chunk_lhs(blk):
        return jnp.concatenate([ufl_ref[blk, q] for q in range(STEP_PAIRS)], axis=-1).astype(_bf16)

    def bproj(j):
        for blk in range(BLOCKS_PER_TILE * j, BLOCKS_PER_TILE * (j + 1)):
            s_ref[:, blk * BLOCK_COLS:(blk + 1) * BLOCK_COLS] = jnp.dot(
                chunk_lhs(blk), bs_ref[blk], preferred_element_type=_f32)

    def scan(j):
        re, lam = [], []
        for blk in range(BLOCKS_PER_TILE * j, BLOCKS_PER_TILE * (j + 1)):
            for k in range(BLOCK_STATE // LANES):
                re.append(slice(blk * BLOCK_COLS + k * LANES, blk * BLOCK_COLS + (k + 1) * LANES))
                lam.append(slice(blk * BLOCK_STATE + k * LANES, blk * BLOCK_STATE + (k + 1) * LANES))
        im = [slice(r.start + BLOCK_STATE, r.stop + BLOCK_STATE) for r in re]
        n = len(re)
        lre = [lam_ref[0, :, ln] for ln in lam]
        lim = [lam_ref[1, :, ln] for ln in lam]
        xre = [st_ref[:, r] for r in re]
        xim = [st_ref[:, r] for r in im]
        for c in range(N_CHUNKS):
            rows = slice(c * BATCH, (c + 1) * BATCH)
            for k in range(n):
                loc_re, loc_im = s_ref[rows, re[k]], s_ref[rows, im[k]]
                s_ref[rows, re[k]] = xre[k]
                s_ref[rows, im[k]] = xim[k]
                xre[k], xim[k] = (lre[k] * xre[k] - lim[k] * xim[k] + loc_re,
                                  lre[k] * xim[k] + lim[k] * xre[k] + loc_im)
        for k in range(n):
            st_ref[:, re[k]] = xre[k]
            st_ref[:, im[k]] = xim[k]

    def cproj(j):
        for blk in range(BLOCKS_PER_TILE * j, BLOCKS_PER_TILE * (j + 1)):
            x_in = s_ref[:, blk * BLOCK_COLS:(blk + 1) * BLOCK_COLS].astype(_bf16)
            yf = jnp.dot(x_in, cs_ref[blk], preferred_element_type=_f32)
            yf = yf + jnp.dot(chunk_lhs(blk), tz_ref[blk], preferred_element_type=_f32)
            for q in range(STEP_PAIRS):
                yfl_ref[blk, q] = yf[:, q * LANES:(q + 1) * LANES]

    bproj(0)
    for j in range(N_LANE_TILES_U):
        if j + 1 < N_LANE_TILES_U:
            bproj(j + 1)
        if j > 0:
            cproj(j - 1)
        scan(j)
    cproj(N_LANE_TILES_U - 1)

    for j in range(N_LANE_TILES_U):
        for q in range(STEP_PAIRS):
            for b in range(BATCH):
                src = pl.ds(b, N_CHUNKS, stride=BATCH)
                lo = yfl_ref[BLOCKS_PER_TILE * j, q, src, :]
                hi = yfl_ref[BLOCKS_PER_TILE * j + 1, q, src, :]
                y32_ref[j, pl.ds(b * T_SSM + 2 * q, N_CHUNKS, stride=M_CHUNK), :] = (
                    jnp.where(low_half, lo, swap_halves(hi)))
                y32_ref[j, pl.ds(b * T_SSM + 2 * q + 1, N_CHUNKS, stride=M_CHUNK), :] = (
                    jnp.where(low_half, swap_halves(lo), hi))
    y = jnp.concatenate([y32_ref[j] for j in range(N_LANE_TILES_U)], axis=-1)
    z = jax.nn.gelu(y).astype(_bf16)
    gate = jnp.dot(z, gluw_ref[...], preferred_element_type=_f32) + glub_ref[...]
    z = z * jax.nn.sigmoid(gate.astype(_bf16))
    ms = jnp.dot(z * z, ones_ref[...], preferred_element_type=_f32)
    out = z * lax.rsqrt(ms + EPS).astype(_bf16) * g_ref[...].astype(_bf16)
    y_ref[...] = out.reshape(BATCH, T_SSM, D_SSM)


def _ssm(u, bs_w, lam8, cs_w, tz_w, glu_w_b, glu_b, g_out_ssm, ones_ssm):
    rows = BATCH * T_SSM
    return pl.pallas_call(
        _ssm_kernel,
        grid=(SEQ // T_SSM,),
        in_specs=[
            pl.BlockSpec((BATCH, T_SSM, D_SSM), lambda i: (0, i, 0)),
            _const_spec((N_BLOCKS, BLOCK_CHUNK, BLOCK_COLS)),
            _const_spec((2, SUBLANES, N_STATE)),
            _const_spec((N_BLOCKS, BLOCK_COLS, BLOCK_CHUNK)),
            _const_spec((N_BLOCKS, BLOCK_CHUNK, BLOCK_CHUNK)),
            _const_spec((D_SSM, D_SSM)),
            _const_spec((1, D_SSM)),
            _const_spec((1, D_SSM)),
            _const_spec((D_SSM, D_SSM)),
        ],
        out_specs=pl.BlockSpec((BATCH, T_SSM, D_SSM), lambda i: (0, i, 0)),
        out_shape=jax.ShapeDtypeStruct((BATCH, SEQ, D_SSM), _bf16),
        scratch_shapes=[
            pltpu.VMEM((SUBLANES, N_BLOCKS * BLOCK_COLS), _f32),
            pltpu.VMEM((N_LANE_TILES_U, rows, LANES), _f32),
            pltpu.VMEM((N_BLOCKS, STEP_PAIRS, CHUNK_ROWS, LANES), _f32),
            pltpu.VMEM((CHUNK_ROWS, N_BLOCKS * BLOCK_COLS), _f32),
            pltpu.VMEM((N_BLOCKS, STEP_PAIRS, CHUNK_ROWS, LANES), _f32),
            pltpu.VMEM((N_LANE_TILES_U, rows, LANES), _f32),
        ],
        compiler_params=pltpu.CompilerParams(
            dimension_semantics=("arbitrary",), vmem_limit_bytes=VMEM_LIMIT),
        name="s5_mixer",
    )(u, bs_w, lam8, cs_w, tz_w, glu_w_b, glu_b, g_out_ssm, ones_ssm)


def _ffn_kernel(x_ref, ya_ref, yb_ref, mod_ref, wout_ref, gpm_ref, gpf_ref, wup_ref, fcw_ref, wdn_ref, gpo_ref,
                o_ref, halo_ref, act_ref, slab_ref):
    ti = pl.program_id(1)

    @pl.when(ti == 0)
    def _():
        halo_ref[...] = jnp.zeros(halo_ref.shape, _f32)

    x = x_ref[0]
    gt1 = mod_ref[0, 2:3, :]
    sh2 = mod_ref[0, 3:4, :]
    sc2 = mod_ref[0, 4:5, :]
    gt2 = mod_ref[0, 5:6, :]
    mix = jnp.dot(ya_ref[0], wout_ref[0:D_SSM, :], preferred_element_type=_f32)
    mix = mix + jnp.dot(yb_ref[0], wout_ref[D_SSM:, :], preferred_element_type=_f32)
    x1 = x + _rms(mix, gpm_ref[...] * gt1)
    h2 = (_rms(x1, gpf_ref[...] * (1.0 + sc2)) + sh2).astype(_bf16)

    def up_chunk(c):
        cols_a = slice(c * FF_CHUNK, (c + 1) * FF_CHUNK)
        cols_v = slice(D_FF + c * FF_CHUNK, D_FF + (c + 1) * FF_CHUNK)
        return (jnp.dot(h2, wup_ref[:, cols_a], preferred_element_type=_f32), cols_a,
                jnp.dot(h2, wup_ref[:, cols_v], preferred_element_type=_f32), cols_v)

    def conv(hc, cols, slab0):
        halo = halo_ref[:, cols]
        halo_ref[:, cols] = hc[T_FFN - SUBLANES:, :]
        return _causal_conv3(hc, halo, fcw_ref[:, cols], slab_ref, slab0)

    nxt = up_chunk(0)
    for c in range(N_FF_CHUNKS):
        ha, cols_a, hv, cols_v = nxt
        if c + 1 < N_FF_CHUNKS:
            nxt = up_chunk(c + 1)
        slab0 = (c % 2) * FFN_SLABS_PER_CHUNK
        a = conv(ha, cols_a, slab0)
        v = conv(hv, cols_v, slab0 + FF_CHUNK // LANES)
        act_ref[:, cols_a] = (a * jax.nn.sigmoid(a) * v).astype(_bf16)
    down = jnp.dot(act_ref[...], wdn_ref[...], preferred_element_type=_f32)
    o_ref[0] = x1 + _rms(down, gpo_ref[...] * gt2)


def _ffn(x, ya, yb, mod3, w_out_b, g_post_mix, g_pre_ffn, w_up_b, ffn_cw, w_down_b, g_post_ffn):
    n_t = SEQ // T_FFN
    return pl.pallas_call(
        _ffn_kernel,
        grid=(BATCH, n_t),
        in_specs=[
            pl.BlockSpec((1, T_FFN, D_MODEL), lambda b, t: (b, t, 0)),
            pl.BlockSpec((1, T_FFN, D_SSM), lambda b, t: (b, t, 0)),
            pl.BlockSpec((1, T_FFN, D_CONV), lambda b, t: (b, t, 0)),
            pl.BlockSpec((1, N_MOD, D_MODEL), lambda b, t: (b, 0, 0)),
            _const_spec((D_MODEL, D_MODEL)),
            _const_spec((1, D_MODEL)),
            _const_spec((1, D_MODEL)),
            _const_spec((D_MODEL, 2 * D_FF)),
            _const_spec((3, 2 * D_FF)),
            _const_spec((D_FF, D_MODEL)),
            _const_spec((1, D_MODEL)),
        ],
        out_specs=pl.BlockSpec((1, T_FFN, D_MODEL), lambda b, t: (b, t, 0)),
        out_shape=jax.ShapeDtypeStruct((BATCH, SEQ, D_MODEL), _f32),
        scratch_shapes=[
            pltpu.VMEM((SUBLANES, 2 * D_FF), _f32),
            pltpu.VMEM((T_FFN, D_FF), _bf16),
            pltpu.VMEM((2 * FFN_SLABS_PER_CHUNK, T_FFN + SUBLANES, LANES), _f32),
        ],
        compiler_params=pltpu.CompilerParams(
            dimension_semantics=("arbitrary", "arbitrary"), vmem_limit_bytes=VMEM_LIMIT),
        name="outproj_convffn",
    )(x, ya, yb, mod3, w_out_b, g_post_mix, g_pre_ffn, w_up_b, ffn_cw, w_down_b, g_post_ffn)


def _head_mean_matrix(width, head):
    idx = jnp.arange(width) // head
    return jnp.where(idx[:, None] == idx[None, :], 1.0 / head, 0.0).astype(_bf16)


def kernel(x, c, w_ada, b_ada, g_pre_mix, g_post_mix, w_in, ssm_lam_re, ssm_lam_im, ssm_log_step, ssm_b_re, ssm_b_im, ssm_c_re, ssm_c_im, ssm_d, glu_w, glu_b, g_out_ssm, conv_w, g_out_conv, w_out, g_pre_ffn, g_post_ffn, w_up, ffn_conv_w, w_down):
    assert x.shape == (BATCH, SEQ, D_MODEL) and w_ada.shape[0] == 1
    row = lambda a: a.reshape(1, -1)

    mod3 = _modulation(c, w_ada[0], b_ada[0]).reshape(BATCH, N_MOD, D_MODEL)

    lam8, bs_w, cs_w, tz_w = _ssm_prep(
        ssm_lam_re[0], ssm_lam_im[0], ssm_log_step[0], ssm_b_re[0], ssm_b_im[0], ssm_c_re[0], ssm_c_im[0],
        ssm_d[0])

    u, yb = _premix(x, mod3, row(g_pre_mix[0]), w_in[0].astype(_bf16), conv_w[0], row(g_out_conv[0]),
                    _head_mean_matrix(D_CONV, D_CONV // CONV_HEADS))

    ya = _ssm(u, bs_w, lam8, cs_w, tz_w, glu_w[0].astype(_bf16),
              row(glu_b[0]), row(g_out_ssm[0]), _head_mean_matrix(D_SSM, SSM_GROUP))

    return _ffn(x, ya, yb, mod3, w_out[0].astype(_bf16), row(g_post_mix[0]), row(g_pre_ffn[0]),
                w_up[0].astype(_bf16), ffn_conv_w[0], w_down[0].astype(_bf16), row(g_post_ffn[0]))
```

```python
import jax
import jax.numpy as jnp
from jax import lax
from jax.experimental import pallas as pl
from jax.experimental.pallas import tpu as pltpu

D_MODEL = 1024
BATCH = 8
SEQ = 4096
D_SSM = 512
D_CONV = 512
SSM_GROUP = 16
N_SSM_GROUPS = 32
SSM_STATE = 64
CONV_HEADS = 8
D_FF = 2816
N_MOD = 6
D_IN_PROJ = 2048
EPS = 1e-6
LAMBDA_RE_MAX = -1e-4

N_STATE = N_SSM_GROUPS * SSM_STATE
SUBLANES = 8
LANES = 128
GROUPS_PER_LANE_TILE = LANES // SSM_GROUP
N_LANE_TILES_U = D_SSM // LANES
STATE_PER_TILE = GROUPS_PER_LANE_TILE * SSM_STATE

T_PRE = 1024
T_SSM = 256
T_FFN = 1024
FF_CHUNK = 256
N_FF_CHUNKS = D_FF // FF_CHUNK
FFN_SLABS_PER_CHUNK = 2 * FF_CHUNK // LANES
M_CHUNK = 4
N_CHUNKS = T_SSM // M_CHUNK
CHUNK_ROWS = N_CHUNKS * BATCH
GROUPS_PER_BLOCK = GROUPS_PER_LANE_TILE // 2
BLOCKS_PER_TILE = GROUPS_PER_LANE_TILE // GROUPS_PER_BLOCK
N_BLOCKS = N_SSM_GROUPS // GROUPS_PER_BLOCK
BLOCK_LANES = GROUPS_PER_BLOCK * SSM_GROUP
BLOCK_STATE = GROUPS_PER_BLOCK * SSM_STATE
BLOCK_CHUNK = M_CHUNK * BLOCK_LANES
BLOCK_COLS = 2 * BLOCK_STATE
STEP_PAIRS = M_CHUNK // 2
VMEM_LIMIT = 60 * 1024 * 1024

_f32 = jnp.float32
_bf16 = jnp.bfloat16


def _const_spec(shape):
    nd = len(shape)
    return pl.BlockSpec(shape, lambda *_: (0,) * nd, pipeline_mode=pl.Buffered(1))


def _rms(x, g):
    ms = jnp.mean(x * x, axis=-1, keepdims=True)
    return x * lax.rsqrt(ms + EPS) * g


def _causal_conv3(h, halo, w, slab_ref, slab0):
    rows = h.shape[0]
    outs = []
    for k in range(h.shape[1] // LANES):
        lanes = slice(k * LANES, (k + 1) * LANES)
        slab_ref[slab0 + k, 0:SUBLANES, :] = halo[:, lanes]
        slab_ref[slab0 + k, SUBLANES:SUBLANES + rows, :] = h[:, lanes]
        h1 = slab_ref[slab0 + k, pl.ds(SUBLANES - 1, rows, stride=1), :]
        h2 = slab_ref[slab0 + k, pl.ds(SUBLANES - 2, rows, stride=1), :]
        outs.append(w[0:1, lanes] * h2 + w[1:2, lanes] * h1 + w[2:3, lanes] * h[:, lanes])
    return jnp.concatenate(outs, axis=-1)


def _mod_kernel(c_ref, w_ref, b_ref, o_ref):
    c = c_ref[...]
    c_act = c * jax.nn.sigmoid(c)
    o_ref[...] = jnp.dot(c_act, w_ref[...], preferred_element_type=_f32) + b_ref[...]


def _modulation(c, w_ada, b_ada):
    n_tile = 2 * D_MODEL
    return pl.pallas_call(
        _mod_kernel,
        grid=(N_MOD * D_MODEL // n_tile,),
        in_specs=[
            pl.BlockSpec((BATCH, D_MODEL), lambda j: (0, 0)),
            pl.BlockSpec((D_MODEL, n_tile), lambda j: (0, j)),
            pl.BlockSpec((1, n_tile), lambda j: (0, j)),
        ],
        out_specs=pl.BlockSpec((BATCH, n_tile), lambda j: (0, j)),
        out_shape=jax.ShapeDtypeStruct((BATCH, N_MOD * D_MODEL), _f32),
        name="adaln_mod",
    )(c, w_ada, b_ada.reshape(1, -1))


def _prep_kernel(lre_ref, lim_ref, lst_ref, bre_ref, bim_ref, cre_ref, cim_ref, d_ref,
                 lam8_ref, bs_ref, cs_ref, tz_ref):
    lre = jnp.minimum(lre_ref[0], LAMBDA_RE_MAX)
    lim = lim_ref[0]
    step = jnp.exp(lst_ref[0])
    log_mag, ang = lre * step, lim * step

    def lam_pow(k):
        mag = jnp.exp(k * log_mag)
        return mag * jnp.cos(k * ang), mag * jnp.sin(k * ang)

    pw = [lam_pow(float(k)) for k in range(M_CHUNK + 1)]

    def block_diag(blk):
        wide = jnp.concatenate([blk] * GROUPS_PER_BLOCK, axis=-1)
        r = lax.broadcasted_iota(jnp.int32, wide.shape, 0) // SSM_GROUP
        c = lax.broadcasted_iota(jnp.int32, wide.shape, 1) // SSM_STATE
        return jnp.where(r == c, wide, 0.0)

    a_re, a_im = pw[1]
    n_re = a_re - 1.0
    den = lre * lre + lim * lim
    q_re = (n_re * lre + a_im * lim) / den
    q_im = (a_im * lre - n_re * lim) / den
    b_re, b_im = block_diag(bre_ref[0]), block_diag(bim_ref[0])
    bb_re = q_re * b_re - q_im * b_im
    bb_im = q_re * b_im + q_im * b_re
    c_re, c_im = block_diag(cre_ref[0]), block_diag(cim_ref[0])

    lam8_ref[0] = jnp.broadcast_to(pw[M_CHUNK][0], (SUBLANES, BLOCK_STATE))
    lam8_ref[1] = jnp.broadcast_to(pw[M_CHUNK][1], (SUBLANES, BLOCK_STATE))
    for s in range(M_CHUNK):
        rows = slice(s * BLOCK_LANES, (s + 1) * BLOCK_LANES)
        p_re, p_im = pw[M_CHUNK - 1 - s]
        bs_ref[0, rows, 0:BLOCK_STATE] = (p_re * bb_re - p_im * bb_im).astype(_bf16)
        bs_ref[0, rows, BLOCK_STATE:] = (p_re * bb_im + p_im * bb_re).astype(_bf16)
        p_re, p_im = pw[s + 1]
        cs_ref[0, 0:BLOCK_STATE, rows] = (p_re * c_re - p_im * c_im).T.astype(_bf16)
        cs_ref[0, BLOCK_STATE:, rows] = (-(p_re * c_im + p_im * c_re)).T.astype(_bf16)
    tz_ref[...] = jnp.zeros(tz_ref.shape, _bf16)
    contract_p = (((1,), (1,)), ((), ()))
    for tau in range(M_CHUNK):
        p_re, p_im = pw[tau]
        ct_re = p_re * c_re - p_im * c_im
        ct_im = p_re * c_im + p_im * c_re
        k_tau = (lax.dot_general(bb_re, ct_re, contract_p, precision=lax.Precision.HIGHEST,
                                 preferred_element_type=_f32)
                 - lax.dot_general(bb_im, ct_im, contract_p, precision=lax.Precision.HIGHEST,
                                   preferred_element_type=_f32))
        if tau == 0:
            r = lax.broadcasted_iota(jnp.int32, k_tau.shape, 0)
            c = lax.broadcasted_iota(jnp.int32, k_tau.shape, 1)
            k_tau = k_tau + jnp.where(r == c, d_ref[0], 0.0)
        for s0 in range(M_CHUNK - tau):
            tz_ref[0, s0 * BLOCK_LANES:(s0 + 1) * BLOCK_LANES,
                   (s0 + tau) * BLOCK_LANES:(s0 + tau + 1) * BLOCK_LANES] = k_tau.astype(_bf16)


def _ssm_prep(lam_re, lam_im, log_step, b_re, b_im, c_re, c_im, d_skip):
    nb = N_BLOCKS
    lstep = jnp.broadcast_to(log_step[:, None], (N_SSM_GROUPS, SSM_STATE))
    as_row = lambda a: a.reshape(nb, 1, BLOCK_STATE)
    b_t = lambda a: a.transpose(0, 2, 1).reshape(nb, BLOCK_LANES, SSM_STATE)
    c_t = lambda a: a.reshape(nb, BLOCK_LANES, SSM_STATE)
    tile_spec = lambda r, c: pl.BlockSpec((1, r, c), lambda j: (j, 0, 0))
    return pl.pallas_call(
        _prep_kernel,
        grid=(nb,),
        in_specs=[tile_spec(1, BLOCK_STATE)] * 3 + [tile_spec(BLOCK_LANES, SSM_STATE)] * 4
        + [tile_spec(1, BLOCK_LANES)],
        out_specs=[
            pl.BlockSpec((2, SUBLANES, BLOCK_STATE), lambda j: (0, 0, j)),
            tile_spec(BLOCK_CHUNK, BLOCK_COLS),
            tile_spec(BLOCK_COLS, BLOCK_CHUNK),
            tile_spec(BLOCK_CHUNK, BLOCK_CHUNK),
        ],
        out_shape=(
            jax.ShapeDtypeStruct((2, SUBLANES, N_STATE), _f32),
            jax.ShapeDtypeStruct((nb, BLOCK_CHUNK, BLOCK_COLS), _bf16),
            jax.ShapeDtypeStruct((nb, BLOCK_COLS, BLOCK_CHUNK), _bf16),
            jax.ShapeDtypeStruct((nb, BLOCK_CHUNK, BLOCK_CHUNK), _bf16),
        ),
        name="s5_discretise",
    )(as_row(lam_re), as_row(lam_im), as_row(lstep), b_t(b_re), b_t(b_im), c_t(c_re), c_t(c_im),
      d_skip.reshape(nb, 1, BLOCK_LANES))


def _premix_kernel(x_ref, mod_ref, g_ref, win_ref, cw_ref, gconv_ref, ones_ref,
                   u_ref, yb_ref, halo_ref, slab_ref):
    ti = pl.program_id(1)

    @pl.when(ti == 0)
    def _():
        halo_ref[...] = jnp.zeros(halo_ref.shape, _f32)

    x = x_ref[0]
    sh = mod_ref[0, 0:1, :]
    sc = mod_ref[0, 1:2, :]
    h = _rms(x, g_ref[...] * (1.0 + sc)) + sh
    hb = h.astype(_bf16)
    proj = jnp.dot(hb, win_ref[:, D_SSM:], preferred_element_type=_f32)
    u_ref[0] = jnp.dot(hb, win_ref[:, 0:D_SSM], preferred_element_type=_f32).astype(_bf16)
    bg = proj[:, 0:D_CONV]
    cg = proj[:, D_CONV:2 * D_CONV]
    v = proj[:, 2 * D_CONV:]
    cv = cg * v
    halo = halo_ref[...]
    halo_ref[...] = cv[T_PRE - SUBLANES:, :]
    yb = bg * _causal_conv3(cv, halo, cw_ref, slab_ref, 0)
    ms = jnp.dot((yb * yb).astype(_bf16), ones_ref[...], preferred_element_type=_f32)
    yb_ref[0] = (yb * lax.rsqrt(ms + EPS) * gconv_ref[...]).astype(_bf16)


def _premix(x, mod3, g_pre, w_in_b, conv_w, g_out_conv, ones_conv):
    n_t = SEQ // T_PRE
    return pl.pallas_call(
        _premix_kernel,
        grid=(BATCH, n_t),
        in_specs=[
            pl.BlockSpec((1, T_PRE, D_MODEL), lambda b, t: (b, t, 0)),
            pl.BlockSpec((1, N_MOD, D_MODEL), lambda b, t: (b, 0, 0)),
            _const_spec((1, D_MODEL)),
            _const_spec((D_MODEL, D_IN_PROJ)),
            _const_spec((3, D_CONV)),
            _const_spec((1, D_CONV)),
            _const_spec((D_CONV, D_CONV)),
        ],
        out_specs=[
            pl.BlockSpec((1, T_PRE, D_SSM), lambda b, t: (b, t, 0)),
            pl.BlockSpec((1, T_PRE, D_CONV), lambda b, t: (b, t, 0)),
        ],
        out_shape=[
            jax.ShapeDtypeStruct((BATCH, SEQ, D_SSM), _bf16),
            jax.ShapeDtypeStruct((BATCH, SEQ, D_CONV), _bf16),
        ],
        scratch_shapes=[
            pltpu.VMEM((SUBLANES, D_CONV), _f32),
            pltpu.VMEM((D_CONV // LANES, T_PRE + SUBLANES, LANES), _f32),
        ],
        compiler_params=pltpu.CompilerParams(
            dimension_semantics=("arbitrary", "arbitrary"), vmem_limit_bytes=VMEM_LIMIT),
        name="premix",
    )(x, mod3, g_pre, w_in_b, conv_w, g_out_conv, ones_conv)


def _ssm_kernel(u_ref, bs_ref, lam_ref, cs_ref, tz_ref, gluw_ref, glub_ref, g_ref, ones_ref,
                y_ref, st_ref, u32_ref, ufl_ref, s_ref, yfl_ref, y32_ref):
    i = pl.program_id(0)

    @pl.when(i == 0)
    def _():
        st_ref[...] = jnp.zeros(st_ref.shape, _f32)

    low_half = lax.broadcasted_iota(jnp.int32, (N_CHUNKS, LANES), 1) < BLOCK_LANES
    swap_halves = lambda v: pltpu.roll(v, BLOCK_LANES, axis=1)

    u = u_ref[...].reshape(BATCH * T_SSM, D_SSM)
    for j in range(N_LANE_TILES_U):
        u32_ref[j] = u[:, j * LANES:(j + 1) * LANES].astype(_f32)
    for j in range(N_LANE_TILES_U):
        for q in range(STEP_PAIRS):
            for b in range(BATCH):
                v0 = u32_ref[j, pl.ds(b * T_SSM + 2 * q, N_CHUNKS, stride=M_CHUNK), :]
                v1 = u32_ref[j, pl.ds(b * T_SSM + 2 * q + 1, N_CHUNKS, stride=M_CHUNK), :]
                dst = pl.ds(b, N_CHUNKS, stride=BATCH)
                ufl_ref[BLOCKS_PER_TILE * j, q, dst, :] = jnp.where(low_half, v0, swap_halves(v1))
                ufl_ref[BLOCKS_PER_TILE * j + 1, q, dst, :] = jnp.where(low_half, swap_halves(v0), v1)

    def chunk_lhs(blk):
        return jnp.concatenate([ufl_ref[blk, q] for q in range(STEP_PAIRS)], axis=-1).astype(_bf16)

    def bproj(j):
        for blk in range(BLOCKS_PER_TILE * j, BLOCKS_PER_TILE * (j + 1)):
            s_ref[:, blk * BLOCK_COLS:(blk + 1) * BLOCK_COLS] = jnp.dot(
                chunk_lhs(blk), bs_ref[blk], preferred_element_type=_f32)

    def scan(j):
        re, lam = [], []
        for blk in range(BLOCKS_PER_TILE * j, BLOCKS_PER_TILE * (j + 1)):
            for k in range(BLOCK_STATE // LANES):
                re.append(slice(blk * BLOCK_COLS + k * LANES, blk * BLOCK_COLS + (k + 1) * LANES))
                lam.append(slice(blk * BLOCK_STATE + k * LANES, blk * BLOCK_STATE + (k + 1) * LANES))
        im = [slice(r.start + BLOCK_STATE, r.stop + BLOCK_STATE) for r in re]
        n = len(re)
        lre = [lam_ref[0, :, ln] for ln in lam]
        lim = [lam_ref[1, :, ln] for ln in lam]
        xre = [st_ref[:, r] for r in re]
        xim = [st_ref[:, r] for r in im]
        for c in range(N_CHUNKS):
            rows = slice(c * BATCH, (c + 1) * BATCH)
            for k in range(n):
                loc_re, loc_im = s_ref[rows, re[k]], s_ref[rows, im[k]]
                s_ref[rows, re[k]] = xre[k]
                s_ref[rows, im[k]] = xim[k]
                xre[k], xim[k] = (lre[k] * xre[k] - lim[k] * xim[k] + loc_re,
                                  lre[k] * xim[k] + lim[k] * xre[k] + loc_im)
        for k in range(n):
            st_ref[:, re[k]] = xre[k]
            st_ref[:, im[k]] = xim[k]

    def cproj(j):
        for blk in range(BLOCKS_PER_TILE * j, BLOCKS_PER_TILE * (j + 1)):
            x_in = s_ref[:, blk * BLOCK_COLS:(blk + 1) * BLOCK_COLS].astype(_bf16)
            yf = jnp.dot(x_in, cs_ref[blk], preferred_element_type=_f32)
            yf = yf + jnp.dot(chunk_lhs(blk), tz_ref[blk], preferred_element_type=_f32)
            for q in range(STEP_PAIRS):
                yfl_ref[blk, q] = yf[:, q * LANES:(q + 1) * LANES]

    bproj(0)
    for j in range(N_LANE_TILES_U):
        if j + 1 < N_LANE_TILES_U:
            bproj(j + 1)
        if j > 0:
            cproj(j - 1)
        scan(j)
    cproj(N_LANE_TILES_U - 1)

    for j in range(N_LANE_TILES_U):
        for q in range(STEP_PAIRS):
            for b in range(BATCH):
                src = pl.ds(b, N_CHUNKS, stride=BATCH)
                lo = yfl_ref[BLOCKS_PER_TILE * j, q, src, :]
                hi = yfl_ref[BLOCKS_PER_TILE * j + 1, q, src, :]
                y32_ref[j, pl.ds(b * T_SSM + 2 * q, N_CHUNKS, stride=M_CHUNK), :] = (
                    jnp.where(low_half, lo, swap_halves(hi)))
                y32_ref[j, pl.ds(b * T_SSM + 2 * q + 1, N_CHUNKS, stride=M_CHUNK), :] = (
                    jnp.where(low_half, swap_halves(lo), hi))
    y = jnp.concatenate([y32_ref[j] for j in range(N_LANE_TILES_U)], axis=-1)
    z = jax.nn.gelu(y).astype(_bf16)
    gate = jnp.dot(z, gluw_ref[...], preferred_element_type=_f32) + glub_ref[...]
    z = z * jax.nn.sigmoid(gate.astype(_bf16))
    ms = jnp.dot(z * z, ones_ref[...], preferred_element_type=_f32)
    out = z * lax.rsqrt(ms + EPS).astype(_bf16) * g_ref[...].astype(_bf16)
    y_ref[...] = out.reshape(BATCH, T_SSM, D_SSM)


def _ssm(u, bs_w, lam8, cs_w, tz_w, glu_w_b, glu_b, g_out_ssm, ones_ssm):
    rows = BATCH * T_SSM
    return pl.pallas_call(
        _ssm_kernel,
        grid=(SEQ // T_SSM,),
        in_specs=[
            pl.BlockSpec((BATCH, T_SSM, D_SSM), lambda i: (0, i, 0)),
            _const_spec((N_BLOCKS, BLOCK_CHUNK, BLOCK_COLS)),
            _const_spec((2, SUBLANES, N_STATE)),
            _const_spec((N_BLOCKS, BLOCK_COLS, BLOCK_CHUNK)),
            _const_spec((N_BLOCKS, BLOCK_CHUNK, BLOCK_CHUNK)),
            _const_spec((D_SSM, D_SSM)),
            _const_spec((1, D_SSM)),
            _const_spec((1, D_SSM)),
            _const_spec((D_SSM, D_SSM)),
        ],
        out_specs=pl.BlockSpec((BATCH, T_SSM, D_SSM), lambda i: (0, i, 0)),
        out_shape=jax.ShapeDtypeStruct((BATCH, SEQ, D_SSM), _bf16),
        scratch_shapes=[
            pltpu.VMEM((SUBLANES, N_BLOCKS * BLOCK_COLS), _f32),
            pltpu.VMEM((N_LANE_TILES_U, rows, LANES), _f32),
            pltpu.VMEM((N_BLOCKS, STEP_PAIRS, CHUNK_ROWS, LANES), _f32),
            pltpu.VMEM((CHUNK_ROWS, N_BLOCKS * BLOCK_COLS), _f32),
            pltpu.VMEM((N_BLOCKS, STEP_PAIRS, CHUNK_ROWS, LANES), _f32),
            pltpu.VMEM((N_LANE_TILES_U, rows, LANES), _f32),
        ],
        compiler_params=pltpu.CompilerParams(
            dimension_semantics=("arbitrary",), vmem_limit_bytes=VMEM_LIMIT),
        name="s5_mixer",
    )(u, bs_w, lam8, cs_w, tz_w, glu_w_b, glu_b, g_out_ssm, ones_ssm)


def _ffn_kernel(x_ref, ya_ref, yb_ref, mod_ref, wout_ref, gpm_ref, gpf_ref, wup_ref, fcw_ref, wdn_ref, gpo_ref,
                o_ref, halo_ref, act_ref, slab_ref):
    ti = pl.program_id(1)

    @pl.when(ti == 0)
    def _():
        halo_ref[...] = jnp.zeros(halo_ref.shape, _f32)

    x = x_ref[0]
    gt1 = mod_ref[0, 2:3, :]
    sh2 = mod_ref[0, 3:4, :]
    sc2 = mod_ref[0, 4:5, :]
    gt2 = mod_ref[0, 5:6, :]
    mix = jnp.dot(jnp.concatenate([ya_ref[0], yb_ref[0]], axis=-1), wout_ref[...], preferred_element_type=_f32)
    x1 = x + _rms(mix, gpm_ref[...] * gt1)
    h2 = (_rms(x1, gpf_ref[...] * (1.0 + sc2)) + sh2).astype(_bf16)

    def up_chunk(c):
        cols_a = slice(c * FF_CHUNK, (c + 1) * FF_CHUNK)
        cols_v = slice(D_FF + c * FF_CHUNK, D_FF + (c + 1) * FF_CHUNK)
        return (jnp.dot(h2, wup_ref[:, cols_a], preferred_element_type=_f32), cols_a,
                jnp.dot(h2, wup_ref[:, cols_v], preferred_element_type=_f32), cols_v)

    def conv(hc, cols, slab0):
        halo = halo_ref[:, cols]
        halo_ref[:, cols] = hc[T_FFN - SUBLANES:, :]
        return _causal_conv3(hc, halo, fcw_ref[:, cols], slab_ref, slab0)

    nxt = up_chunk(0)
    for c in range(N_FF_CHUNKS):
        ha, cols_a, hv, cols_v = nxt
        if c + 1 < N_FF_CHUNKS:
            nxt = up_chunk(c + 1)
        slab0 = (c % 2) * FFN_SLABS_PER_CHUNK
        a = conv(ha, cols_a, slab0)
        v = conv(hv, cols_v, slab0 + FF_CHUNK // LANES)
        act_ref[:, cols_a] = (a * jax.nn.sigmoid(a) * v).astype(_bf16)
    down = jnp.dot(act_ref[...], wdn_ref[...], preferred_element_type=_f32)
    o_ref[0] = x1 + _rms(down, gpo_ref[...] * gt2)


def _ffn(x, ya, yb, mod3, w_out_b, g_post_mix, g_pre_ffn, w_up_b, ffn_cw, w_down_b, g_post_ffn):
    n_t = SEQ // T_FFN
    return pl.pallas_call(
        _ffn_kernel,
        grid=(BATCH, n_t),
        in_specs=[
            pl.BlockSpec((1, T_FFN, D_MODEL), lambda b, t: (b, t, 0)),
            pl.BlockSpec((1, T_FFN, D_SSM), lambda b, t: (b, t, 0)),
            pl.BlockSpec((1, T_FFN, D_CONV), lambda b, t: (b, t, 0)),
            pl.BlockSpec((1, N_MOD, D_MODEL), lambda b, t: (b, 0, 0)),
            _const_spec((D_MODEL, D_MODEL)),
            _const_spec((1, D_MODEL)),
            _const_spec((1, D_MODEL)),
            _const_spec((D_MODEL, 2 * D_FF)),
            _const_spec((3, 2 * D_FF)),
            _const_spec((D_FF, D_MODEL)),
            _const_spec((1, D_MODEL)),
        ],
        out_specs=pl.BlockSpec((1, T_FFN, D_MODEL), lambda b, t: (b, t, 0)),
        out_shape=jax.ShapeDtypeStruct((BATCH, SEQ, D_MODEL), _f32),
        scratch_shapes=[
            pltpu.VMEM((SUBLANES, 2 * D_FF), _f32),
            pltpu.VMEM((T_FFN, D_FF), _bf16),
            pltpu.VMEM((2 * FFN_SLABS_PER_CHUNK, T_FFN + SUBLANES, LANES), _f32),
        ],
        compiler_params=pltpu.CompilerParams(
            dimension_semantics=("arbitrary", "arbitrary"), vmem_limit_bytes=VMEM_LIMIT),
        name="outproj_convffn",
    )(x, ya, yb, mod3, w_out_b, g_post_mix, g_pre_ffn, w_up_b, ffn_cw, w_down_b, g_post_ffn)


def _head_mean_matrix(width, head):
    idx = jnp.arange(width) // head
    return jnp.where(idx[:, None] == idx[None, :], 1.0 / head, 0.0).astype(_bf16)


def kernel(x, c, w_ada, b_ada, g_pre_mix, g_post_mix, w_in, ssm_lam_re, ssm_lam_im, ssm_log_step, ssm_b_re, ssm_b_im, ssm_c_re, ssm_c_im, ssm_d, glu_w, glu_b, g_out_ssm, conv_w, g_out_conv, w_out, g_pre_ffn, g_post_ffn, w_up, ffn_conv_w, w_down):
    assert x.shape == (BATCH, SEQ, D_MODEL) and w_ada.shape[0] == 1
    row = lambda a: a.reshape(1, -1)

    mod3 = _modulation(c, w_ada[0], b_ada[0]).reshape(BATCH, N_MOD, D_MODEL)

    lam8, bs_w, cs_w, tz_w = _ssm_prep(
        ssm_lam_re[0], ssm_lam_im[0], ssm_log_step[0], ssm_b_re[0], ssm_b_im[0], ssm_c_re[0], ssm_c_im[0],
        ssm_d[0])

    u, yb = _premix(x, mod3, row(g_pre_mix[0]), w_in[0].astype(_bf16), conv_w[0], row(g_out_conv[0]),
                    _head_mean_matrix(D_CONV, D_CONV // CONV_HEADS))

    ya = _ssm(u, bs_w, lam8, cs_w, tz_w, glu_w[0].astype(_bf16),
              row(glu_b[0]), row(g_out_ssm[0]), _head_mean_matrix(D_SSM, SSM_GROUP))

    return _ffn(x, ya, yb, mod3, w_out[0].astype(_bf16), row(g_post_mix[0]), row(g_pre_ffn[0]),
                w_up[0].astype(_bf16), ffn_conv_w[0], w_down[0].astype(_bf16), row(g_post_ffn[0]))
```

```python
import jax
import jax.numpy as jnp
from jax import lax
from jax.experimental import pallas as pl
from jax.experimental.pallas import tpu as pltpu

D_MODEL = 1024
BATCH = 8
SEQ = 4096
D_SSM = 512
D_CONV = 512
SSM_GROUP = 16
N_SSM_GROUPS = 32
SSM_STATE = 64
CONV_HEADS = 8
D_FF = 2816
N_MOD = 6
D_IN_PROJ = 2048
EPS = 1e-6
LAMBDA_RE_MAX = -1e-4

N_STATE = N_SSM_GROUPS * SSM_STATE
SUBLANES = 8
LANES = 128
GROUPS_PER_LANE_TILE = LANES // SSM_GROUP
N_LANE_TILES_U = D_SSM // LANES
STATE_PER_TILE = GROUPS_PER_LANE_TILE * SSM_STATE

T_PRE = 1024
T_SSM = 256
T_FFN = 1024
FF_CHUNK = 256
N_FF_CHUNKS = D_FF // FF_CHUNK
FFN_SLABS_PER_CHUNK = 2 * FF_CHUNK // LANES
M_CHUNK = 4
N_CHUNKS = T_SSM // M_CHUNK
CHUNK_ROWS = N_CHUNKS * BATCH
GROUPS_PER_BLOCK = GROUPS_PER_LANE_TILE // 2
BLOCKS_PER_TILE = GROUPS_PER_LANE_TILE // GROUPS_PER_BLOCK
N_BLOCKS = N_SSM_GROUPS // GROUPS_PER_BLOCK
BLOCK_LANES = GROUPS_PER_BLOCK * SSM_GROUP
BLOCK_STATE = GROUPS_PER_BLOCK * SSM_STATE
BLOCK_CHUNK = M_CHUNK * BLOCK_LANES
BLOCK_COLS = 2 * BLOCK_STATE
STEP_PAIRS = M_CHUNK // 2
VMEM_LIMIT = 60 * 1024 * 1024

_f32 = jnp.float32
_bf16 = jnp.bfloat16


def _const_spec(shape):
    nd = len(shape)
    return pl.BlockSpec(shape, lambda *_: (0,) * nd, pipeline_mode=pl.Buffered(1))


def _rms(x, g):
    ms = jnp.mean(x * x, axis=-1, keepdims=True)
    return x * lax.rsqrt(ms + EPS) * g


def _causal_conv3(h, halo, w, slab_ref, slab0):
    rows = h.shape[0]
    outs = []
    for k in range(h.shape[1] // LANES):
        lanes = slice(k * LANES, (k + 1) * LANES)
        slab_ref[slab0 + k, 0:SUBLANES, :] = halo[:, lanes]
        slab_ref[slab0 + k, SUBLANES:SUBLANES + rows, :] = h[:, lanes]
        h1 = slab_ref[slab0 + k, pl.ds(SUBLANES - 1, rows, stride=1), :]
        h2 = slab_ref[slab0 + k, pl.ds(SUBLANES - 2, rows, stride=1), :]
        outs.append(w[0:1, lanes] * h2 + w[1:2, lanes] * h1 + w[2:3, lanes] * h[:, lanes])
    return jnp.concatenate(outs, axis=-1)


def _mod_kernel(c_ref, w_ref, b_ref, o_ref):
    c = c_ref[...]
    c_act = c * jax.nn.sigmoid(c)
    o_ref[...] = jnp.dot(c_act, w_ref[...], preferred_element_type=_f32) + b_ref[...]


def _modulation(c, w_ada, b_ada):
    n_tile = 2 * D_MODEL
    return pl.pallas_call(
        _mod_kernel,
        grid=(N_MOD * D_MODEL // n_tile,),
        in_specs=[
            pl.BlockSpec((BATCH, D_MODEL), lambda j: (0, 0)),
            pl.BlockSpec((D_MODEL, n_tile), lambda j: (0, j)),
            pl.BlockSpec((1, n_tile), lambda j: (0, j)),
        ],
        out_specs=pl.BlockSpec((BATCH, n_tile), lambda j: (0, j)),
        out_shape=jax.ShapeDtypeStruct((BATCH, N_MOD * D_MODEL), _f32),
        name="adaln_mod",
    )(c, w_ada, b_ada.reshape(1, -1))


def _prep_kernel(lre_ref, lim_ref, lst_ref, bre_ref, bim_ref, cre_ref, cim_ref, d_ref,
                 lam8_ref, bs_ref, cs_ref, tz_ref):
    lre = jnp.minimum(lre_ref[0], LAMBDA_RE_MAX)
    lim = lim_ref[0]
    step = jnp.exp(lst_ref[0])
    log_mag, ang = lre * step, lim * step

    def lam_pow(k):
        mag = jnp.exp(k * log_mag)
        return mag * jnp.cos(k * ang), mag * jnp.sin(k * ang)

    pw = [lam_pow(float(k)) for k in range(M_CHUNK + 1)]

    def block_diag(blk):
        wide = jnp.concatenate([blk] * GROUPS_PER_BLOCK, axis=-1)
        r = lax.broadcasted_iota(jnp.int32, wide.shape, 0) // SSM_GROUP
        c = lax.broadcasted_iota(jnp.int32, wide.shape, 1) // SSM_STATE
        return jnp.where(r == c, wide, 0.0)

    a_re, a_im = pw[1]
    n_re = a_re - 1.0
    den = lre * lre + lim * lim
    q_re = (n_re * lre + a_im * lim) / den
    q_im = (a_im * lre - n_re * lim) / den
    b_re, b_im = block_diag(bre_ref[0]), block_diag(bim_ref[0])
    bb_re = q_re * b_re - q_im * b_im
    bb_im = q_re * b_im + q_im * b_re
    c_re, c_im = block_diag(cre_ref[0]), block_diag(cim_ref[0])

    lam8_ref[0] = jnp.broadcast_to(pw[M_CHUNK][0], (SUBLANES, BLOCK_STATE))
    lam8_ref[1] = jnp.broadcast_to(pw[M_CHUNK][1], (SUBLANES, BLOCK_STATE))
    for s in range(M_CHUNK):
        rows = slice(s * BLOCK_LANES, (s + 1) * BLOCK_LANES)
        p_re, p_im = pw[M_CHUNK - 1 - s]
        bs_ref[0, rows, 0:BLOCK_STATE] = (p_re * bb_re - p_im * bb_im).astype(_bf16)
        bs_ref[0, rows, BLOCK_STATE:] = (p_re * bb_im + p_im * bb_re).astype(_bf16)
        p_re, p_im = pw[s + 1]
        cs_ref[0, 0:BLOCK_STATE, rows] = (p_re * c_re - p_im * c_im).T.astype(_bf16)
        cs_ref[0, BLOCK_STATE:, rows] = (-(p_re * c_im + p_im * c_re)).T.astype(_bf16)
    tz_ref[...] = jnp.zeros(tz_ref.shape, _bf16)
    contract_p = (((1,), (1,)), ((), ()))
    for tau in range(M_CHUNK):
        p_re, p_im = pw[tau]
        ct_re = p_re * c_re - p_im * c_im
        ct_im = p_re * c_im + p_im * c_re
        k_tau = (lax.dot_general(bb_re, ct_re, contract_p, precision=lax.Precision.HIGHEST,
                                 preferred_element_type=_f32)
                 - lax.dot_general(bb_im, ct_im, contract_p, precision=lax.Precision.HIGHEST,
                                   preferred_element_type=_f32))
        if tau == 0:
            r = lax.broadcasted_iota(jnp.int32, k_tau.shape, 0)
            c = lax.broadcasted_iota(jnp.int32, k_tau.shape, 1)
            k_tau = k_tau + jnp.where(r == c, d_ref[0], 0.0)
        for s0 in range(M_CHUNK - tau):
            tz_ref[0, s0 * BLOCK_LANES:(s0 + 1) * BLOCK_LANES,
                   (s0 + tau) * BLOCK_LANES:(s0 + tau + 1) * BLOCK_LANES] = k_tau.astype(_bf16)


def _ssm_prep(lam_re, lam_im, log_step, b_re, b_im, c_re, c_im, d_skip):
    nb = N_BLOCKS
    lstep = jnp.broadcast_to(log_step[:, None], (N_SSM_GROUPS, SSM_STATE))
    as_row = lambda a: a.reshape(nb, 1, BLOCK_STATE)
    b_t = lambda a: a.transpose(0, 2, 1).reshape(nb, BLOCK_LANES, SSM_STATE)
    c_t = lambda a: a.reshape(nb, BLOCK_LANES, SSM_STATE)
    tile_spec = lambda r, c: pl.BlockSpec((1, r, c), lambda j: (j, 0, 0))
    return pl.pallas_call(
        _prep_kernel,
        grid=(nb,),
        in_specs=[tile_spec(1, BLOCK_STATE)] * 3 + [tile_spec(BLOCK_LANES, SSM_STATE)] * 4
        + [tile_spec(1, BLOCK_LANES)],
        out_specs=[
            pl.BlockSpec((2, SUBLANES, BLOCK_STATE), lambda j: (0, 0, j)),
            tile_spec(BLOCK_CHUNK, BLOCK_COLS),
            tile_spec(BLOCK_COLS, BLOCK_CHUNK),
            tile_spec(BLOCK_CHUNK, BLOCK_CHUNK),
        ],
        out_shape=(
            jax.ShapeDtypeStruct((2, SUBLANES, N_STATE), _f32),
            jax.ShapeDtypeStruct((nb, BLOCK_CHUNK, BLOCK_COLS), _bf16),
            jax.ShapeDtypeStruct((nb, BLOCK_COLS, BLOCK_CHUNK), _bf16),
            jax.ShapeDtypeStruct((nb, BLOCK_CHUNK, BLOCK_CHUNK), _bf16),
        ),
        name="s5_discretise",
    )(as_row(lam_re), as_row(lam_im), as_row(lstep), b_t(b_re), b_t(b_im), c_t(c_re), c_t(c_im),
      d_skip.reshape(nb, 1, BLOCK_LANES))


def _premix_kernel(x_ref, mod_ref, g_ref, win32_ref, cw_ref, gconv_ref, ones_ref,
                   u_ref, yb_ref, halo_ref, slab_ref, win_ref):
    ti = pl.program_id(1)

    @pl.when(jnp.logical_and(pl.program_id(0) == 0, ti == 0))
    def _():
        for k in range(D_IN_PROJ // D_CONV):
            cols = slice(k * D_CONV, (k + 1) * D_CONV)
            win_ref[:, cols] = win32_ref[:, cols].astype(_bf16)

    @pl.when(ti == 0)
    def _():
        halo_ref[...] = jnp.zeros(halo_ref.shape, _f32)

    x = x_ref[0]
    sh = mod_ref[0, 0:1, :]
    sc = mod_ref[0, 1:2, :]
    h = _rms(x, g_ref[...] * (1.0 + sc)) + sh
    hb = h.astype(_bf16)
    proj = jnp.dot(hb, win_ref[:, D_SSM:], preferred_element_type=_f32)
    u_ref[0] = jnp.dot(hb, win_ref[:, 0:D_SSM], preferred_element_type=_f32).astype(_bf16)
    bg = proj[:, 0:D_CONV]
    cg = proj[:, D_CONV:2 * D_CONV]
    v = proj[:, 2 * D_CONV:]
    cv = cg * v
    halo = halo_ref[...]
    halo_ref[...] = cv[T_PRE - SUBLANES:, :]
    yb = bg * _causal_conv3(cv, halo, cw_ref, slab_ref, 0)
    ms = jnp.dot((yb * yb).astype(_bf16), ones_ref[...], preferred_element_type=_f32)
    yb_ref[0] = (yb * lax.rsqrt(ms + EPS) * gconv_ref[...]).astype(_bf16)


def _premix(x, mod3, g_pre, w_in, conv_w, g_out_conv, ones_conv):
    n_t = SEQ // T_PRE
    return pl.pallas_call(
        _premix_kernel,
        grid=(BATCH, n_t),
        in_specs=[
            pl.BlockSpec((1, T_PRE, D_MODEL), lambda b, t: (b, t, 0)),
            pl.BlockSpec((1, N_MOD, D_MODEL), lambda b, t: (b, 0, 0)),
            _const_spec((1, D_MODEL)),
            _const_spec((D_MODEL, D_IN_PROJ)),
            _const_spec((3, D_CONV)),
            _const_spec((1, D_CONV)),
            _const_spec((D_CONV, D_CONV)),
        ],
        out_specs=[
            pl.BlockSpec((1, T_PRE, D_SSM), lambda b, t: (b, t, 0)),
            pl.BlockSpec((1, T_PRE, D_CONV), lambda b, t: (b, t, 0)),
        ],
        out_shape=[
            jax.ShapeDtypeStruct((BATCH, SEQ, D_SSM), _bf16),
            jax.ShapeDtypeStruct((BATCH, SEQ, D_CONV), _bf16),
        ],
        scratch_shapes=[
            pltpu.VMEM((SUBLANES, D_CONV), _f32),
            pltpu.VMEM((D_CONV // LANES, T_PRE + SUBLANES, LANES), _f32),
            pltpu.VMEM((D_MODEL, D_IN_PROJ), _bf16),
        ],
        compiler_params=pltpu.CompilerParams(
            dimension_semantics=("arbitrary", "arbitrary"), vmem_limit_bytes=VMEM_LIMIT),
        name="premix",
    )(x, mod3, g_pre, w_in, conv_w, g_out_conv, ones_conv)


def _ssm_kernel(u_ref, bs_ref, lam_ref, cs_ref, tz_ref, gluw_ref, glub_ref, g_ref, ones_ref,
                y_ref, st_ref, u32_ref, ufl_ref, s_ref, yfl_ref, y32_ref):
    i = pl.program_id(0)

    @pl.when(i == 0)
    def _():
        st_ref[...] = jnp.zeros(st_ref.shape, _f32)

    low_half = lax.broadcasted_iota(jnp.int32, (N_CHUNKS, LANES), 1) < BLOCK_LANES
    swap_halves = lambda v: pltpu.roll(v, BLOCK_LANES, axis=1)

    u = u_ref[...].reshape(BATCH * T_SSM, D_SSM)
    for j in range(N_LANE_TILES_U):
        u32_ref[j] = u[:, j * LANES:(j + 1) * LANES].astype(_f32)
    for j in range(N_LANE_TILES_U):
        for q in range(STEP_PAIRS):
            for b in range(BATCH):
                v0 = u32_ref[j, pl.ds(b * T_SSM + 2 * q, N_CHUNKS, stride=M_CHUNK), :]
                v1 = u32_ref[j, pl.ds(b * T_SSM + 2 * q + 1, N_CHUNKS, stride=M_CHUNK), :]
                dst = pl.ds(b, N_CHUNKS, stride=BATCH)
                ufl_ref[BLOCKS_PER_TILE * j, q, dst, :] = jnp.where(low_half, v0, swap_halves(v1))
                ufl_ref[BLOCKS_PER_TILE * j + 1, q, dst, :] = jnp.where(low_half, swap_halves(v0), v1)

    def chunk_lhs(blk):
        return jnp.concatenate([ufl_ref[blk, q] for q in range(STEP_PAIRS)], axis=-1).astype(_bf16)

    def bproj(j):
        for blk in range(BLOCKS_PER_TILE * j, BLOCKS_PER_TILE * (j + 1)):
            s_ref[:, blk * BLOCK_COLS:(blk + 1) * BLOCK_COLS] = jnp.dot(
                chunk_lhs(blk), bs_ref[blk], preferred_element_type=_f32)

    def scan(j):
        re, lam = [], []
        for blk in range(BLOCKS_PER_TILE * j, BLOCKS_PER_TILE * (j + 1)):
            for k in range(BLOCK_STATE // LANES):
                re.append(slice(blk * BLOCK_COLS + k * LANES, blk * BLOCK_COLS + (k + 1) * LANES))
                lam.append(slice(blk * BLOCK_STATE + k * LANES, blk * BLOCK_STATE + (k + 1) * LANES))
        im = [slice(r.start + BLOCK_STATE, r.stop + BLOCK_STATE) for r in re]
        n = len(re)
        lre = [lam_ref[0, :, ln] for ln in lam]
        lim = [lam_ref[1, :, ln] for ln in lam]
        xre = [st_ref[:, r] for r in re]
        xim = [st_ref[:, r] for r in im]
        for c in range(N_CHUNKS):
            rows = slice(c * BATCH, (c + 1) * BATCH)
            for k in range(n):
                loc_re, loc_im = s_ref[rows, re[k]], s_ref[rows, im[k]]
                s_ref[rows, re[k]] = xre[k]
                s_ref[rows, im[k]] = xim[k]
                xre[k], xim[k] = (lre[k] * xre[k] - lim[k] * xim[k] + loc_re,
                                  lre[k] * xim[k] + lim[k] * xre[k] + loc_im)
        for k in range(n):
            st_ref[:, re[k]] = xre[k]
            st_ref[:, im[k]] = xim[k]

    def cproj(j):
        for blk in range(BLOCKS_PER_TILE * j, BLOCKS_PER_TILE * (j + 1)):
            x_in = s_ref[:, blk * BLOCK_COLS:(blk + 1) * BLOCK_COLS].astype(_bf16)
            yf = jnp.dot(x_in, cs_ref[blk], preferred_element_type=_f32)
            yf = yf + jnp.dot(chunk_lhs(blk), tz_ref[blk], preferred_element_type=_f32)
            for q in range(STEP_PAIRS):
                yfl_ref[blk, q] = yf[:, q * LANES:(q + 1) * LANES]

    bproj(0)
    for j in range(N_LANE_TILES_U):
        if j + 1 < N_LANE_TILES_U:
            bproj(j + 1)
        if j > 0:
            cproj(j - 1)
        scan(j)
    cproj(N_LANE_TILES_U - 1)

    for j in range(N_LANE_TILES_U):
        for q in range(STEP_PAIRS):
            for b in range(BATCH):
                src = pl.ds(b, N_CHUNKS, stride=BATCH)
                lo = yfl_ref[BLOCKS_PER_TILE * j, q, src, :]
                hi = yfl_ref[BLOCKS_PER_TILE * j + 1, q, src, :]
                y32_ref[j, pl.ds(b * T_SSM + 2 * q, N_CHUNKS, stride=M_CHUNK), :] = (
                    jnp.where(low_half, lo, swap_halves(hi)))
                y32_ref[j, pl.ds(b * T_SSM + 2 * q + 1, N_CHUNKS, stride=M_CHUNK), :] = (
                    jnp.where(low_half, swap_halves(lo), hi))
    y = jnp.concatenate([y32_ref[j] for j in range(N_LANE_TILES_U)], axis=-1)
    z = jax.nn.gelu(y).astype(_bf16)
    gate = jnp.dot(z, gluw_ref[...], preferred_element_type=_f32) + glub_ref[...]
    z = z * jax.nn.sigmoid(gate.astype(_bf16))
    ms = jnp.dot(z * z, ones_ref[...], preferred_element_type=_f32)
    out = z * lax.rsqrt(ms + EPS).astype(_bf16) * g_ref[...].astype(_bf16)
    y_ref[...] = out.reshape(BATCH, T_SSM, D_SSM)


def _ssm(u, bs_w, lam8, cs_w, tz_w, glu_w_b, glu_b, g_out_ssm, ones_ssm):
    rows = BATCH * T_SSM
    return pl.pallas_call(
        _ssm_kernel,
        grid=(SEQ // T_SSM,),
        in_specs=[
            pl.BlockSpec((BATCH, T_SSM, D_SSM), lambda i: (0, i, 0)),
            _const_spec((N_BLOCKS, BLOCK_CHUNK, BLOCK_COLS)),
            _const_spec((2, SUBLANES, N_STATE)),
            _const_spec((N_BLOCKS, BLOCK_COLS, BLOCK_CHUNK)),
            _const_spec((N_BLOCKS, BLOCK_CHUNK, BLOCK_CHUNK)),
            _const_spec((D_SSM, D_SSM)),
            _const_spec((1, D_SSM)),
            _const_spec((1, D_SSM)),
            _const_spec((D_SSM, D_SSM)),
        ],
        out_specs=pl.BlockSpec((BATCH, T_SSM, D_SSM), lambda i: (0, i, 0)),
        out_shape=jax.ShapeDtypeStruct((BATCH, SEQ, D_SSM), _bf16),
        scratch_shapes=[
            pltpu.VMEM((SUBLANES, N_BLOCKS * BLOCK_COLS), _f32),
            pltpu.VMEM((N_LANE_TILES_U, rows, LANES), _f32),
            pltpu.VMEM((N_BLOCKS, STEP_PAIRS, CHUNK_ROWS, LANES), _f32),
            pltpu.VMEM((CHUNK_ROWS, N_BLOCKS * BLOCK_COLS), _f32),
            pltpu.VMEM((N_BLOCKS, STEP_PAIRS, CHUNK_ROWS, LANES), _f32),
            pltpu.VMEM((N_LANE_TILES_U, rows, LANES), _f32),
        ],
        compiler_params=pltpu.CompilerParams(
            dimension_semantics=("arbitrary",), vmem_limit_bytes=VMEM_LIMIT),
        name="s5_mixer",
    )(u, bs_w, lam8, cs_w, tz_w, glu_w_b, glu_b, g_out_ssm, ones_ssm)


def _ffn_kernel(x_ref, ya_ref, yb_ref, mod_ref, wout_ref, gpm_ref, gpf_ref, wup_ref, fcw_ref, wdn_ref, gpo_ref,
                o_ref, halo_ref, act_ref, slab_ref):
    ti = pl.program_id(1)

    @pl.when(ti == 0)
    def _():
        halo_ref[...] = jnp.zeros(halo_ref.shape, _f32)

    x = x_ref[0]
    gt1 = mod_ref[0, 2:3, :]
    sh2 = mod_ref[0, 3:4, :]
    sc2 = mod_ref[0, 4:5, :]
    gt2 = mod_ref[0, 5:6, :]
    mix = jnp.dot(jnp.concatenate([ya_ref[0], yb_ref[0]], axis=-1), wout_ref[...], preferred_element_type=_f32)
    x1 = x + _rms(mix, gpm_ref[...] * gt1)
    h2 = (_rms(x1, gpf_ref[...] * (1.0 + sc2)) + sh2).astype(_bf16)

    def up_chunk(c):
        cols_a = slice(c * FF_CHUNK, (c + 1) * FF_CHUNK)
        cols_v = slice(D_FF + c * FF_CHUNK, D_FF + (c + 1) * FF_CHUNK)
        return (jnp.dot(h2, wup_ref[:, cols_a], preferred_element_type=_f32), cols_a,
                jnp.dot(h2, wup_ref[:, cols_v], preferred_element_type=_f32), cols_v)

    def conv(hc, cols, slab0):
        halo = halo_ref[:, cols]
        halo_ref[:, cols] = hc[T_FFN - SUBLANES:, :]
        return _causal_conv3(hc, halo, fcw_ref[:, cols], slab_ref, slab0)

    nxt = up_chunk(0)
    for c in range(N_FF_CHUNKS):
        ha, cols_a, hv, cols_v = nxt
        if c + 1 < N_FF_CHUNKS:
            nxt = up_chunk(c + 1)
        slab0 = (c % 2) * FFN_SLABS_PER_CHUNK
        a = conv(ha, cols_a, slab0)
        v = conv(hv, cols_v, slab0 + FF_CHUNK // LANES)
        act_ref[:, cols_a] = (a * jax.nn.sigmoid(a) * v).astype(_bf16)
    down = jnp.dot(act_ref[...], wdn_ref[...], preferred_element_type=_f32)
    o_ref[0] = x1 + _rms(down, gpo_ref[...] * gt2)


def _ffn(x, ya, yb, mod3, w_out_b, g_post_mix, g_pre_ffn, w_up_b, ffn_cw, w_down_b, g_post_ffn):
    n_t = SEQ // T_FFN
    return pl.pallas_call(
        _ffn_kernel,
        grid=(BATCH, n_t),
        in_specs=[
            pl.BlockSpec((1, T_FFN, D_MODEL), lambda b, t: (b, t, 0)),
            pl.BlockSpec((1, T_FFN, D_SSM), lambda b, t: (b, t, 0)),
            pl.BlockSpec((1, T_FFN, D_CONV), lambda b, t: (b, t, 0)),
            pl.BlockSpec((1, N_MOD, D_MODEL), lambda b, t: (b, 0, 0)),
            _const_spec((D_MODEL, D_MODEL)),
            _const_spec((1, D_MODEL)),
            _const_spec((1, D_MODEL)),
            _const_spec((D_MODEL, 2 * D_FF)),
            _const_spec((3, 2 * D_FF)),
            _const_spec((D_FF, D_MODEL)),
            _const_spec((1, D_MODEL)),
        ],
        out_specs=pl.BlockSpec((1, T_FFN, D_MODEL), lambda b, t: (b, t, 0)),
        out_shape=jax.ShapeDtypeStruct((BATCH, SEQ, D_MODEL), _f32),
        scratch_shapes=[
            pltpu.VMEM((SUBLANES, 2 * D_FF), _f32),
            pltpu.VMEM((T_FFN, D_FF), _bf16),
            pltpu.VMEM((2 * FFN_SLABS_PER_CHUNK, T_FFN + SUBLANES, LANES), _f32),
        ],
        compiler_params=pltpu.CompilerParams(
            dimension_semantics=("arbitrary", "arbitrary"), vmem_limit_bytes=VMEM_LIMIT),
        name="outproj_convffn",
    )(x, ya, yb, mod3, w_out_b, g_post_mix, g_pre_ffn, w_up_b, ffn_cw, w_down_b, g_post_ffn)


def _head_mean_matrix(width, head):
    idx = jnp.arange(width) // head
    return jnp.where(idx[:, None] == idx[None, :], 1.0 / head, 0.0).astype(_bf16)


def kernel(x, c, w_ada, b_ada, g_pre_mix, g_post_mix, w_in, ssm_lam_re, ssm_lam_im, ssm_log_step, ssm_b_re, ssm_b_im, ssm_c_re, ssm_c_im, ssm_d, glu_w, glu_b, g_out_ssm, conv_w, g_out_conv, w_out, g_pre_ffn, g_post_ffn, w_up, ffn_conv_w, w_down):
    assert x.shape == (BATCH, SEQ, D_MODEL) and w_ada.shape[0] == 1
    row = lambda a: a.reshape(1, -1)

    mod3 = _modulation(c, w_ada[0], b_ada[0]).reshape(BATCH, N_MOD, D_MODEL)

    lam8, bs_w, cs_w, tz_w = _ssm_prep(
        ssm_lam_re[0], ssm_lam_im[0], ssm_log_step[0], ssm_b_re[0], ssm_b_im[0], ssm_c_re[0], ssm_c_im[0],
        ssm_d[0])

    u, yb = _premix(x, mod3, row(g_pre_mix[0]), w_in[0], conv_w[0], row(g_out_conv[0]),
                    _head_mean_matrix(D_CONV, D_CONV // CONV_HEADS))

    ya = _ssm(u, bs_w, lam8, cs_w, tz_w, glu_w[0].astype(_bf16),
              row(glu_b[0]), row(g_out_ssm[0]), _head_mean_matrix(D_SSM, SSM_GROUP))

    return _ffn(x, ya, yb, mod3, w_out[0].astype(_bf16), row(g_post_mix[0]), row(g_pre_ffn[0]),
                w_up[0].astype(_bf16), ffn_conv_w[0], w_down[0].astype(_bf16), row(g_post_ffn[0]))
```

```python
import math

import jax
import jax.numpy as jnp
from jax import lax
from jax.experimental import pallas as pl
from jax.experimental.pallas import tpu as pltpu

D_MODEL = 1024
BATCH = 8
SEQ = 4096
D_SSM = 512
D_CONV = 512
SSM_GROUP = 16
N_SSM_GROUPS = 32
SSM_STATE = 64
CONV_HEADS = 8
D_FF = 2816
N_MOD = 6
D_IN_PROJ = 2048
EPS = 1e-6
LAMBDA_RE_MAX = -1e-4

N_STATE = N_SSM_GROUPS * SSM_STATE
SUBLANES = 8
LANES = 128
GROUPS_PER_LANE_TILE = LANES // SSM_GROUP
N_LANE_TILES_U = D_SSM // LANES

T_PRE = 1024
T_SSM = 256
T_FFN = 1024
FF_CHUNK = 256
N_FF_CHUNKS = D_FF // FF_CHUNK
FFN_SLABS_PER_CHUNK = 2 * FF_CHUNK // LANES
M_CHUNK = 4
N_CHUNKS = T_SSM // M_CHUNK
CHUNK_ROWS = N_CHUNKS * BATCH
GROUPS_PER_BLOCK = GROUPS_PER_LANE_TILE // 2
BLOCKS_PER_TILE = GROUPS_PER_LANE_TILE // GROUPS_PER_BLOCK
N_BLOCKS = N_SSM_GROUPS // GROUPS_PER_BLOCK
BLOCK_LANES = GROUPS_PER_BLOCK * SSM_GROUP
BLOCK_STATE = GROUPS_PER_BLOCK * SSM_STATE
BLOCK_CHUNK = M_CHUNK * BLOCK_LANES
BLOCK_COLS = 2 * BLOCK_STATE
STEP_PAIRS = M_CHUNK // 2
VMEM_LIMIT = 60 * 1024 * 1024

_f32 = jnp.float32
_bf16 = jnp.bfloat16


def _const_spec(shape):
    nd = len(shape)
    return pl.BlockSpec(shape, lambda *_: (0,) * nd, pipeline_mode=pl.Buffered(1))


def _rms(x, g):
    ms = jnp.mean(x * x, axis=-1, keepdims=True)
    return x * lax.rsqrt(ms + EPS) * g


def _gelu_tanh(y):
    k = math.sqrt(2.0 / math.pi)
    half_y = 0.5 * y
    return half_y + half_y * jnp.tanh(y * (k + (k * 0.044715) * (y * y)))


def _causal_conv3(h, halo, w, slab_ref, slab0):
    rows = h.shape[0]
    outs = []
    for k in range(h.shape[1] // LANES):
        lanes = slice(k * LANES, (k + 1) * LANES)
        slab_ref[slab0 + k, 0:SUBLANES, :] = halo[:, lanes]
        slab_ref[slab0 + k, SUBLANES:SUBLANES + rows, :] = h[:, lanes]
        h1 = slab_ref[slab0 + k, pl.ds(SUBLANES - 1, rows, stride=1), :]
        h2 = slab_ref[slab0 + k, pl.ds(SUBLANES - 2, rows, stride=1), :]
        outs.append(w[0:1, lanes] * h2 + w[1:2, lanes] * h1 + w[2:3, lanes] * h[:, lanes])
    return jnp.concatenate(outs, axis=-1)


def _mod_kernel(c_ref, w_ref, b_ref, o_ref):
    c = c_ref[...]
    c_act = c * jax.nn.sigmoid(c)
    o_ref[...] = jnp.dot(c_act, w_ref[...], preferred_element_type=_f32) + b_ref[...]


def _modulation(c, w_ada, b_ada):
    n_tile = 2 * D_MODEL
    return pl.pallas_call(
        _mod_kernel,
        grid=(N_MOD * D_MODEL // n_tile,),
        in_specs=[
            pl.BlockSpec((BATCH, D_MODEL), lambda j: (0, 0)),
            pl.BlockSpec((D_MODEL, n_tile), lambda j: (0, j)),
            pl.BlockSpec((1, n_tile), lambda j: (0, j)),
        ],
        out_specs=pl.BlockSpec((BATCH, n_tile), lambda j: (0, j)),
        out_shape=jax.ShapeDtypeStruct((BATCH, N_MOD * D_MODEL), _f32),
        name="adaln_mod",
    )(c, w_ada, b_ada.reshape(1, -1))


def _prep_kernel(lre_ref, lim_ref, lst_ref, bre_ref, bim_ref, cre_ref, cim_ref, d_ref,
                 lam8_ref, bs_ref, cs_ref, tz_ref):
    lre = jnp.minimum(lre_ref[0], LAMBDA_RE_MAX)
    lim = lim_ref[0]
    step = jnp.exp(lst_ref[0])
    log_mag, ang = lre * step, lim * step

    def lam_pow(k):
        mag = jnp.exp(k * log_mag)
        return mag * jnp.cos(k * ang), mag * jnp.sin(k * ang)

    pw = [lam_pow(float(k)) for k in range(M_CHUNK + 1)]

    def block_diag(blk):
        wide = jnp.concatenate([blk] * GROUPS_PER_BLOCK, axis=-1)
        r = lax.broadcasted_iota(jnp.int32, wide.shape, 0) // SSM_GROUP
        c = lax.broadcasted_iota(jnp.int32, wide.shape, 1) // SSM_STATE
        return jnp.where(r == c, wide, 0.0)

    a_re, a_im = pw[1]
    n_re = a_re - 1.0
    den = lre * lre + lim * lim
    q_re = (n_re * lre + a_im * lim) / den
    q_im = (a_im * lre - n_re * lim) / den
    b_re, b_im = block_diag(bre_ref[0]), block_diag(bim_ref[0])
    bb_re = q_re * b_re - q_im * b_im
    bb_im = q_re * b_im + q_im * b_re
    c_re, c_im = block_diag(cre_ref[0]), block_diag(cim_ref[0])

    lam8_ref[0] = jnp.broadcast_to(pw[M_CHUNK][0], (SUBLANES, BLOCK_STATE))
    lam8_ref[1] = jnp.broadcast_to(pw[M_CHUNK][1], (SUBLANES, BLOCK_STATE))
    for s in range(M_CHUNK):
        rows = slice(s * BLOCK_LANES, (s + 1) * BLOCK_LANES)
        p_re, p_im = pw[M_CHUNK - 1 - s]
        bs_ref[0, rows, 0:BLOCK_STATE] = (p_re * bb_re - p_im * bb_im).astype(_bf16)
        bs_ref[0, rows, BLOCK_STATE:] = (p_re * bb_im + p_im * bb_re).astype(_bf16)
        p_re, p_im = pw[s + 1]
        cs_ref[0, 0:BLOCK_STATE, rows] = (p_re * c_re - p_im * c_im).T.astype(_bf16)
        cs_ref[0, BLOCK_STATE:, rows] = (-(p_re * c_im + p_im * c_re)).T.astype(_bf16)
    tz_ref[...] = jnp.zeros(tz_ref.shape, _bf16)
    contract_p = (((1,), (1,)), ((), ()))
    for tau in range(M_CHUNK):
        p_re, p_im = pw[tau]
        ct_re = p_re * c_re - p_im * c_im
        ct_im = p_re * c_im + p_im * c_re
        k_tau = (lax.dot_general(bb_re, ct_re, contract_p, precision=lax.Precision.HIGHEST,
                                 preferred_element_type=_f32)
                 - lax.dot_general(bb_im, ct_im, contract_p, precision=lax.Precision.HIGHEST,
                                   preferred_element_type=_f32))
        if tau == 0:
            r = lax.broadcasted_iota(jnp.int32, k_tau.shape, 0)
            c = lax.broadcasted_iota(jnp.int32, k_tau.shape, 1)
            k_tau = k_tau + jnp.where(r == c, d_ref[0], 0.0)
        for s0 in range(M_CHUNK - tau):
            tz_ref[0, s0 * BLOCK_LANES:(s0 + 1) * BLOCK_LANES,
                   (s0 + tau) * BLOCK_LANES:(s0 + tau + 1) * BLOCK_LANES] = k_tau.astype(_bf16)


def _ssm_prep(lam_re, lam_im, log_step, b_re, b_im, c_re, c_im, d_skip):
    nb = N_BLOCKS
    lstep = jnp.broadcast_to(log_step[:, None], (N_SSM_GROUPS, SSM_STATE))
    as_row = lambda a: a.reshape(nb, 1, BLOCK_STATE)
    b_t = lambda a: a.transpose(0, 2, 1).reshape(nb, BLOCK_LANES, SSM_STATE)
    c_t = lambda a: a.reshape(nb, BLOCK_LANES, SSM_STATE)
    tile_spec = lambda r, c: pl.BlockSpec((1, r, c), lambda j: (j, 0, 0))
    return pl.pallas_call(
        _prep_kernel,
        grid=(nb,),
        in_specs=[tile_spec(1, BLOCK_STATE)] * 3 + [tile_spec(BLOCK_LANES, SSM_STATE)] * 4
        + [tile_spec(1, BLOCK_LANES)],
        out_specs=[
            pl.BlockSpec((2, SUBLANES, BLOCK_STATE), lambda j: (0, 0, j)),
            tile_spec(BLOCK_CHUNK, BLOCK_COLS),
            tile_spec(BLOCK_COLS, BLOCK_CHUNK),
            tile_spec(BLOCK_CHUNK, BLOCK_CHUNK),
        ],
        out_shape=(
            jax.ShapeDtypeStruct((2, SUBLANES, N_STATE), _f32),
            jax.ShapeDtypeStruct((nb, BLOCK_CHUNK, BLOCK_COLS), _bf16),
            jax.ShapeDtypeStruct((nb, BLOCK_COLS, BLOCK_CHUNK), _bf16),
            jax.ShapeDtypeStruct((nb, BLOCK_CHUNK, BLOCK_CHUNK), _bf16),
        ),
        name="s5_discretise",
    )(as_row(lam_re), as_row(lam_im), as_row(lstep), b_t(b_re), b_t(b_im), c_t(c_re), c_t(c_im),
      d_skip.reshape(nb, 1, BLOCK_LANES))


def _premix_kernel(x_ref, mod_ref, g_ref, win32_ref, cw_ref, gconv_ref, ones_ref,
                   u_ref, yb_ref, halo_ref, slab_ref, win_ref):
    ti = pl.program_id(1)

    @pl.when(jnp.logical_and(pl.program_id(0) == 0, ti == 0))
    def _():
        for k in range(D_IN_PROJ // D_CONV):
            cols = slice(k * D_CONV, (k + 1) * D_CONV)
            win_ref[:, cols] = win32_ref[:, cols].astype(_bf16)

    @pl.when(ti == 0)
    def _():
        halo_ref[...] = jnp.zeros(halo_ref.shape, _f32)

    x = x_ref[0]
    sh = mod_ref[0, 0:1, :]
    sc = mod_ref[0, 1:2, :]
    h = _rms(x, g_ref[...] * (1.0 + sc)) + sh
    hb = h.astype(_bf16)
    proj = jnp.dot(hb, win_ref[:, D_SSM:], preferred_element_type=_f32)
    u_ref[0] = jnp.dot(hb, win_ref[:, 0:D_SSM], preferred_element_type=_f32).astype(_bf16)
    bg = proj[:, 0:D_CONV]
    cg = proj[:, D_CONV:2 * D_CONV]
    v = proj[:, 2 * D_CONV:]
    cv = cg * v
    halo = halo_ref[...]
    halo_ref[...] = cv[T_PRE - SUBLANES:, :]
    yb = bg * _causal_conv3(cv, halo, cw_ref, slab_ref, 0)
    ms = jnp.dot((yb * yb).astype(_bf16), ones_ref[...], preferred_element_type=_f32)
    yb_ref[0] = (yb * lax.rsqrt(ms + EPS) * gconv_ref[...]).astype(_bf16)


def _premix(x, mod3, g_pre, w_in, conv_w, g_out_conv, ones_conv):
    n_t = SEQ // T_PRE
    return pl.pallas_call(
        _premix_kernel,
        grid=(BATCH, n_t),
        in_specs=[
            pl.BlockSpec((1, T_PRE, D_MODEL), lambda b, t: (b, t, 0)),
            pl.BlockSpec((1, N_MOD, D_MODEL), lambda b, t: (b, 0, 0)),
            _const_spec((1, D_MODEL)),
            _const_spec((D_MODEL, D_IN_PROJ)),
            _const_spec((3, D_CONV)),
            _const_spec((1, D_CONV)),
            _const_spec((D_CONV, D_CONV)),
        ],
        out_specs=[
            pl.BlockSpec((1, T_PRE, D_SSM), lambda b, t: (b, t, 0)),
            pl.BlockSpec((1, T_PRE, D_CONV), lambda b, t: (b, t, 0)),
        ],
        out_shape=[
            jax.ShapeDtypeStruct((BATCH, SEQ, D_SSM), _bf16),
            jax.ShapeDtypeStruct((BATCH, SEQ, D_CONV), _bf16),
        ],
        scratch_shapes=[
            pltpu.VMEM((SUBLANES, D_CONV), _f32),
            pltpu.VMEM((D_CONV // LANES, T_PRE + SUBLANES, LANES), _f32),
            pltpu.VMEM((D_MODEL, D_IN_PROJ), _bf16),
        ],
        compiler_params=pltpu.CompilerParams(
            dimension_semantics=("arbitrary", "arbitrary"), vmem_limit_bytes=VMEM_LIMIT),
        name="premix",
    )(x, mod3, g_pre, w_in, conv_w, g_out_conv, ones_conv)


def _ssm_kernel(u_ref, bs_ref, lam_ref, cs_ref, tz_ref, gluw_ref, glub_ref, g_ref, ones_ref,
                y_ref, st_ref, u32_ref, ufl_ref, s_ref, yfl_ref, y32_ref):
    i = pl.program_id(0)

    @pl.when(i == 0)
    def _():
        st_ref[...] = jnp.zeros(st_ref.shape, _f32)

    low_half = lax.broadcasted_iota(jnp.int32, (N_CHUNKS, LANES), 1) < BLOCK_LANES
    swap_halves = lambda v: pltpu.roll(v, BLOCK_LANES, axis=1)

    u = u_ref[...].reshape(BATCH * T_SSM, D_SSM)
    for j in range(N_LANE_TILES_U):
        u32_ref[j] = u[:, j * LANES:(j + 1) * LANES].astype(_f32)
    for j in range(N_LANE_TILES_U):
        for q in range(STEP_PAIRS):
            for b in range(BATCH):
                v0 = u32_ref[j, pl.ds(b * T_SSM + 2 * q, N_CHUNKS, stride=M_CHUNK), :]
                v1 = u32_ref[j, pl.ds(b * T_SSM + 2 * q + 1, N_CHUNKS, stride=M_CHUNK), :]
                dst = pl.ds(b, N_CHUNKS, stride=BATCH)
                ufl_ref[BLOCKS_PER_TILE * j, q, dst, :] = jnp.where(low_half, v0, swap_halves(v1))
                ufl_ref[BLOCKS_PER_TILE * j + 1, q, dst, :] = jnp.where(low_half, swap_halves(v0), v1)

    def chunk_lhs(blk):
        return jnp.concatenate([ufl_ref[blk, q] for q in range(STEP_PAIRS)], axis=-1).astype(_bf16)

    def bproj(j):
        for blk in range(BLOCKS_PER_TILE * j, BLOCKS_PER_TILE * (j + 1)):
            s_ref[:, blk * BLOCK_COLS:(blk + 1) * BLOCK_COLS] = jnp.dot(
                chunk_lhs(blk), bs_ref[blk], preferred_element_type=_f32)

    def scan(j):
        re, lam = [], []
        for blk in range(BLOCKS_PER_TILE * j, BLOCKS_PER_TILE * (j + 1)):
            for k in range(BLOCK_STATE // LANES):
                re.append(slice(blk * BLOCK_COLS + k * LANES, blk * BLOCK_COLS + (k + 1) * LANES))
                lam.append(slice(blk * BLOCK_STATE + k * LANES, blk * BLOCK_STATE + (k + 1) * LANES))
        im = [slice(r.start + BLOCK_STATE, r.stop + BLOCK_STATE) for r in re]
        n = len(re)
        lre = [lam_ref[0, :, ln] for ln in lam]
        lim = [lam_ref[1, :, ln] for ln in lam]
        xre = [st_ref[:, r] for r in re]
        xim = [st_ref[:, r] for r in im]
        for c in range(N_CHUNKS):
            rows = slice(c * BATCH, (c + 1) * BATCH)
            for k in range(n):
                loc_re, loc_im = s_ref[rows, re[k]], s_ref[rows, im[k]]
                s_ref[rows, re[k]] = xre[k]
                s_ref[rows, im[k]] = xim[k]
                xre[k], xim[k] = (lre[k] * xre[k] - lim[k] * xim[k] + loc_re,
                                  lre[k] * xim[k] + lim[k] * xre[k] + loc_im)
        for k in range(n):
            st_ref[:, re[k]] = xre[k]
            st_ref[:, im[k]] = xim[k]

    def cproj(j):
        for blk in range(BLOCKS_PER_TILE * j, BLOCKS_PER_TILE * (j + 1)):
            x_in = s_ref[:, blk * BLOCK_COLS:(blk + 1) * BLOCK_COLS].astype(_bf16)
            yf = jnp.dot(x_in, cs_ref[blk], preferred_element_type=_f32)
            yf = yf + jnp.dot(chunk_lhs(blk), tz_ref[blk], preferred_element_type=_f32)
            for q in range(STEP_PAIRS):
                yfl_ref[blk, q] = yf[:, q * LANES:(q + 1) * LANES]

    bproj(0)
    for j in range(N_LANE_TILES_U):
        if j + 1 < N_LANE_TILES_U:
            bproj(j + 1)
        if j > 0:
            cproj(j - 1)
        scan(j)
    cproj(N_LANE_TILES_U - 1)

    for j in range(N_LANE_TILES_U):
        for q in range(STEP_PAIRS):
            for b in range(BATCH):
                src = pl.ds(b, N_CHUNKS, stride=BATCH)
                lo = yfl_ref[BLOCKS_PER_TILE * j, q, src, :]
                hi = yfl_ref[BLOCKS_PER_TILE * j + 1, q, src, :]
                y32_ref[j, pl.ds(b * T_SSM + 2 * q, N_CHUNKS, stride=M_CHUNK), :] = (
                    jnp.where(low_half, lo, swap_halves(hi)))
                y32_ref[j, pl.ds(b * T_SSM + 2 * q + 1, N_CHUNKS, stride=M_CHUNK), :] = (
                    jnp.where(low_half, swap_halves(lo), hi))
    y = jnp.concatenate([y32_ref[j] for j in range(N_LANE_TILES_U)], axis=-1)
    z = _gelu_tanh(y).astype(_bf16)
    gate = jnp.dot(z, gluw_ref[...], preferred_element_type=_f32) + glub_ref[...]
    z = z * jax.nn.sigmoid(gate.astype(_bf16))
    ms = jnp.dot(z * z, ones_ref[...], preferred_element_type=_f32)
    out = z * lax.rsqrt(ms + EPS).astype(_bf16) * g_ref[...].astype(_bf16)
    y_ref[...] = out.reshape(BATCH, T_SSM, D_SSM)


def _ssm(u, bs_w, lam8, cs_w, tz_w, glu_w_b, glu_b, g_out_ssm, ones_ssm):
    rows = BATCH * T_SSM
    return pl.pallas_call(
        _ssm_kernel,
        grid=(SEQ // T_SSM,),
        in_specs=[
            pl.BlockSpec((BATCH, T_SSM, D_SSM), lambda i: (0, i, 0)),
            _const_spec((N_BLOCKS, BLOCK_CHUNK, BLOCK_COLS)),
            _const_spec((2, SUBLANES, N_STATE)),
            _const_spec((N_BLOCKS, BLOCK_COLS, BLOCK_CHUNK)),
            _const_spec((N_BLOCKS, BLOCK_CHUNK, BLOCK_CHUNK)),
            _const_spec((D_SSM, D_SSM)),
            _const_spec((1, D_SSM)),
            _const_spec((1, D_SSM)),
            _const_spec((D_SSM, D_SSM)),
        ],
        out_specs=pl.BlockSpec((BATCH, T_SSM, D_SSM), lambda i: (0, i, 0)),
        out_shape=jax.ShapeDtypeStruct((BATCH, SEQ, D_SSM), _bf16),
        scratch_shapes=[
            pltpu.VMEM((SUBLANES, N_BLOCKS * BLOCK_COLS), _f32),
            pltpu.VMEM((N_LANE_TILES_U, rows, LANES), _f32),
            pltpu.VMEM((N_BLOCKS, STEP_PAIRS, CHUNK_ROWS, LANES), _f32),
            pltpu.VMEM((CHUNK_ROWS, N_BLOCKS * BLOCK_COLS), _f32),
            pltpu.VMEM((N_BLOCKS, STEP_PAIRS, CHUNK_ROWS, LANES), _f32),
            pltpu.VMEM((N_LANE_TILES_U, rows, LANES), _f32),
        ],
        compiler_params=pltpu.CompilerParams(
            dimension_semantics=("arbitrary",), vmem_limit_bytes=VMEM_LIMIT),
        name="s5_mixer",
    )(u, bs_w, lam8, cs_w, tz_w, glu_w_b, glu_b, g_out_ssm, ones_ssm)


def _ffn_kernel(x_ref, ya_ref, yb_ref, mod_ref, wout_ref, gpm_ref, gpf_ref, wup_ref, fcw_ref, wdn_ref, gpo_ref,
                o_ref, halo_ref, act_ref, slab_ref):
    ti = pl.program_id(1)

    @pl.when(ti == 0)
    def _():
        halo_ref[...] = jnp.zeros(halo_ref.shape, _f32)

    x = x_ref[0]
    gt1 = mod_ref[0, 2:3, :]
    sh2 = mod_ref[0, 3:4, :]
    sc2 = mod_ref[0, 4:5, :]
    gt2 = mod_ref[0, 5:6, :]
    mix = jnp.dot(jnp.concatenate([ya_ref[0], yb_ref[0]], axis=-1), wout_ref[...], preferred_element_type=_f32)
    x1 = x + _rms(mix, gpm_ref[...] * gt1)
    h2 = (_rms(x1, gpf_ref[...] * (1.0 + sc2)) + sh2).astype(_bf16)

    def up_chunk(c):
        cols_a = slice(c * FF_CHUNK, (c + 1) * FF_CHUNK)
        cols_v = slice(D_FF + c * FF_CHUNK, D_FF + (c + 1) * FF_CHUNK)
        return (jnp.dot(h2, wup_ref[:, cols_a], preferred_element_type=_f32), cols_a,
                jnp.dot(h2, wup_ref[:, cols_v], preferred_element_type=_f32), cols_v)

    def conv(hc, cols, slab0):
        halo = halo_ref[:, cols]
        halo_ref[:, cols] = hc[T_FFN - SUBLANES:, :]
        return _causal_conv3(hc, halo, fcw_ref[:, cols], slab_ref, slab0)

    nxt = up_chunk(0)
    for c in range(N_FF_CHUNKS):
        ha, cols_a, hv, cols_v = nxt
        if c + 1 < N_FF_CHUNKS:
            nxt = up_chunk(c + 1)
        slab0 = (c % 2) * FFN_SLABS_PER_CHUNK
        a = conv(ha, cols_a, slab0)
        v = conv(hv, cols_v, slab0 + FF_CHUNK // LANES)
        act_ref[:, cols_a] = (a * jax.nn.sigmoid(a) * v).astype(_bf16)
    down = jnp.dot(act_ref[...], wdn_ref[...], preferred_element_type=_f32)
    o_ref[0] = x1 + _rms(down, gpo_ref[...] * gt2)


def _ffn(x, ya, yb, mod3, w_out_b, g_post_mix, g_pre_ffn, w_up_b, ffn_cw, w_down_b, g_post_ffn):
    n_t = SEQ // T_FFN
    return pl.pallas_call(
        _ffn_kernel,
        grid=(BATCH, n_t),
        in_specs=[
            pl.BlockSpec((1, T_FFN, D_MODEL), lambda b, t: (b, t, 0)),
            pl.BlockSpec((1, T_FFN, D_SSM), lambda b, t: (b, t, 0)),
            pl.BlockSpec((1, T_FFN, D_CONV), lambda b, t: (b, t, 0)),
            pl.BlockSpec((1, N_MOD, D_MODEL), lambda b, t: (b, 0, 0)),
            _const_spec((D_MODEL, D_MODEL)),
            _const_spec((1, D_MODEL)),
            _const_spec((1, D_MODEL)),
            _const_spec((D_MODEL, 2 * D_FF)),
            _const_spec((3, 2 * D_FF)),
            _const_spec((D_FF, D_MODEL)),
            _const_spec((1, D_MODEL)),
        ],
        out_specs=pl.BlockSpec((1, T_FFN, D_MODEL), lambda b, t: (b, t, 0)),
        out_shape=jax.ShapeDtypeStruct((BATCH, SEQ, D_MODEL), _f32),
        scratch_shapes=[
            pltpu.VMEM((SUBLANES, 2 * D_FF), _f32),
            pltpu.VMEM((T_FFN, D_FF), _bf16),
            pltpu.VMEM((2 * FFN_SLABS_PER_CHUNK, T_FFN + SUBLANES, LANES), _f32),
        ],
        compiler_params=pltpu.CompilerParams(
            dimension_semantics=("arbitrary", "arbitrary"), vmem_limit_bytes=VMEM_LIMIT),
        name="outproj_convffn",
    )(x, ya, yb, mod3, w_out_b, g_post_mix, g_pre_ffn, w_up_b, ffn_cw, w_down_b, g_post_ffn)


def _head_mean_matrix(width, head):
    idx = jnp.arange(width) // head
    return jnp.where(idx[:, None] == idx[None, :], 1.0 / head, 0.0).astype(_bf16)


def kernel(x, c, w_ada, b_ada, g_pre_mix, g_post_mix, w_in, ssm_lam_re, ssm_lam_im, ssm_log_step, ssm_b_re, ssm_b_im, ssm_c_re, ssm_c_im, ssm_d, glu_w, glu_b, g_out_ssm, conv_w, g_out_conv, w_out, g_pre_ffn, g_post_ffn, w_up, ffn_conv_w, w_down):
    assert x.shape == (BATCH, SEQ, D_MODEL) and w_ada.shape[0] == 1
    row = lambda a: a.reshape(1, -1)

    mod3 = _modulation(c, w_ada[0], b_ada[0]).reshape(BATCH, N_MOD, D_MODEL)

    lam8, bs_w, cs_w, tz_w = _ssm_prep(
        ssm_lam_re[0], ssm_lam_im[0], ssm_log_step[0], ssm_b_re[0], ssm_b_im[0], ssm_c_re[0], ssm_c_im[0],
        ssm_d[0])

    u, yb = _premix(x, mod3, row(g_pre_mix[0]), w_in[0], conv_w[0], row(g_out_conv[0]),
                    _head_mean_matrix(D_CONV, D_CONV // CONV_HEADS))

    ya = _ssm(u, bs_w, lam8, cs_w, tz_w, glu_w[0].astype(_bf16),
              row(glu_b[0]), row(g_out_ssm[0]), _head_mean_matrix(D_SSM, SSM_GROUP))

    return _ffn(x, ya, yb, mod3, w_out[0].astype(_bf16), row(g_post_mix[0]), row(g_pre_ffn[0]),
                w_up[0].astype(_bf16), ffn_conv_w[0], w_down[0].astype(_bf16), row(g_post_ffn[0]))
```

```python
import math

import jax
import jax.numpy as jnp
from jax import lax
from jax.experimental import pallas as pl
from jax.experimental.pallas import tpu as pltpu

D_MODEL = 1024
BATCH = 8
SEQ = 4096
D_SSM = 512
D_CONV = 512
SSM_GROUP = 16
N_SSM_GROUPS = 32
SSM_STATE = 64
CONV_HEADS = 8
D_FF = 2816
N_MOD = 6
D_IN_PROJ = 2048
EPS = 1e-6
LAMBDA_RE_MAX = -1e-4

N_STATE = N_SSM_GROUPS * SSM_STATE
SUBLANES = 8
LANES = 128
GROUPS_PER_LANE_TILE = LANES // SSM_GROUP
N_LANE_TILES_U = D_SSM // LANES

T_PRE = 1024
N_PRE_STEPS = BATCH * (SEQ // T_PRE)
WDN_BLOCK_ROWS = 176
N_WDN_BLOCKS = D_FF // WDN_BLOCK_ROWS
T_SSM = 256
T_FFN = 1024
FF_CHUNK = 256
N_FF_CHUNKS = D_FF // FF_CHUNK
FFN_SLABS_PER_CHUNK = 2 * FF_CHUNK // LANES
M_CHUNK = 4
N_CHUNKS = T_SSM // M_CHUNK
CHUNK_ROWS = N_CHUNKS * BATCH
GROUPS_PER_BLOCK = GROUPS_PER_LANE_TILE // 2
BLOCKS_PER_TILE = GROUPS_PER_LANE_TILE // GROUPS_PER_BLOCK
N_BLOCKS = N_SSM_GROUPS // GROUPS_PER_BLOCK
BLOCK_LANES = GROUPS_PER_BLOCK * SSM_GROUP
BLOCK_STATE = GROUPS_PER_BLOCK * SSM_STATE
BLOCK_CHUNK = M_CHUNK * BLOCK_LANES
BLOCK_COLS = 2 * BLOCK_STATE
STEP_PAIRS = M_CHUNK // 2
VMEM_LIMIT = 60 * 1024 * 1024

_f32 = jnp.float32
_bf16 = jnp.bfloat16


def _const_spec(shape):
    nd = len(shape)
    return pl.BlockSpec(shape, lambda *_: (0,) * nd, pipeline_mode=pl.Buffered(1))


def _rms(x, g):
    ms = jnp.mean(x * x, axis=-1, keepdims=True)
    return x * lax.rsqrt(ms + EPS) * g


def _gelu_tanh(y):
    k = math.sqrt(2.0 / math.pi)
    half_y = 0.5 * y
    return half_y + half_y * jnp.tanh(y * (k + (k * 0.044715) * (y * y)))


def _causal_conv3(h, halo, w, slab_ref, slab0):
    rows = h.shape[0]
    outs = []
    for k in range(h.shape[1] // LANES):
        lanes = slice(k * LANES, (k + 1) * LANES)
        slab_ref[slab0 + k, 0:SUBLANES, :] = halo[:, lanes]
        slab_ref[slab0 + k, SUBLANES:SUBLANES + rows, :] = h[:, lanes]
        h1 = slab_ref[slab0 + k, pl.ds(SUBLANES - 1, rows, stride=1), :]
        h2 = slab_ref[slab0 + k, pl.ds(SUBLANES - 2, rows, stride=1), :]
        outs.append(w[0:1, lanes] * h2 + w[1:2, lanes] * h1 + w[2:3, lanes] * h[:, lanes])
    return jnp.concatenate(outs, axis=-1)


def _mod_kernel(c_ref, w_ref, b_ref, o_ref):
    c = c_ref[...]
    c_act = c * jax.nn.sigmoid(c)
    o_ref[...] = jnp.dot(c_act, w_ref[...], preferred_element_type=_f32) + b_ref[...]


def _modulation(c, w_ada, b_ada):
    n_tile = 2 * D_MODEL
    return pl.pallas_call(
        _mod_kernel,
        grid=(N_MOD * D_MODEL // n_tile,),
        in_specs=[
            pl.BlockSpec((BATCH, D_MODEL), lambda j: (0, 0)),
            pl.BlockSpec((D_MODEL, n_tile), lambda j: (0, j)),
            pl.BlockSpec((1, n_tile), lambda j: (0, j)),
        ],
        out_specs=pl.BlockSpec((BATCH, n_tile), lambda j: (0, j)),
        out_shape=jax.ShapeDtypeStruct((BATCH, N_MOD * D_MODEL), _f32),
        name="adaln_mod",
    )(c, w_ada, b_ada.reshape(1, -1))


def _prep_kernel(lre_ref, lim_ref, lst_ref, bre_ref, bim_ref, cre_ref, cim_ref, d_ref,
                 lam8_ref, bs_ref, cs_ref, tz_ref):
    lre = jnp.minimum(lre_ref[0], LAMBDA_RE_MAX)
    lim = lim_ref[0]
    step = jnp.exp(lst_ref[0])
    log_mag, ang = lre * step, lim * step

    def lam_pow(k):
        mag = jnp.exp(k * log_mag)
        return mag * jnp.cos(k * ang), mag * jnp.sin(k * ang)

    pw = [lam_pow(float(k)) for k in range(M_CHUNK + 1)]

    def block_diag(blk):
        wide = jnp.concatenate([blk] * GROUPS_PER_BLOCK, axis=-1)
        r = lax.broadcasted_iota(jnp.int32, wide.shape, 0) // SSM_GROUP
        c = lax.broadcasted_iota(jnp.int32, wide.shape, 1) // SSM_STATE
        return jnp.where(r == c, wide, 0.0)

    a_re, a_im = pw[1]
    n_re = a_re - 1.0
    den = lre * lre + lim * lim
    q_re = (n_re * lre + a_im * lim) / den
    q_im = (a_im * lre - n_re * lim) / den
    b_re, b_im = block_diag(bre_ref[0]), block_diag(bim_ref[0])
    bb_re = q_re * b_re - q_im * b_im
    bb_im = q_re * b_im + q_im * b_re
    c_re, c_im = block_diag(cre_ref[0]), block_diag(cim_ref[0])

    lam8_ref[0] = jnp.broadcast_to(pw[M_CHUNK][0], (SUBLANES, BLOCK_STATE))
    lam8_ref[1] = jnp.broadcast_to(pw[M_CHUNK][1], (SUBLANES, BLOCK_STATE))
    for s in range(M_CHUNK):
        rows = slice(s * BLOCK_LANES, (s + 1) * BLOCK_LANES)
        p_re, p_im = pw[M_CHUNK - 1 - s]
        bs_ref[0, rows, 0:BLOCK_STATE] = (p_re * bb_re - p_im * bb_im).astype(_bf16)
        bs_ref[0, rows, BLOCK_STATE:] = (p_re * bb_im + p_im * bb_re).astype(_bf16)
        p_re, p_im = pw[s + 1]
        cs_ref[0, 0:BLOCK_STATE, rows] = (p_re * c_re - p_im * c_im).T.astype(_bf16)
        cs_ref[0, BLOCK_STATE:, rows] = (-(p_re * c_im + p_im * c_re)).T.astype(_bf16)
    tz_ref[...] = jnp.zeros(tz_ref.shape, _bf16)
    contract_p = (((1,), (1,)), ((), ()))
    for tau in range(M_CHUNK):
        p_re, p_im = pw[tau]
        ct_re = p_re * c_re - p_im * c_im
        ct_im = p_re * c_im + p_im * c_re
        k_tau = (lax.dot_general(bb_re, ct_re, contract_p, precision=lax.Precision.HIGHEST,
                                 preferred_element_type=_f32)
                 - lax.dot_general(bb_im, ct_im, contract_p, precision=lax.Precision.HIGHEST,
                                   preferred_element_type=_f32))
        if tau == 0:
            r = lax.broadcasted_iota(jnp.int32, k_tau.shape, 0)
            c = lax.broadcasted_iota(jnp.int32, k_tau.shape, 1)
            k_tau = k_tau + jnp.where(r == c, d_ref[0], 0.0)
        for s0 in range(M_CHUNK - tau):
            tz_ref[0, s0 * BLOCK_LANES:(s0 + 1) * BLOCK_LANES,
                   (s0 + tau) * BLOCK_LANES:(s0 + tau + 1) * BLOCK_LANES] = k_tau.astype(_bf16)


def _ssm_prep(lam_re, lam_im, log_step, b_re, b_im, c_re, c_im, d_skip):
    nb = N_BLOCKS
    lstep = jnp.broadcast_to(log_step[:, None], (N_SSM_GROUPS, SSM_STATE))
    as_row = lambda a: a.reshape(nb, 1, BLOCK_STATE)
    b_t = lambda a: a.transpose(0, 2, 1).reshape(nb, BLOCK_LANES, SSM_STATE)
    c_t = lambda a: a.reshape(nb, BLOCK_LANES, SSM_STATE)
    tile_spec = lambda r, c: pl.BlockSpec((1, r, c), lambda j: (j, 0, 0))
    return pl.pallas_call(
        _prep_kernel,
        grid=(nb,),
        in_specs=[tile_spec(1, BLOCK_STATE)] * 3 + [tile_spec(BLOCK_LANES, SSM_STATE)] * 4
        + [tile_spec(1, BLOCK_LANES)],
        out_specs=[
            pl.BlockSpec((2, SUBLANES, BLOCK_STATE), lambda j: (0, 0, j)),
            tile_spec(BLOCK_CHUNK, BLOCK_COLS),
            tile_spec(BLOCK_COLS, BLOCK_CHUNK),
            tile_spec(BLOCK_CHUNK, BLOCK_CHUNK),
        ],
        out_shape=(
            jax.ShapeDtypeStruct((2, SUBLANES, N_STATE), _f32),
            jax.ShapeDtypeStruct((nb, BLOCK_CHUNK, BLOCK_COLS), _bf16),
            jax.ShapeDtypeStruct((nb, BLOCK_COLS, BLOCK_CHUNK), _bf16),
            jax.ShapeDtypeStruct((nb, BLOCK_CHUNK, BLOCK_CHUNK), _bf16),
        ),
        name="s5_discretise",
    )(as_row(lam_re), as_row(lam_im), as_row(lstep), b_t(b_re), b_t(b_im), c_t(c_re), c_t(c_im),
      d_skip.reshape(nb, 1, BLOCK_LANES))


def _premix_kernel(x_ref, mod_ref, g_ref, win32_ref, cw_ref, gconv_ref, ones_ref, wup32_ref, wout32_ref, wdn32_ref,
                   u_ref, yb_ref, wup_ref, wout_ref, wdn_ref, halo_ref, slab_ref, win_ref):
    ti = pl.program_id(1)
    step = pl.program_id(0) * (SEQ // T_PRE) + ti

    @pl.when(step == 0)
    def _():
        for k in range(D_IN_PROJ // D_CONV):
            cols = slice(k * D_CONV, (k + 1) * D_CONV)
            win_ref[:, cols] = win32_ref[:, cols].astype(_bf16)

    wup_ref[...] = wup32_ref[...].astype(_bf16)
    wout_ref[...] = wout32_ref[...].astype(_bf16)

    @pl.when(step < N_WDN_BLOCKS)
    def _():
        wdn_ref[...] = wdn32_ref[...].astype(_bf16)

    @pl.when(ti == 0)
    def _():
        halo_ref[...] = jnp.zeros(halo_ref.shape, _f32)

    x = x_ref[0]
    sh = mod_ref[0, 0:1, :]
    sc = mod_ref[0, 1:2, :]
    h = _rms(x, g_ref[...] * (1.0 + sc)) + sh
    hb = h.astype(_bf16)
    proj = jnp.dot(hb, win_ref[:, D_SSM:], preferred_element_type=_f32)
    u_ref[0] = jnp.dot(hb, win_ref[:, 0:D_SSM], preferred_element_type=_f32).astype(_bf16)
    bg = proj[:, 0:D_CONV]
    cg = proj[:, D_CONV:2 * D_CONV]
    v = proj[:, 2 * D_CONV:]
    cv = cg * v
    halo = halo_ref[...]
    halo_ref[...] = cv[T_PRE - SUBLANES:, :]
    yb = bg * _causal_conv3(cv, halo, cw_ref, slab_ref, 0)
    ms = jnp.dot((yb * yb).astype(_bf16), ones_ref[...], preferred_element_type=_f32)
    yb_ref[0] = (yb * lax.rsqrt(ms + EPS) * gconv_ref[...]).astype(_bf16)


def _premix(x, mod3, g_pre, w_in, conv_w, g_out_conv, ones_conv, w_up, w_out, w_down):
    n_t = SEQ // T_PRE
    step = lambda b, t: b * n_t + t
    up_rows = D_MODEL // N_PRE_STEPS
    by_step = lambda b, t: (step(b, t), 0)
    wdn_block = lambda b, t: (jnp.minimum(step(b, t), N_WDN_BLOCKS - 1), 0)
    return pl.pallas_call(
        _premix_kernel,
        grid=(BATCH, n_t),
        in_specs=[
            pl.BlockSpec((1, T_PRE, D_MODEL), lambda b, t: (b, t, 0)),
            pl.BlockSpec((1, N_MOD, D_MODEL), lambda b, t: (b, 0, 0)),
            _const_spec((1, D_MODEL)),
            _const_spec((D_MODEL, D_IN_PROJ)),
            _const_spec((3, D_CONV)),
            _const_spec((1, D_CONV)),
            _const_spec((D_CONV, D_CONV)),
            pl.BlockSpec((up_rows, 2 * D_FF), by_step),
            pl.BlockSpec((up_rows, D_MODEL), by_step),
            pl.BlockSpec((WDN_BLOCK_ROWS, D_MODEL), wdn_block),
        ],
        out_specs=[
            pl.BlockSpec((1, T_PRE, D_SSM), lambda b, t: (b, t, 0)),
            pl.BlockSpec((1, T_PRE, D_CONV), lambda b, t: (b, t, 0)),
            pl.BlockSpec((up_rows, 2 * D_FF), by_step),
            pl.BlockSpec((up_rows, D_MODEL), by_step),
            pl.BlockSpec((WDN_BLOCK_ROWS, D_MODEL), wdn_block),
        ],
        out_shape=[
            jax.ShapeDtypeStruct((BATCH, SEQ, D_SSM), _bf16),
            jax.ShapeDtypeStruct((BATCH, SEQ, D_CONV), _bf16),
            jax.ShapeDtypeStruct((D_MODEL, 2 * D_FF), _bf16),
            jax.ShapeDtypeStruct((D_MODEL, D_MODEL), _bf16),
            jax.ShapeDtypeStruct((D_FF, D_MODEL), _bf16),
        ],
        scratch_shapes=[
            pltpu.VMEM((SUBLANES, D_CONV), _f32),
            pltpu.VMEM((D_CONV // LANES, T_PRE + SUBLANES, LANES), _f32),
            pltpu.VMEM((D_MODEL, D_IN_PROJ), _bf16),
        ],
        compiler_params=pltpu.CompilerParams(
            dimension_semantics=("arbitrary", "arbitrary"), vmem_limit_bytes=VMEM_LIMIT),
        name="premix",
    )(x, mod3, g_pre, w_in, conv_w, g_out_conv, ones_conv, w_up, w_out, w_down)


def _ssm_kernel(u_ref, bs_ref, lam_ref, cs_ref, tz_ref, gluw_ref, glub_ref, g_ref, ones_ref,
                y_ref, st_ref, u32_ref, ufl_ref, s_ref, yfl_ref, y32_ref):
    i = pl.program_id(0)

    @pl.when(i == 0)
    def _():
        st_ref[...] = jnp.zeros(st_ref.shape, _f32)

    low_half = lax.broadcasted_iota(jnp.int32, (N_CHUNKS, LANES), 1) < BLOCK_LANES
    swap_halves = lambda v: pltpu.roll(v, BLOCK_LANES, axis=1)

    u = u_ref[...].reshape(BATCH * T_SSM, D_SSM)
    for j in range(N_LANE_TILES_U):
        u32_ref[j] = u[:, j * LANES:(j + 1) * LANES].astype(_f32)
    for j in range(N_LANE_TILES_U):
        for q in range(STEP_PAIRS):
            for b in range(BATCH):
                v0 = u32_ref[j, pl.ds(b * T_SSM + 2 * q, N_CHUNKS, stride=M_CHUNK), :]
                v1 = u32_ref[j, pl.ds(b * T_SSM + 2 * q + 1, N_CHUNKS, stride=M_CHUNK), :]
                dst = pl.ds(b, N_CHUNKS, stride=BATCH)
                ufl_ref[BLOCKS_PER_TILE * j, q, dst, :] = jnp.where(low_half, v0, swap_halves(v1))
                ufl_ref[BLOCKS_PER_TILE * j + 1, q, dst, :] = jnp.where(low_half, swap_halves(v0), v1)

    def chunk_lhs(blk):
        return jnp.concatenate([ufl_ref[blk, q] for q in range(STEP_PAIRS)], axis=-1).astype(_bf16)

    def bproj(j):
        for blk in range(BLOCKS_PER_TILE * j, BLOCKS_PER_TILE * (j + 1)):
            s_ref[:, blk * BLOCK_COLS:(blk + 1) * BLOCK_COLS] = jnp.dot(
                chunk_lhs(blk), bs_ref[blk], preferred_element_type=_f32)

    def scan(j):
        re, lam = [], []
        for blk in range(BLOCKS_PER_TILE * j, BLOCKS_PER_TILE * (j + 1)):
            for k in range(BLOCK_STATE // LANES):
                re.append(slice(blk * BLOCK_COLS + k * LANES, blk * BLOCK_COLS + (k + 1) * LANES))
                lam.append(slice(blk * BLOCK_STATE + k * LANES, blk * BLOCK_STATE + (k + 1) * LANES))
        im = [slice(r.start + BLOCK_STATE, r.stop + BLOCK_STATE) for r in re]
        n = len(re)
        lre = [lam_ref[0, :, ln] for ln in lam]
        lim = [lam_ref[1, :, ln] for ln in lam]
        xre = [st_ref[:, r] for r in re]
        xim = [st_ref[:, r] for r in im]
        for c in range(N_CHUNKS):
            rows = slice(c * BATCH, (c + 1) * BATCH)
            for k in range(n):
                loc_re, loc_im = s_ref[rows, re[k]], s_ref[rows, im[k]]
                s_ref[rows, re[k]] = xre[k]
                s_ref[rows, im[k]] = xim[k]
                xre[k], xim[k] = (lre[k] * xre[k] - lim[k] * xim[k] + loc_re,
                                  lre[k] * xim[k] + lim[k] * xre[k] + loc_im)
        for k in range(n):
            st_ref[:, re[k]] = xre[k]
            st_ref[:, im[k]] = xim[k]

    def cproj(j):
        for blk in range(BLOCKS_PER_TILE * j, BLOCKS_PER_TILE * (j + 1)):
            x_in = s_ref[:, blk * BLOCK_COLS:(blk + 1) * BLOCK_COLS].astype(_bf16)
            yf = jnp.dot(x_in, cs_ref[blk], preferred_element_type=_f32)
            yf = yf + jnp.dot(chunk_lhs(blk), tz_ref[blk], preferred_element_type=_f32)
            for q in range(STEP_PAIRS):
                yfl_ref[blk, q] = yf[:, q * LANES:(q + 1) * LANES]

    bproj(0)
    for j in range(N_LANE_TILES_U):
        if j + 1 < N_LANE_TILES_U:
            bproj(j + 1)
        if j > 0:
            cproj(j - 1)
        scan(j)
    cproj(N_LANE_TILES_U - 1)

    for j in range(N_LANE_TILES_U):
        for q in range(STEP_PAIRS):
            for b in range(BATCH):
                src = pl.ds(b, N_CHUNKS, stride=BATCH)
                lo = yfl_ref[BLOCKS_PER_TILE * j, q, src, :]
                hi = yfl_ref[BLOCKS_PER_TILE * j + 1, q, src, :]
                y32_ref[j, pl.ds(b * T_SSM + 2 * q, N_CHUNKS, stride=M_CHUNK), :] = (
                    jnp.where(low_half, lo, swap_halves(hi)))
                y32_ref[j, pl.ds(b * T_SSM + 2 * q + 1, N_CHUNKS, stride=M_CHUNK), :] = (
                    jnp.where(low_half, swap_halves(lo), hi))
    y = jnp.concatenate([y32_ref[j] for j in range(N_LANE_TILES_U)], axis=-1)
    z = _gelu_tanh(y).astype(_bf16)
    gate = jnp.dot(z, gluw_ref[...], preferred_element_type=_f32) + glub_ref[...]
    z = z * jax.nn.sigmoid(gate.astype(_bf16))
    ms = jnp.dot(z * z, ones_ref[...], preferred_element_type=_f32)
    out = z * lax.rsqrt(ms + EPS).astype(_bf16) * g_ref[...].astype(_bf16)
    y_ref[...] = out.reshape(BATCH, T_SSM, D_SSM)


def _ssm(u, bs_w, lam8, cs_w, tz_w, glu_w_b, glu_b, g_out_ssm, ones_ssm):
    rows = BATCH * T_SSM
    return pl.pallas_call(
        _ssm_kernel,
        grid=(SEQ // T_SSM,),
        in_specs=[
            pl.BlockSpec((BATCH, T_SSM, D_SSM), lambda i: (0, i, 0)),
            _const_spec((N_BLOCKS, BLOCK_CHUNK, BLOCK_COLS)),
            _const_spec((2, SUBLANES, N_STATE)),
            _const_spec((N_BLOCKS, BLOCK_COLS, BLOCK_CHUNK)),
            _const_spec((N_BLOCKS, BLOCK_CHUNK, BLOCK_CHUNK)),
            _const_spec((D_SSM, D_SSM)),
            _const_spec((1, D_SSM)),
            _const_spec((1, D_SSM)),
            _const_spec((D_SSM, D_SSM)),
        ],
        out_specs=pl.BlockSpec((BATCH, T_SSM, D_SSM), lambda i: (0, i, 0)),
        out_shape=jax.ShapeDtypeStruct((BATCH, SEQ, D_SSM), _bf16),
        scratch_shapes=[
            pltpu.VMEM((SUBLANES, N_BLOCKS * BLOCK_COLS), _f32),
            pltpu.VMEM((N_LANE_TILES_U, rows, LANES), _f32),
            pltpu.VMEM((N_BLOCKS, STEP_PAIRS, CHUNK_ROWS, LANES), _f32),
            pltpu.VMEM((CHUNK_ROWS, N_BLOCKS * BLOCK_COLS), _f32),
            pltpu.VMEM((N_BLOCKS, STEP_PAIRS, CHUNK_ROWS, LANES), _f32),
            pltpu.VMEM((N_LANE_TILES_U, rows, LANES), _f32),
        ],
        compiler_params=pltpu.CompilerParams(
            dimension_semantics=("arbitrary",), vmem_limit_bytes=VMEM_LIMIT),
        name="s5_mixer",
    )(u, bs_w, lam8, cs_w, tz_w, glu_w_b, glu_b, g_out_ssm, ones_ssm)


def _ffn_kernel(x_ref, ya_ref, yb_ref, mod_ref, wout_ref, gpm_ref, gpf_ref, wup_ref, fcw_ref, wdn_ref, gpo_ref,
                o_ref, halo_ref, act_ref, slab_ref):
    ti = pl.program_id(1)

    @pl.when(ti == 0)
    def _():
        halo_ref[...] = jnp.zeros(halo_ref.shape, _f32)

    x = x_ref[0]
    gt1 = mod_ref[0, 2:3, :]
    sh2 = mod_ref[0, 3:4, :]
    sc2 = mod_ref[0, 4:5, :]
    gt2 = mod_ref[0, 5:6, :]
    mix = jnp.dot(jnp.concatenate([ya_ref[0], yb_ref[0]], axis=-1), wout_ref[...], preferred_element_type=_f32)
    x1 = x + _rms(mix, gpm_ref[...] * gt1)
    h2 = (_rms(x1, gpf_ref[...] * (1.0 + sc2)) + sh2).astype(_bf16)

    def up_chunk(c):
        cols_a = slice(c * FF_CHUNK, (c + 1) * FF_CHUNK)
        cols_v = slice(D_FF + c * FF_CHUNK, D_FF + (c + 1) * FF_CHUNK)
        return (jnp.dot(h2, wup_ref[:, cols_a], preferred_element_type=_f32), cols_a,
                jnp.dot(h2, wup_ref[:, cols_v], preferred_element_type=_f32), cols_v)

    def conv(hc, cols, slab0):
        halo = halo_ref[:, cols]
        halo_ref[:, cols] = hc[T_FFN - SUBLANES:, :]
        return _causal_conv3(hc, halo, fcw_ref[:, cols], slab_ref, slab0)

    nxt = up_chunk(0)
    for c in range(N_FF_CHUNKS):
        ha, cols_a, hv, cols_v = nxt
        if c + 1 < N_FF_CHUNKS:
            nxt = up_chunk(c + 1)
        slab0 = (c % 2) * FFN_SLABS_PER_CHUNK
        a = conv(ha, cols_a, slab0)
        v = conv(hv, cols_v, slab0 + FF_CHUNK // LANES)
        act_ref[:, cols_a] = (a * jax.nn.sigmoid(a) * v).astype(_bf16)
    down = jnp.dot(act_ref[...], wdn_ref[...], preferred_element_type=_f32)
    o_ref[0] = x1 + _rms(down, gpo_ref[...] * gt2)


def _ffn(x, ya, yb, mod3, w_out_b, g_post_mix, g_pre_ffn, w_up_b, ffn_cw, w_down_b, g_post_ffn):
    n_t = SEQ // T_FFN
    return pl.pallas_call(
        _ffn_kernel,
        grid=(BATCH, n_t),
        in_specs=[
            pl.BlockSpec((1, T_FFN, D_MODEL), lambda b, t: (b, t, 0)),
            pl.BlockSpec((1, T_FFN, D_SSM), lambda b, t: (b, t, 0)),
            pl.BlockSpec((1, T_FFN, D_CONV), lambda b, t: (b, t, 0)),
            pl.BlockSpec((1, N_MOD, D_MODEL), lambda b, t: (b, 0, 0)),
            _const_spec((D_MODEL, D_MODEL)),
            _const_spec((1, D_MODEL)),
            _const_spec((1, D_MODEL)),
            _const_spec((D_MODEL, 2 * D_FF)),
            _const_spec((3, 2 * D_FF)),
            _const_spec((D_FF, D_MODEL)),
            _const_spec((1, D_MODEL)),
        ],
        out_specs=pl.BlockSpec((1, T_FFN, D_MODEL), lambda b, t: (b, t, 0)),
        out_shape=jax.ShapeDtypeStruct((BATCH, SEQ, D_MODEL), _f32),
        scratch_shapes=[
            pltpu.VMEM((SUBLANES, 2 * D_FF), _f32),
            pltpu.VMEM((T_FFN, D_FF), _bf16),
            pltpu.VMEM((2 * FFN_SLABS_PER_CHUNK, T_FFN + SUBLANES, LANES), _f32),
        ],
        compiler_params=pltpu.CompilerParams(
            dimension_semantics=("arbitrary", "arbitrary"), vmem_limit_bytes=VMEM_LIMIT),
        name="outproj_convffn",
    )(x, ya, yb, mod3, w_out_b, g_post_mix, g_pre_ffn, w_up_b, ffn_cw, w_down_b, g_post_ffn)


def _head_mean_matrix(width, head):
    idx = jnp.arange(width) // head
    return jnp.where(idx[:, None] == idx[None, :], 1.0 / head, 0.0).astype(_bf16)


def kernel(x, c, w_ada, b_ada, g_pre_mix, g_post_mix, w_in, ssm_lam_re, ssm_lam_im, ssm_log_step, ssm_b_re, ssm_b_im, ssm_c_re, ssm_c_im, ssm_d, glu_w, glu_b, g_out_ssm, conv_w, g_out_conv, w_out, g_pre_ffn, g_post_ffn, w_up, ffn_conv_w, w_down):
    assert x.shape == (BATCH, SEQ, D_MODEL) and w_ada.shape[0] == 1
    row = lambda a: a.reshape(1, -1)

    mod3 = _modulation(c, w_ada[0], b_ada[0]).reshape(BATCH, N_MOD, D_MODEL)

    lam8, bs_w, cs_w, tz_w = _ssm_prep(
        ssm_lam_re[0], ssm_lam_im[0], ssm_log_step[0], ssm_b_re[0], ssm_b_im[0], ssm_c_re[0], ssm_c_im[0],
        ssm_d[0])

    u, yb, w_up_b, w_out_b, w_down_b = _premix(
        x, mod3, row(g_pre_mix[0]), w_in[0], conv_w[0], row(g_out_conv[0]),
        _head_mean_matrix(D_CONV, D_CONV // CONV_HEADS), w_up[0], w_out[0], w_down[0])

    ya = _ssm(u, bs_w, lam8, cs_w, tz_w, glu_w[0].astype(_bf16),
              row(glu_b[0]), row(g_out_ssm[0]), _head_mean_matrix(D_SSM, SSM_GROUP))

    return _ffn(x, ya, yb, mod3, w_out_b, row(g_post_mix[0]), row(g_pre_ffn[0]),
                w_up_b, ffn_conv_w[0], w_down_b, row(g_post_ffn[0]))
```

```python
import math

import jax
import jax.numpy as jnp
from jax import lax
from jax.experimental import pallas as pl
from jax.experimental.pallas import tpu as pltpu

D_MODEL = 1024
BATCH = 8
SEQ = 4096
D_SSM = 512
D_CONV = 512
SSM_GROUP = 16
N_SSM_GROUPS = 32
SSM_STATE = 64
CONV_HEADS = 8
D_FF = 2816
N_MOD = 6
D_IN_PROJ = 2048
EPS = 1e-6
LAMBDA_RE_MAX = -1e-4

N_STATE = N_SSM_GROUPS * SSM_STATE
SUBLANES = 8
LANES = 128
GROUPS_PER_LANE_TILE = LANES // SSM_GROUP
N_LANE_TILES_U = D_SSM // LANES

T_PRE = 1024
N_PRE_STEPS = BATCH * (SEQ // T_PRE)
WDN_BLOCK_ROWS = 176
N_WDN_BLOCKS = D_FF // WDN_BLOCK_ROWS
T_SSM = 256
T_FFN = 1024
FF_CHUNK = 256
N_FF_CHUNKS = D_FF // FF_CHUNK
FFN_SLABS_PER_CHUNK = 2 * FF_CHUNK // LANES
M_CHUNK = 4
N_CHUNKS = T_SSM // M_CHUNK
CHUNK_ROWS = N_CHUNKS * BATCH
GROUPS_PER_BLOCK = GROUPS_PER_LANE_TILE // 2
BLOCKS_PER_TILE = GROUPS_PER_LANE_TILE // GROUPS_PER_BLOCK
N_BLOCKS = N_SSM_GROUPS // GROUPS_PER_BLOCK
BLOCK_LANES = GROUPS_PER_BLOCK * SSM_GROUP
BLOCK_STATE = GROUPS_PER_BLOCK * SSM_STATE
BLOCK_CHUNK = M_CHUNK * BLOCK_LANES
BLOCK_COLS = 2 * BLOCK_STATE
N_PREP_IN = 8
STEP_PAIRS = M_CHUNK // 2
VMEM_LIMIT = 60 * 1024 * 1024

_f32 = jnp.float32
_bf16 = jnp.bfloat16


def _const_spec(shape):
    nd = len(shape)
    return pl.BlockSpec(shape, lambda *_: (0,) * nd, pipeline_mode=pl.Buffered(1))


def _rms(x, g):
    ms = jnp.mean(x * x, axis=-1, keepdims=True)
    return x * lax.rsqrt(ms + EPS) * g


def _gelu_tanh(y):
    k = math.sqrt(2.0 / math.pi)
    half_y = 0.5 * y
    return half_y + half_y * jnp.tanh(y * (k + (k * 0.044715) * (y * y)))


def _causal_conv3(h, halo, w, slab_ref, slab0):
    rows = h.shape[0]
    outs = []
    for k in range(h.shape[1] // LANES):
        lanes = slice(k * LANES, (k + 1) * LANES)
        slab_ref[slab0 + k, 0:SUBLANES, :] = halo[:, lanes]
        slab_ref[slab0 + k, SUBLANES:SUBLANES + rows, :] = h[:, lanes]
        h1 = slab_ref[slab0 + k, pl.ds(SUBLANES - 1, rows, stride=1), :]
        h2 = slab_ref[slab0 + k, pl.ds(SUBLANES - 2, rows, stride=1), :]
        outs.append(w[0:1, lanes] * h2 + w[1:2, lanes] * h1 + w[2:3, lanes] * h[:, lanes])
    return jnp.concatenate(outs, axis=-1)


def _mod_kernel(c_ref, w_ref, b_ref, o_ref):
    c = c_ref[...]
    c_act = c * jax.nn.sigmoid(c)
    o_ref[...] = jnp.dot(c_act, w_ref[...], preferred_element_type=_f32) + b_ref[...]


def _setup_kernel(c_ref, w_ref, b_ref, *refs):
    mod_ref, prep_refs = refs[N_PREP_IN], refs[:N_PREP_IN] + refs[N_PREP_IN + 1:]
    _mod_kernel(c_ref, w_ref, b_ref, mod_ref)
    _prep_kernel(*prep_refs)


def _prep_kernel(lre_ref, lim_ref, lst_ref, bre_ref, bim_ref, cre_ref, cim_ref, d_ref,
                 lam8_ref, bs_ref, cs_ref, tz_ref):
    lre = jnp.minimum(lre_ref[0], LAMBDA_RE_MAX)
    lim = lim_ref[0]
    step = jnp.exp(lst_ref[0])
    log_mag, ang = lre * step, lim * step

    def lam_pow(k):
        mag = jnp.exp(k * log_mag)
        return mag * jnp.cos(k * ang), mag * jnp.sin(k * ang)

    pw = [lam_pow(float(k)) for k in range(M_CHUNK + 1)]

    def block_diag(blk):
        wide = jnp.concatenate([blk] * GROUPS_PER_BLOCK, axis=-1)
        r = lax.broadcasted_iota(jnp.int32, wide.shape, 0) // SSM_GROUP
        c = lax.broadcasted_iota(jnp.int32, wide.shape, 1) // SSM_STATE
        return jnp.where(r == c, wide, 0.0)

    a_re, a_im = pw[1]
    n_re = a_re - 1.0
    den = lre * lre + lim * lim
    q_re = (n_re * lre + a_im * lim) / den
    q_im = (a_im * lre - n_re * lim) / den
    b_re, b_im = block_diag(bre_ref[0]), block_diag(bim_ref[0])
    bb_re = q_re * b_re - q_im * b_im
    bb_im = q_re * b_im + q_im * b_re
    c_re, c_im = block_diag(cre_ref[0]), block_diag(cim_ref[0])

    lam8_ref[0] = jnp.broadcast_to(pw[M_CHUNK][0], (SUBLANES, BLOCK_STATE))
    lam8_ref[1] = jnp.broadcast_to(pw[M_CHUNK][1], (SUBLANES, BLOCK_STATE))
    for s in range(M_CHUNK):
        rows = slice(s * BLOCK_LANES, (s + 1) * BLOCK_LANES)
        p_re, p_im = pw[M_CHUNK - 1 - s]
        bs_ref[0, rows, 0:BLOCK_STATE] = (p_re * bb_re - p_im * bb_im).astype(_bf16)
        bs_ref[0, rows, BLOCK_STATE:] = (p_re * bb_im + p_im * bb_re).astype(_bf16)
        p_re, p_im = pw[s + 1]
        cs_ref[0, 0:BLOCK_STATE, rows] = (p_re * c_re - p_im * c_im).T.astype(_bf16)
        cs_ref[0, BLOCK_STATE:, rows] = (-(p_re * c_im + p_im * c_re)).T.astype(_bf16)
    tz_ref[...] = jnp.zeros(tz_ref.shape, _bf16)
    contract_p = (((1,), (1,)), ((), ()))
    for tau in range(M_CHUNK):
        p_re, p_im = pw[tau]
        ct_re = p_re * c_re - p_im * c_im
        ct_im = p_re * c_im + p_im * c_re
        k_tau = (lax.dot_general(bb_re, ct_re, contract_p, precision=lax.Precision.HIGHEST,
                                 preferred_element_type=_f32)
                 - lax.dot_general(bb_im, ct_im, contract_p, precision=lax.Precision.HIGHEST,
                                   preferred_element_type=_f32))
        if tau == 0:
            r = lax.broadcasted_iota(jnp.int32, k_tau.shape, 0)
            c = lax.broadcasted_iota(jnp.int32, k_tau.shape, 1)
            k_tau = k_tau + jnp.where(r == c, d_ref[0], 0.0)
        for s0 in range(M_CHUNK - tau):
            tz_ref[0, s0 * BLOCK_LANES:(s0 + 1) * BLOCK_LANES,
                   (s0 + tau) * BLOCK_LANES:(s0 + tau + 1) * BLOCK_LANES] = k_tau.astype(_bf16)


def _setup(c, w_ada, b_ada, lam_re, lam_im, log_step, b_re, b_im, c_re, c_im, d_skip):
    nb = N_BLOCKS
    n_tile = N_MOD * D_MODEL // nb
    lstep = jnp.broadcast_to(log_step[:, None], (N_SSM_GROUPS, SSM_STATE))
    as_row = lambda a: a.reshape(nb, 1, BLOCK_STATE)
    b_t = lambda a: a.transpose(0, 2, 1).reshape(nb, BLOCK_LANES, SSM_STATE)
    c_t = lambda a: a.reshape(nb, BLOCK_LANES, SSM_STATE)
    tile_spec = lambda r, c: pl.BlockSpec((1, r, c), lambda j: (j, 0, 0))
    prep_in = [tile_spec(1, BLOCK_STATE)] * 3 + [tile_spec(BLOCK_LANES, SSM_STATE)] * 4 + [tile_spec(1, BLOCK_LANES)]
    assert len(prep_in) == N_PREP_IN
    return pl.pallas_call(
        _setup_kernel,
        grid=(nb,),
        in_specs=[
            pl.BlockSpec((BATCH, D_MODEL), lambda j: (0, 0)),
            pl.BlockSpec((D_MODEL, n_tile), lambda j: (0, j)),
            pl.BlockSpec((1, n_tile), lambda j: (0, j)),
        ] + prep_in,
        out_specs=[
            pl.BlockSpec((BATCH, n_tile), lambda j: (0, j)),
            pl.BlockSpec((2, SUBLANES, BLOCK_STATE), lambda j: (0, 0, j)),
            tile_spec(BLOCK_CHUNK, BLOCK_COLS),
            tile_spec(BLOCK_COLS, BLOCK_CHUNK),
            tile_spec(BLOCK_CHUNK, BLOCK_CHUNK),
        ],
        out_shape=(
            jax.ShapeDtypeStruct((BATCH, N_MOD * D_MODEL), _f32),
            jax.ShapeDtypeStruct((2, SUBLANES, N_STATE), _f32),
            jax.ShapeDtypeStruct((nb, BLOCK_CHUNK, BLOCK_COLS), _bf16),
            jax.ShapeDtypeStruct((nb, BLOCK_COLS, BLOCK_CHUNK), _bf16),
            jax.ShapeDtypeStruct((nb, BLOCK_CHUNK, BLOCK_CHUNK), _bf16),
        ),
        name="adaln_and_s5_setup",
    )(c, w_ada, b_ada.reshape(1, -1), as_row(lam_re), as_row(lam_im), as_row(lstep), b_t(b_re), b_t(b_im),
      c_t(c_re), c_t(c_im), d_skip.reshape(nb, 1, BLOCK_LANES))


def _premix_kernel(x_ref, mod_ref, g_ref, win32_ref, cw_ref, gconv_ref, ones_ref, wup32_ref, wout32_ref, wdn32_ref,
                   u_ref, yb_ref, wup_ref, wout_ref, wdn_ref, halo_ref, slab_ref, win_ref):
    ti = pl.program_id(1)
    step = pl.program_id(0) * (SEQ // T_PRE) + ti

    @pl.when(step == 0)
    def _():
        for k in range(D_IN_PROJ // D_CONV):
            cols = slice(k * D_CONV, (k + 1) * D_CONV)
            win_ref[:, cols] = win32_ref[:, cols].astype(_bf16)

    wup_ref[...] = wup32_ref[...].astype(_bf16)
    wout_ref[...] = wout32_ref[...].astype(_bf16)

    @pl.when(step < N_WDN_BLOCKS)
    def _():
        wdn_ref[...] = wdn32_ref[...].astype(_bf16)

    @pl.when(ti == 0)
    def _():
        halo_ref[...] = jnp.zeros(halo_ref.shape, _f32)

    x = x_ref[0]
    sh = mod_ref[0, 0:1, :]
    sc = mod_ref[0, 1:2, :]
    h = _rms(x, g_ref[...] * (1.0 + sc)) + sh
    hb = h.astype(_bf16)
    proj = jnp.dot(hb, win_ref[:, D_SSM:], preferred_element_type=_f32)
    u_ref[0] = jnp.dot(hb, win_ref[:, 0:D_SSM], preferred_element_type=_f32).astype(_bf16)
    bg = proj[:, 0:D_CONV]
    cg = proj[:, D_CONV:2 * D_CONV]
    v = proj[:, 2 * D_CONV:]
    cv = cg * v
    halo = halo_ref[...]
    halo_ref[...] = cv[T_PRE - SUBLANES:, :]
    yb = bg * _causal_conv3(cv, halo, cw_ref, slab_ref, 0)
    ms = jnp.dot((yb * yb).astype(_bf16), ones_ref[...], preferred_element_type=_f32)
    yb_ref[0] = (yb * lax.rsqrt(ms + EPS) * gconv_ref[...]).astype(_bf16)


def _premix(x, mod3, g_pre, w_in, conv_w, g_out_conv, ones_conv, w_up, w_out, w_down):
    n_t = SEQ // T_PRE
    step = lambda b, t: b * n_t + t
    up_rows = D_MODEL // N_PRE_STEPS
    by_step = lambda b, t: (step(b, t), 0)
    wdn_block = lambda b, t: (jnp.minimum(step(b, t), N_WDN_BLOCKS - 1), 0)
    return pl.pallas_call(
        _premix_kernel,
        grid=(BATCH, n_t),
        in_specs=[
            pl.BlockSpec((1, T_PRE, D_MODEL), lambda b, t: (b, t, 0)),
            pl.BlockSpec((1, N_MOD, D_MODEL), lambda b, t: (b, 0, 0)),
            _const_spec((1, D_MODEL)),
            _const_spec((D_MODEL, D_IN_PROJ)),
            _const_spec((3, D_CONV)),
            _const_spec((1, D_CONV)),
            _const_spec((D_CONV, D_CONV)),
            pl.BlockSpec((up_rows, 2 * D_FF), by_step),
            pl.BlockSpec((up_rows, D_MODEL), by_step),
            pl.BlockSpec((WDN_BLOCK_ROWS, D_MODEL), wdn_block),
        ],
        out_specs=[
            pl.BlockSpec((1, T_PRE, D_SSM), lambda b, t: (b, t, 0)),
            pl.BlockSpec((1, T_PRE, D_CONV), lambda b, t: (b, t, 0)),
            pl.BlockSpec((up_rows, 2 * D_FF), by_step),
            pl.BlockSpec((up_rows, D_MODEL), by_step),
            pl.BlockSpec((WDN_BLOCK_ROWS, D_MODEL), wdn_block),
        ],
        out_shape=[
            jax.ShapeDtypeStruct((BATCH, SEQ, D_SSM), _bf16),
            jax.ShapeDtypeStruct((BATCH, SEQ, D_CONV), _bf16),
            jax.ShapeDtypeStruct((D_MODEL, 2 * D_FF), _bf16),
            jax.ShapeDtypeStruct((D_MODEL, D_MODEL), _bf16),
            jax.ShapeDtypeStruct((D_FF, D_MODEL), _bf16),
        ],
        scratch_shapes=[
            pltpu.VMEM((SUBLANES, D_CONV), _f32),
            pltpu.VMEM((D_CONV // LANES, T_PRE + SUBLANES, LANES), _f32),
            pltpu.VMEM((D_MODEL, D_IN_PROJ), _bf16),
        ],
        compiler_params=pltpu.CompilerParams(
            dimension_semantics=("arbitrary", "arbitrary"), vmem_limit_bytes=VMEM_LIMIT),
        name="premix",
    )(x, mod3, g_pre, w_in, conv_w, g_out_conv, ones_conv, w_up, w_out, w_down)


def _ssm_kernel(u_ref, bs_ref, lam_ref, cs_ref, tz_ref, gluw_ref, glub_ref, g_ref, ones_ref,
                y_ref, st_ref, u32_ref, ufl_ref, s_ref, yfl_ref, y32_ref):
    i = pl.program_id(0)

    @pl.when(i == 0)
    def _():
        st_ref[...] = jnp.zeros(st_ref.shape, _f32)

    low_half = lax.broadcasted_iota(jnp.int32, (N_CHUNKS, LANES), 1) < BLOCK_LANES
    swap_halves = lambda v: pltpu.roll(v, BLOCK_LANES, axis=1)

    u = u_ref[...].reshape(BATCH * T_SSM, D_SSM)
    for j in range(N_LANE_TILES_U):
        u32_ref[j] = u[:, j * LANES:(j + 1) * LANES].astype(_f32)
    for j in range(N_LANE_TILES_U):
        for q in range(STEP_PAIRS):
            for b in range(BATCH):
                v0 = u32_ref[j, pl.ds(b * T_SSM + 2 * q, N_CHUNKS, stride=M_CHUNK), :]
                v1 = u32_ref[j, pl.ds(b * T_SSM + 2 * q + 1, N_CHUNKS, stride=M_CHUNK), :]
                dst = pl.ds(b, N_CHUNKS, stride=BATCH)
                ufl_ref[BLOCKS_PER_TILE * j, q, dst, :] = jnp.where(low_half, v0, swap_halves(v1))
                ufl_ref[BLOCKS_PER_TILE * j + 1, q, dst, :] = jnp.where(low_half, swap_halves(v0), v1)

    def chunk_lhs(blk):
        return jnp.concatenate([ufl_ref[blk, q] for q in range(STEP_PAIRS)], axis=-1).astype(_bf16)

    def bproj(j):
        for blk in range(BLOCKS_PER_TILE * j, BLOCKS_PER_TILE * (j + 1)):
            s_ref[:, blk * BLOCK_COLS:(blk + 1) * BLOCK_COLS] = jnp.dot(
                chunk_lhs(blk), bs_ref[blk], preferred_element_type=_f32)

    def scan(j):
        re, lam = [], []
        for blk in range(BLOCKS_PER_TILE * j, BLOCKS_PER_TILE * (j + 1)):
            for k in range(BLOCK_STATE // LANES):
                re.append(slice(blk * BLOCK_COLS + k * LANES, blk * BLOCK_COLS + (k + 1) * LANES))
                lam.append(slice(blk * BLOCK_STATE + k * LANES, blk * BLOCK_STATE + (k + 1) * LANES))
        im = [slice(r.start + BLOCK_STATE, r.stop + BLOCK_STATE) for r in re]
        n = len(re)
        lre = [lam_ref[0, :, ln] for ln in lam]
        lim = [lam_ref[1, :, ln] for ln in lam]
        xre = [st_ref[:, r] for r in re]
        xim = [st_ref[:, r] for r in im]
        for c in range(N_CHUNKS):
            rows = slice(c * BATCH, (c + 1) * BATCH)
            for k in range(n):
                loc_re, loc_im = s_ref[rows, re[k]], s_ref[rows, im[k]]
                s_ref[rows, re[k]] = xre[k]
                s_ref[rows, im[k]] = xim[k]
                xre[k], xim[k] = (lre[k] * xre[k] - lim[k] * xim[k] + loc_re,
                                  lre[k] * xim[k] + lim[k] * xre[k] + loc_im)
        for k in range(n):
            st_ref[:, re[k]] = xre[k]
            st_ref[:, im[k]] = xim[k]

    def cproj(j):
        for blk in range(BLOCKS_PER_TILE * j, BLOCKS_PER_TILE * (j + 1)):
            x_in = s_ref[:, blk * BLOCK_COLS:(blk + 1) * BLOCK_COLS].astype(_bf16)
            yf = jnp.dot(x_in, cs_ref[blk], preferred_element_type=_f32)
            yf = yf + jnp.dot(chunk_lhs(blk), tz_ref[blk], preferred_element_type=_f32)
            for q in range(STEP_PAIRS):
                yfl_ref[blk, q] = yf[:, q * LANES:(q + 1) * LANES]

    bproj(0)
    for j in range(N_LANE_TILES_U):
        if j + 1 < N_LANE_TILES_U:
            bproj(j + 1)
        if j > 0:
            cproj(j - 1)
        scan(j)
    cproj(N_LANE_TILES_U - 1)

    for j in range(N_LANE_TILES_U):
        for q in range(STEP_PAIRS):
            for b in range(BATCH):
                src = pl.ds(b, N_CHUNKS, stride=BATCH)
                lo = yfl_ref[BLOCKS_PER_TILE * j, q, src, :]
                hi = yfl_ref[BLOCKS_PER_TILE * j + 1, q, src, :]
                y32_ref[j, pl.ds(b * T_SSM + 2 * q, N_CHUNKS, stride=M_CHUNK), :] = (
                    jnp.where(low_half, lo, swap_halves(hi)))
                y32_ref[j, pl.ds(b * T_SSM + 2 * q + 1, N_CHUNKS, stride=M_CHUNK), :] = (
                    jnp.where(low_half, swap_halves(lo), hi))
    y = jnp.concatenate([y32_ref[j] for j in range(N_LANE_TILES_U)], axis=-1)
    z = _gelu_tanh(y).astype(_bf16)
    gate = jnp.dot(z, gluw_ref[...], preferred_element_type=_f32) + glub_ref[...]
    z = z * jax.nn.sigmoid(gate.astype(_bf16))
    ms = jnp.dot(z * z, ones_ref[...], preferred_element_type=_f32)
    out = z * lax.rsqrt(ms + EPS).astype(_bf16) * g_ref[...].astype(_bf16)
    y_ref[...] = out.reshape(BATCH, T_SSM, D_SSM)


def _ssm(u, bs_w, lam8, cs_w, tz_w, glu_w_b, glu_b, g_out_ssm, ones_ssm):
    rows = BATCH * T_SSM
    return pl.pallas_call(
        _ssm_kernel,
        grid=(SEQ // T_SSM,),
        in_specs=[
            pl.BlockSpec((BATCH, T_SSM, D_SSM), lambda i: (0, i, 0)),
            _const_spec((N_BLOCKS, BLOCK_CHUNK, BLOCK_COLS)),
            _const_spec((2, SUBLANES, N_STATE)),
            _const_spec((N_BLOCKS, BLOCK_COLS, BLOCK_CHUNK)),
            _const_spec((N_BLOCKS, BLOCK_CHUNK, BLOCK_CHUNK)),
            _const_spec((D_SSM, D_SSM)),
            _const_spec((1, D_SSM)),
            _const_spec((1, D_SSM)),
            _const_spec((D_SSM, D_SSM)),
        ],
        out_specs=pl.BlockSpec((BATCH, T_SSM, D_SSM), lambda i: (0, i, 0)),
        out_shape=jax.ShapeDtypeStruct((BATCH, SEQ, D_SSM), _bf16),
        scratch_shapes=[
            pltpu.VMEM((SUBLANES, N_BLOCKS * BLOCK_COLS), _f32),
            pltpu.VMEM((N_LANE_TILES_U, rows, LANES), _f32),
            pltpu.VMEM((N_BLOCKS, STEP_PAIRS, CHUNK_ROWS, LANES), _f32),
            pltpu.VMEM((CHUNK_ROWS, N_BLOCKS * BLOCK_COLS), _f32),
            pltpu.VMEM((N_BLOCKS, STEP_PAIRS, CHUNK_ROWS, LANES), _f32),
            pltpu.VMEM((N_LANE_TILES_U, rows, LANES), _f32),
        ],
        compiler_params=pltpu.CompilerParams(
            dimension_semantics=("arbitrary",), vmem_limit_bytes=VMEM_LIMIT),
        name="s5_mixer",
    )(u, bs_w, lam8, cs_w, tz_w, glu_w_b, glu_b, g_out_ssm, ones_ssm)


def _ffn_kernel(x_ref, ya_ref, yb_ref, mod_ref, wout_ref, gpm_ref, gpf_ref, wup_ref, fcw_ref, wdn_ref, gpo_ref,
                o_ref, halo_ref, act_ref, slab_ref):
    ti = pl.program_id(1)

    @pl.when(ti == 0)
    def _():
        halo_ref[...] = jnp.zeros(halo_ref.shape, _f32)

    x = x_ref[0]
    gt1 = mod_ref[0, 2:3, :]
    sh2 = mod_ref[0, 3:4, :]
    sc2 = mod_ref[0, 4:5, :]
    gt2 = mod_ref[0, 5:6, :]
    mix = jnp.dot(jnp.concatenate([ya_ref[0], yb_ref[0]], axis=-1), wout_ref[...], preferred_element_type=_f32)
    x1 = x + _rms(mix, gpm_ref[...] * gt1)
    h2 = (_rms(x1, gpf_ref[...] * (1.0 + sc2)) + sh2).astype(_bf16)

    def up_chunk(c):
        cols_a = slice(c * FF_CHUNK, (c + 1) * FF_CHUNK)
        cols_v = slice(D_FF + c * FF_CHUNK, D_FF + (c + 1) * FF_CHUNK)
        return (jnp.dot(h2, wup_ref[:, cols_a], preferred_element_type=_f32), cols_a,
                jnp.dot(h2, wup_ref[:, cols_v], preferred_element_type=_f32), cols_v)

    def conv(hc, cols, slab0):
        halo = halo_ref[:, cols]
        halo_ref[:, cols] = hc[T_FFN - SUBLANES:, :]
        return _causal_conv3(hc, halo, fcw_ref[:, cols], slab_ref, slab0)

    nxt = up_chunk(0)
    for c in range(N_FF_CHUNKS):
        ha, cols_a, hv, cols_v = nxt
        if c + 1 < N_FF_CHUNKS:
            nxt = up_chunk(c + 1)
        slab0 = (c % 2) * FFN_SLABS_PER_CHUNK
        a = conv(ha, cols_a, slab0)
        v = conv(hv, cols_v, slab0 + FF_CHUNK // LANES)
        act_ref[:, cols_a] = (a * jax.nn.sigmoid(a) * v).astype(_bf16)
    down = jnp.dot(act_ref[...], wdn_ref[...], preferred_element_type=_f32)
    o_ref[0] = x1 + _rms(down, gpo_ref[...] * gt2)


def _ffn(x, ya, yb, mod3, w_out_b, g_post_mix, g_pre_ffn, w_up_b, ffn_cw, w_down_b, g_post_ffn):
    n_t = SEQ // T_FFN
    return pl.pallas_call(
        _ffn_kernel,
        grid=(BATCH, n_t),
        in_specs=[
            pl.BlockSpec((1, T_FFN, D_MODEL), lambda b, t: (b, t, 0)),
            pl.BlockSpec((1, T_FFN, D_SSM), lambda b, t: (b, t, 0)),
            pl.BlockSpec((1, T_FFN, D_CONV), lambda b, t: (b, t, 0)),
            pl.BlockSpec((1, N_MOD, D_MODEL), lambda b, t: (b, 0, 0)),
            _const_spec((D_MODEL, D_MODEL)),
            _const_spec((1, D_MODEL)),
            _const_spec((1, D_MODEL)),
            _const_spec((D_MODEL, 2 * D_FF)),
            _const_spec((3, 2 * D_FF)),
            _const_spec((D_FF, D_MODEL)),
            _const_spec((1, D_MODEL)),
        ],
        out_specs=pl.BlockSpec((1, T_FFN, D_MODEL), lambda b, t: (b, t, 0)),
        out_shape=jax.ShapeDtypeStruct((BATCH, SEQ, D_MODEL), _f32),
        scratch_shapes=[
            pltpu.VMEM((SUBLANES, 2 * D_FF), _f32),
            pltpu.VMEM((T_FFN, D_FF), _bf16),
            pltpu.VMEM((2 * FFN_SLABS_PER_CHUNK, T_FFN + SUBLANES, LANES), _f32),
        ],
        compiler_params=pltpu.CompilerParams(
            dimension_semantics=("arbitrary", "arbitrary"), vmem_limit_bytes=VMEM_LIMIT),
        name="outproj_convffn",
    )(x, ya, yb, mod3, w_out_b, g_post_mix, g_pre_ffn, w_up_b, ffn_cw, w_down_b, g_post_ffn)


def _head_mean_matrix(width, head):
    idx = jnp.arange(width) // head
    return jnp.where(idx[:, None] == idx[None, :], 1.0 / head, 0.0).astype(_bf16)


def kernel(x, c, w_ada, b_ada, g_pre_mix, g_post_mix, w_in, ssm_lam_re, ssm_lam_im, ssm_log_step, ssm_b_re, ssm_b_im, ssm_c_re, ssm_c_im, ssm_d, glu_w, glu_b, g_out_ssm, conv_w, g_out_conv, w_out, g_pre_ffn, g_post_ffn, w_up, ffn_conv_w, w_down):
    assert x.shape == (BATCH, SEQ, D_MODEL) and w_ada.shape[0] == 1
    row = lambda a: a.reshape(1, -1)

    mod, lam8, bs_w, cs_w, tz_w = _setup(
        c, w_ada[0], b_ada[0], ssm_lam_re[0], ssm_lam_im[0], ssm_log_step[0], ssm_b_re[0], ssm_b_im[0],
        ssm_c_re[0], ssm_c_im[0], ssm_d[0])
    mod3 = mod.reshape(BATCH, N_MOD, D_MODEL)

    u, yb, w_up_b, w_out_b, w_down_b = _premix(
        x, mod3, row(g_pre_mix[0]), w_in[0], conv_w[0], row(g_out_conv[0]),
        _head_mean_matrix(D_CONV, D_CONV // CONV_HEADS), w_up[0], w_out[0], w_down[0])

    ya = _ssm(u, bs_w, lam8, cs_w, tz_w, glu_w[0].astype(_bf16),
              row(glu_b[0]), row(g_out_ssm[0]), _head_mean_matrix(D_SSM, SSM_GROUP))

    return _ffn(x, ya, yb, mod3, w_out_b, row(g_post_mix[0]), row(g_pre_ffn[0]),
                w_up_b, ffn_conv_w[0], w_down_b, row(g_post_ffn[0]))
```

```python
import math

import jax
import jax.numpy as jnp
import numpy as np
from jax import lax
from jax.experimental import pallas as pl
from jax.experimental.pallas import tpu as pltpu

D_MODEL = 1024
BATCH = 8
SEQ = 4096
D_SSM = 512
D_CONV = 512
SSM_GROUP = 16
N_SSM_GROUPS = 32
SSM_STATE = 64
CONV_HEADS = 8
D_FF = 2816
N_MOD = 6
D_IN_PROJ = 2048
EPS = 1e-6
LAMBDA_RE_MAX = -1e-4

N_STATE = N_SSM_GROUPS * SSM_STATE
SUBLANES = 8
LANES = 128
GROUPS_PER_LANE_TILE = LANES // SSM_GROUP
N_LANE_TILES_U = D_SSM // LANES

T_PRE = 1024
N_PRE_STEPS = BATCH * (SEQ // T_PRE)
WDN_BLOCK_ROWS = 176
N_WDN_BLOCKS = D_FF // WDN_BLOCK_ROWS
T_SSM = 256
T_FFN = 1024
FF_CHUNK = 256
N_FF_CHUNKS = D_FF // FF_CHUNK
FFN_SLABS_PER_CHUNK = 2 * FF_CHUNK // LANES
M_CHUNK = 4
N_CHUNKS = T_SSM // M_CHUNK
CHUNK_ROWS = N_CHUNKS * BATCH
GROUPS_PER_BLOCK = GROUPS_PER_LANE_TILE // 2
BLOCKS_PER_TILE = GROUPS_PER_LANE_TILE // GROUPS_PER_BLOCK
N_BLOCKS = N_SSM_GROUPS // GROUPS_PER_BLOCK
BLOCK_LANES = GROUPS_PER_BLOCK * SSM_GROUP
BLOCK_STATE = GROUPS_PER_BLOCK * SSM_STATE
BLOCK_CHUNK = M_CHUNK * BLOCK_LANES
BLOCK_COLS = 2 * BLOCK_STATE
N_PREP_IN = 8
STEP_PAIRS = M_CHUNK // 2
VMEM_LIMIT = 60 * 1024 * 1024

_f32 = jnp.float32
_bf16 = jnp.bfloat16


def _const_spec(shape):
    nd = len(shape)
    return pl.BlockSpec(shape, lambda *_: (0,) * nd, pipeline_mode=pl.Buffered(1))


def _rms(x, g):
    ms = jnp.mean(x * x, axis=-1, keepdims=True)
    return x * lax.rsqrt(ms + EPS) * g


def _gelu_tanh(y):
    k = math.sqrt(2.0 / math.pi)
    half_y = 0.5 * y
    return half_y + half_y * jnp.tanh(y * (k + (k * 0.044715) * (y * y)))


def _causal_conv3(h, halo, w, slab_ref, slab0):
    rows = h.shape[0]
    outs = []
    for k in range(h.shape[1] // LANES):
        lanes = slice(k * LANES, (k + 1) * LANES)
        slab_ref[slab0 + k, 0:SUBLANES, :] = halo[:, lanes]
        slab_ref[slab0 + k, SUBLANES:SUBLANES + rows, :] = h[:, lanes]
        h1 = slab_ref[slab0 + k, pl.ds(SUBLANES - 1, rows, stride=1), :]
        h2 = slab_ref[slab0 + k, pl.ds(SUBLANES - 2, rows, stride=1), :]
        outs.append(w[0:1, lanes] * h2 + w[1:2, lanes] * h1 + w[2:3, lanes] * h[:, lanes])
    return jnp.concatenate(outs, axis=-1)


def _mod_kernel(c_ref, w_ref, b_ref, o_ref):
    c = c_ref[...]
    c_act = c * jax.nn.sigmoid(c)
    o_ref[...] = jnp.dot(c_act, w_ref[...], preferred_element_type=_f32) + b_ref[...]


def _setup_kernel(c_ref, w_ref, b_ref, *refs):
    mod_ref, prep_refs = refs[N_PREP_IN], refs[:N_PREP_IN] + refs[N_PREP_IN + 1:]
    _mod_kernel(c_ref, w_ref, b_ref, mod_ref)
    _prep_kernel(*prep_refs)


def _prep_kernel(lre_ref, lim_ref, lst_ref, bre_ref, bim_ref, cre_ref, cim_ref, d_ref,
                 lam8_ref, bs_ref, cs_ref, tz_ref):
    lre = jnp.minimum(lre_ref[0], LAMBDA_RE_MAX)
    lim = lim_ref[0]
    step = jnp.exp(lst_ref[0])
    log_mag, ang = lre * step, lim * step

    def lam_pow(k):
        mag = jnp.exp(k * log_mag)
        return mag * jnp.cos(k * ang), mag * jnp.sin(k * ang)

    pw = [lam_pow(float(k)) for k in range(M_CHUNK + 1)]

    def block_diag(blk):
        wide = jnp.concatenate([blk] * GROUPS_PER_BLOCK, axis=-1)
        r = lax.broadcasted_iota(jnp.int32, wide.shape, 0) // SSM_GROUP
        c = lax.broadcasted_iota(jnp.int32, wide.shape, 1) // SSM_STATE
        return jnp.where(r == c, wide, 0.0)

    a_re, a_im = pw[1]
    n_re = a_re - 1.0
    den = lre * lre + lim * lim
    q_re = (n_re * lre + a_im * lim) / den
    q_im = (a_im * lre - n_re * lim) / den
    b_re, b_im = block_diag(bre_ref[0]), block_diag(bim_ref[0])
    bb_re = q_re * b_re - q_im * b_im
    bb_im = q_re * b_im + q_im * b_re
    c_re, c_im = block_diag(cre_ref[0]), block_diag(cim_ref[0])

    lam8_ref[0] = jnp.broadcast_to(pw[M_CHUNK][0], (SUBLANES, BLOCK_STATE))
    lam8_ref[1] = jnp.broadcast_to(pw[M_CHUNK][1], (SUBLANES, BLOCK_STATE))
    for s in range(M_CHUNK):
        rows = slice(s * BLOCK_LANES, (s + 1) * BLOCK_LANES)
        p_re, p_im = pw[M_CHUNK - 1 - s]
        bs_ref[0, rows, 0:BLOCK_STATE] = (p_re * bb_re - p_im * bb_im).astype(_bf16)
        bs_ref[0, rows, BLOCK_STATE:] = (p_re * bb_im + p_im * bb_re).astype(_bf16)
        p_re, p_im = pw[s + 1]
        cs_ref[0, 0:BLOCK_STATE, rows] = (p_re * c_re - p_im * c_im).T.astype(_bf16)
        cs_ref[0, BLOCK_STATE:, rows] = (-(p_re * c_im + p_im * c_re)).T.astype(_bf16)
    tz_ref[...] = jnp.zeros(tz_ref.shape, _bf16)
    contract_p = (((1,), (1,)), ((), ()))
    for tau in range(M_CHUNK):
        p_re, p_im = pw[tau]
        ct_re = p_re * c_re - p_im * c_im
        ct_im = p_re * c_im + p_im * c_re
        k_tau = (lax.dot_general(bb_re, ct_re, contract_p, precision=lax.Precision.HIGHEST,
                                 preferred_element_type=_f32)
                 - lax.dot_general(bb_im, ct_im, contract_p, precision=lax.Precision.HIGHEST,
                                   preferred_element_type=_f32))
        if tau == 0:
            r = lax.broadcasted_iota(jnp.int32, k_tau.shape, 0)
            c = lax.broadcasted_iota(jnp.int32, k_tau.shape, 1)
            k_tau = k_tau + jnp.where(r == c, d_ref[0], 0.0)
        for s0 in range(M_CHUNK - tau):
            tz_ref[0, s0 * BLOCK_LANES:(s0 + 1) * BLOCK_LANES,
                   (s0 + tau) * BLOCK_LANES:(s0 + tau + 1) * BLOCK_LANES] = k_tau.astype(_bf16)


def _setup(c, w_ada, b_ada, lam_re, lam_im, log_step, b_re, b_im, c_re, c_im, d_skip):
    nb = N_BLOCKS
    n_tile = N_MOD * D_MODEL // nb
    lstep = jnp.broadcast_to(log_step[:, None], (N_SSM_GROUPS, SSM_STATE))
    as_row = lambda a: a.reshape(nb, 1, BLOCK_STATE)
    b_t = lambda a: a.transpose(0, 2, 1).reshape(nb, BLOCK_LANES, SSM_STATE)
    c_t = lambda a: a.reshape(nb, BLOCK_LANES, SSM_STATE)
    tile_spec = lambda r, c: pl.BlockSpec((1, r, c), lambda j: (j, 0, 0))
    prep_in = [tile_spec(1, BLOCK_STATE)] * 3 + [tile_spec(BLOCK_LANES, SSM_STATE)] * 4 + [tile_spec(1, BLOCK_LANES)]
    assert len(prep_in) == N_PREP_IN
    return pl.pallas_call(
        _setup_kernel,
        grid=(nb,),
        in_specs=[
            pl.BlockSpec((BATCH, D_MODEL), lambda j: (0, 0)),
            pl.BlockSpec((D_MODEL, n_tile), lambda j: (0, j)),
            pl.BlockSpec((1, n_tile), lambda j: (0, j)),
        ] + prep_in,
        out_specs=[
            pl.BlockSpec((BATCH, n_tile), lambda j: (0, j)),
            pl.BlockSpec((2, SUBLANES, BLOCK_STATE), lambda j: (0, 0, j)),
            tile_spec(BLOCK_CHUNK, BLOCK_COLS),
            tile_spec(BLOCK_COLS, BLOCK_CHUNK),
            tile_spec(BLOCK_CHUNK, BLOCK_CHUNK),
        ],
        out_shape=(
            jax.ShapeDtypeStruct((BATCH, N_MOD * D_MODEL), _f32),
            jax.ShapeDtypeStruct((2, SUBLANES, N_STATE), _f32),
            jax.ShapeDtypeStruct((nb, BLOCK_CHUNK, BLOCK_COLS), _bf16),
            jax.ShapeDtypeStruct((nb, BLOCK_COLS, BLOCK_CHUNK), _bf16),
            jax.ShapeDtypeStruct((nb, BLOCK_CHUNK, BLOCK_CHUNK), _bf16),
        ),
        name="adaln_and_s5_setup",
    )(c, w_ada, b_ada.reshape(1, -1), as_row(lam_re), as_row(lam_im), as_row(lstep), b_t(b_re), b_t(b_im),
      c_t(c_re), c_t(c_im), d_skip.reshape(nb, 1, BLOCK_LANES))


def _premix_kernel(x_ref, mod_ref, g_ref, win32_ref, cw_ref, gconv_ref, ones_ref, wup32_ref, wout32_ref, glu32_ref,
                   wdn32_ref, u_ref, yb_ref, wup_ref, wout_ref, glu_ref, wdn_ref, halo_ref, slab_ref, win_ref):
    ti = pl.program_id(1)
    step = pl.program_id(0) * (SEQ // T_PRE) + ti

    @pl.when(step == 0)
    def _():
        for k in range(D_IN_PROJ // D_CONV):
            cols = slice(k * D_CONV, (k + 1) * D_CONV)
            win_ref[:, cols] = win32_ref[:, cols].astype(_bf16)

    wup_ref[...] = wup32_ref[...].astype(_bf16)
    wout_ref[...] = wout32_ref[...].astype(_bf16)
    glu_ref[...] = glu32_ref[...].astype(_bf16)

    @pl.when(step < N_WDN_BLOCKS)
    def _():
        wdn_ref[...] = wdn32_ref[...].astype(_bf16)

    @pl.when(ti == 0)
    def _():
        halo_ref[...] = jnp.zeros(halo_ref.shape, _f32)

    x = x_ref[0]
    sh = mod_ref[0, 0:1, :]
    sc = mod_ref[0, 1:2, :]
    h = _rms(x, g_ref[...] * (1.0 + sc)) + sh
    hb = h.astype(_bf16)
    proj = jnp.dot(hb, win_ref[:, D_SSM:], preferred_element_type=_f32)
    u_ref[0] = jnp.dot(hb, win_ref[:, 0:D_SSM], preferred_element_type=_f32).astype(_bf16)
    bg = proj[:, 0:D_CONV]
    cg = proj[:, D_CONV:2 * D_CONV]
    v = proj[:, 2 * D_CONV:]
    cv = cg * v
    halo = halo_ref[...]
    halo_ref[...] = cv[T_PRE - SUBLANES:, :]
    yb = bg * _causal_conv3(cv, halo, cw_ref, slab_ref, 0)
    ms = jnp.dot((yb * yb).astype(_bf16), ones_ref[...], preferred_element_type=_f32)
    yb_ref[0] = (yb * lax.rsqrt(ms + EPS) * gconv_ref[...]).astype(_bf16)


def _premix(x, mod3, g_pre, w_in, conv_w, g_out_conv, ones_conv, w_up, w_out, glu_w, w_down):
    n_t = SEQ // T_PRE
    step = lambda b, t: b * n_t + t
    up_rows = D_MODEL // N_PRE_STEPS
    glu_rows = D_SSM // N_PRE_STEPS
    by_step = lambda b, t: (step(b, t), 0)
    wdn_block = lambda b, t: (jnp.minimum(step(b, t), N_WDN_BLOCKS - 1), 0)
    return pl.pallas_call(
        _premix_kernel,
        grid=(BATCH, n_t),
        in_specs=[
            pl.BlockSpec((1, T_PRE, D_MODEL), lambda b, t: (b, t, 0)),
            pl.BlockSpec((1, N_MOD, D_MODEL), lambda b, t: (b, 0, 0)),
            _const_spec((1, D_MODEL)),
            _const_spec((D_MODEL, D_IN_PROJ)),
            _const_spec((3, D_CONV)),
            _const_spec((1, D_CONV)),
            _const_spec((D_CONV, D_CONV)),
            pl.BlockSpec((up_rows, 2 * D_FF), by_step),
            pl.BlockSpec((up_rows, D_MODEL), by_step),
            pl.BlockSpec((glu_rows, D_SSM), by_step),
            pl.BlockSpec((WDN_BLOCK_ROWS, D_MODEL), wdn_block),
        ],
        out_specs=[
            pl.BlockSpec((1, T_PRE, D_SSM), lambda b, t: (b, t, 0)),
            pl.BlockSpec((1, T_PRE, D_CONV), lambda b, t: (b, t, 0)),
            pl.BlockSpec((up_rows, 2 * D_FF), by_step),
            pl.BlockSpec((up_rows, D_MODEL), by_step),
            pl.BlockSpec((glu_rows, D_SSM), by_step),
            pl.BlockSpec((WDN_BLOCK_ROWS, D_MODEL), wdn_block),
        ],
        out_shape=[
            jax.ShapeDtypeStruct((BATCH, SEQ, D_SSM), _bf16),
            jax.ShapeDtypeStruct((BATCH, SEQ, D_CONV), _bf16),
            jax.ShapeDtypeStruct((D_MODEL, 2 * D_FF), _bf16),
            jax.ShapeDtypeStruct((D_MODEL, D_MODEL), _bf16),
            jax.ShapeDtypeStruct((D_SSM, D_SSM), _bf16),
            jax.ShapeDtypeStruct((D_FF, D_MODEL), _bf16),
        ],
        scratch_shapes=[
            pltpu.VMEM((SUBLANES, D_CONV), _f32),
            pltpu.VMEM((D_CONV // LANES, T_PRE + SUBLANES, LANES), _f32),
            pltpu.VMEM((D_MODEL, D_IN_PROJ), _bf16),
        ],
        compiler_params=pltpu.CompilerParams(
            dimension_semantics=("arbitrary", "arbitrary"), vmem_limit_bytes=VMEM_LIMIT),
        name="premix",
    )(x, mod3, g_pre, w_in, conv_w, g_out_conv, ones_conv, w_up, w_out, glu_w, w_down)


def _ssm_kernel(u_ref, bs_ref, lam_ref, cs_ref, tz_ref, gluw_ref, glub_ref, g_ref, ones_ref,
                y_ref, st_ref, u32_ref, ufl_ref, s_ref, yfl_ref, y32_ref):
    i = pl.program_id(0)

    @pl.when(i == 0)
    def _():
        st_ref[...] = jnp.zeros(st_ref.shape, _f32)

    low_half = lax.broadcasted_iota(jnp.int32, (N_CHUNKS, LANES), 1) < BLOCK_LANES
    swap_halves = lambda v: pltpu.roll(v, BLOCK_LANES, axis=1)

    u = u_ref[...].reshape(BATCH * T_SSM, D_SSM)
    for j in range(N_LANE_TILES_U):
        u32_ref[j] = u[:, j * LANES:(j + 1) * LANES].astype(_f32)
    for j in range(N_LANE_TILES_U):
        for q in range(STEP_PAIRS):
            for b in range(BATCH):
                v0 = u32_ref[j, pl.ds(b * T_SSM + 2 * q, N_CHUNKS, stride=M_CHUNK), :]
                v1 = u32_ref[j, pl.ds(b * T_SSM + 2 * q + 1, N_CHUNKS, stride=M_CHUNK), :]
                dst = pl.ds(b, N_CHUNKS, stride=BATCH)
                ufl_ref[BLOCKS_PER_TILE * j, q, dst, :] = jnp.where(low_half, v0, swap_halves(v1))
                ufl_ref[BLOCKS_PER_TILE * j + 1, q, dst, :] = jnp.where(low_half, swap_halves(v0), v1)

    def chunk_lhs(blk):
        return jnp.concatenate([ufl_ref[blk, q] for q in range(STEP_PAIRS)], axis=-1).astype(_bf16)

    def bproj(j):
        for blk in range(BLOCKS_PER_TILE * j, BLOCKS_PER_TILE * (j + 1)):
            s_ref[:, blk * BLOCK_COLS:(blk + 1) * BLOCK_COLS] = jnp.dot(
                chunk_lhs(blk), bs_ref[blk], preferred_element_type=_f32)

    def scan(j):
        re, lam = [], []
        for blk in range(BLOCKS_PER_TILE * j, BLOCKS_PER_TILE * (j + 1)):
            for k in range(BLOCK_STATE // LANES):
                re.append(slice(blk * BLOCK_COLS + k * LANES, blk * BLOCK_COLS + (k + 1) * LANES))
                lam.append(slice(blk * BLOCK_STATE + k * LANES, blk * BLOCK_STATE + (k + 1) * LANES))
        im = [slice(r.start + BLOCK_STATE, r.stop + BLOCK_STATE) for r in re]
        n = len(re)
        lre = [lam_ref[0, :, ln] for ln in lam]
        lim = [lam_ref[1, :, ln] for ln in lam]
        xre = [st_ref[:, r] for r in re]
        xim = [st_ref[:, r] for r in im]
        for c in range(N_CHUNKS):
            rows = slice(c * BATCH, (c + 1) * BATCH)
            for k in range(n):
                loc_re, loc_im = s_ref[rows, re[k]], s_ref[rows, im[k]]
                s_ref[rows, re[k]] = xre[k]
                s_ref[rows, im[k]] = xim[k]
                xre[k], xim[k] = (lre[k] * xre[k] - lim[k] * xim[k] + loc_re,
                                  lre[k] * xim[k] + lim[k] * xre[k] + loc_im)
        for k in range(n):
            st_ref[:, re[k]] = xre[k]
            st_ref[:, im[k]] = xim[k]

    def cproj(j):
        for blk in range(BLOCKS_PER_TILE * j, BLOCKS_PER_TILE * (j + 1)):
            x_in = s_ref[:, blk * BLOCK_COLS:(blk + 1) * BLOCK_COLS].astype(_bf16)
            yf = jnp.dot(x_in, cs_ref[blk], preferred_element_type=_f32)
            yf = yf + jnp.dot(chunk_lhs(blk), tz_ref[blk], preferred_element_type=_f32)
            for q in range(STEP_PAIRS):
                yfl_ref[blk, q] = yf[:, q * LANES:(q + 1) * LANES]

    bproj(0)
    for j in range(N_LANE_TILES_U):
        if j + 1 < N_LANE_TILES_U:
            bproj(j + 1)
        if j > 0:
            cproj(j - 1)
        scan(j)
    cproj(N_LANE_TILES_U - 1)

    for j in range(N_LANE_TILES_U):
        for q in range(STEP_PAIRS):
            for b in range(BATCH):
                src = pl.ds(b, N_CHUNKS, stride=BATCH)
                lo = yfl_ref[BLOCKS_PER_TILE * j, q, src, :]
                hi = yfl_ref[BLOCKS_PER_TILE * j + 1, q, src, :]
                y32_ref[j, pl.ds(b * T_SSM + 2 * q, N_CHUNKS, stride=M_CHUNK), :] = (
                    jnp.where(low_half, lo, swap_halves(hi)))
                y32_ref[j, pl.ds(b * T_SSM + 2 * q + 1, N_CHUNKS, stride=M_CHUNK), :] = (
                    jnp.where(low_half, swap_halves(lo), hi))
    y = jnp.concatenate([y32_ref[j] for j in range(N_LANE_TILES_U)], axis=-1)
    z = _gelu_tanh(y).astype(_bf16)
    gate = jnp.dot(z, gluw_ref[...], preferred_element_type=_f32) + glub_ref[...]
    z = z * jax.nn.sigmoid(gate.astype(_bf16))
    ms = jnp.dot(z * z, ones_ref[...], preferred_element_type=_f32)
    out = z * lax.rsqrt(ms + EPS).astype(_bf16) * g_ref[...].astype(_bf16)
    y_ref[...] = out.reshape(BATCH, T_SSM, D_SSM)


def _ssm(u, bs_w, lam8, cs_w, tz_w, glu_w_b, glu_b, g_out_ssm, ones_ssm):
    rows = BATCH * T_SSM
    return pl.pallas_call(
        _ssm_kernel,
        grid=(SEQ // T_SSM,),
        in_specs=[
            pl.BlockSpec((BATCH, T_SSM, D_SSM), lambda i: (0, i, 0)),
            _const_spec((N_BLOCKS, BLOCK_CHUNK, BLOCK_COLS)),
            _const_spec((2, SUBLANES, N_STATE)),
            _const_spec((N_BLOCKS, BLOCK_COLS, BLOCK_CHUNK)),
            _const_spec((N_BLOCKS, BLOCK_CHUNK, BLOCK_CHUNK)),
            _const_spec((D_SSM, D_SSM)),
            _const_spec((1, D_SSM)),
            _const_spec((1, D_SSM)),
            _const_spec((D_SSM, D_SSM)),
        ],
        out_specs=pl.BlockSpec((BATCH, T_SSM, D_SSM), lambda i: (0, i, 0)),
        out_shape=jax.ShapeDtypeStruct((BATCH, SEQ, D_SSM), _bf16),
        scratch_shapes=[
            pltpu.VMEM((SUBLANES, N_BLOCKS * BLOCK_COLS), _f32),
            pltpu.VMEM((N_LANE_TILES_U, rows, LANES), _f32),
            pltpu.VMEM((N_BLOCKS, STEP_PAIRS, CHUNK_ROWS, LANES), _f32),
            pltpu.VMEM((CHUNK_ROWS, N_BLOCKS * BLOCK_COLS), _f32),
            pltpu.VMEM((N_BLOCKS, STEP_PAIRS, CHUNK_ROWS, LANES), _f32),
            pltpu.VMEM((N_LANE_TILES_U, rows, LANES), _f32),
        ],
        compiler_params=pltpu.CompilerParams(
            dimension_semantics=("arbitrary",), vmem_limit_bytes=VMEM_LIMIT),
        name="s5_mixer",
    )(u, bs_w, lam8, cs_w, tz_w, glu_w_b, glu_b, g_out_ssm, ones_ssm)


def _ffn_kernel(x_ref, ya_ref, yb_ref, mod_ref, wout_ref, gpm_ref, gpf_ref, wup_ref, fcw_ref, wdn_ref, gpo_ref,
                o_ref, halo_ref, act_ref, slab_ref):
    ti = pl.program_id(1)

    @pl.when(ti == 0)
    def _():
        halo_ref[...] = jnp.zeros(halo_ref.shape, _f32)

    x = x_ref[0]
    gt1 = mod_ref[0, 2:3, :]
    sh2 = mod_ref[0, 3:4, :]
    sc2 = mod_ref[0, 4:5, :]
    gt2 = mod_ref[0, 5:6, :]
    mix = jnp.dot(jnp.concatenate([ya_ref[0], yb_ref[0]], axis=-1), wout_ref[...], preferred_element_type=_f32)
    x1 = x + _rms(mix, gpm_ref[...] * gt1)
    h2 = (_rms(x1, gpf_ref[...] * (1.0 + sc2)) + sh2).astype(_bf16)

    def up_chunk(c):
        cols_a = slice(c * FF_CHUNK, (c + 1) * FF_CHUNK)
        cols_v = slice(D_FF + c * FF_CHUNK, D_FF + (c + 1) * FF_CHUNK)
        return (jnp.dot(h2, wup_ref[:, cols_a], preferred_element_type=_f32), cols_a,
                jnp.dot(h2, wup_ref[:, cols_v], preferred_element_type=_f32), cols_v)

    def conv(hc, cols, slab0):
        halo = halo_ref[:, cols]
        halo_ref[:, cols] = hc[T_FFN - SUBLANES:, :]
        return _causal_conv3(hc, halo, fcw_ref[:, cols], slab_ref, slab0)

    nxt = up_chunk(0)
    for c in range(N_FF_CHUNKS):
        ha, cols_a, hv, cols_v = nxt
        if c + 1 < N_FF_CHUNKS:
            nxt = up_chunk(c + 1)
        slab0 = (c % 2) * FFN_SLABS_PER_CHUNK
        a = conv(ha, cols_a, slab0)
        v = conv(hv, cols_v, slab0 + FF_CHUNK // LANES)
        act_ref[:, cols_a] = (a * jax.nn.sigmoid(a) * v).astype(_bf16)
    down = jnp.dot(act_ref[...], wdn_ref[...], preferred_element_type=_f32)
    o_ref[0] = x1 + _rms(down, gpo_ref[...] * gt2)


def _ffn(x, ya, yb, mod3, w_out_b, g_post_mix, g_pre_ffn, w_up_b, ffn_cw, w_down_b, g_post_ffn):
    n_t = SEQ // T_FFN
    return pl.pallas_call(
        _ffn_kernel,
        grid=(BATCH, n_t),
        in_specs=[
            pl.BlockSpec((1, T_FFN, D_MODEL), lambda b, t: (b, t, 0)),
            pl.BlockSpec((1, T_FFN, D_SSM), lambda b, t: (b, t, 0)),
            pl.BlockSpec((1, T_FFN, D_CONV), lambda b, t: (b, t, 0)),
            pl.BlockSpec((1, N_MOD, D_MODEL), lambda b, t: (b, 0, 0)),
            _const_spec((D_MODEL, D_MODEL)),
            _const_spec((1, D_MODEL)),
            _const_spec((1, D_MODEL)),
            _const_spec((D_MODEL, 2 * D_FF)),
            _const_spec((3, 2 * D_FF)),
            _const_spec((D_FF, D_MODEL)),
            _const_spec((1, D_MODEL)),
        ],
        out_specs=pl.BlockSpec((1, T_FFN, D_MODEL), lambda b, t: (b, t, 0)),
        out_shape=jax.ShapeDtypeStruct((BATCH, SEQ, D_MODEL), _f32),
        scratch_shapes=[
            pltpu.VMEM((SUBLANES, 2 * D_FF), _f32),
            pltpu.VMEM((T_FFN, D_FF), _bf16),
            pltpu.VMEM((2 * FFN_SLABS_PER_CHUNK, T_FFN + SUBLANES, LANES), _f32),
        ],
        compiler_params=pltpu.CompilerParams(
            dimension_semantics=("arbitrary", "arbitrary"), vmem_limit_bytes=VMEM_LIMIT),
        name="outproj_convffn",
    )(x, ya, yb, mod3, w_out_b, g_post_mix, g_pre_ffn, w_up_b, ffn_cw, w_down_b, g_post_ffn)


def _head_mean_matrix(width, head):
    idx = np.arange(width) // head
    return jnp.asarray(np.where(idx[:, None] == idx[None, :], 1.0 / head, 0.0), dtype=_bf16)


def kernel(x, c, w_ada, b_ada, g_pre_mix, g_post_mix, w_in, ssm_lam_re, ssm_lam_im, ssm_log_step, ssm_b_re, ssm_b_im, ssm_c_re, ssm_c_im, ssm_d, glu_w, glu_b, g_out_ssm, conv_w, g_out_conv, w_out, g_pre_ffn, g_post_ffn, w_up, ffn_conv_w, w_down):
    assert x.shape == (BATCH, SEQ, D_MODEL) and w_ada.shape[0] == 1
    row = lambda a: a.reshape(1, -1)

    mod, lam8, bs_w, cs_w, tz_w = _setup(
        c, w_ada[0], b_ada[0], ssm_lam_re[0], ssm_lam_im[0], ssm_log_step[0], ssm_b_re[0], ssm_b_im[0],
        ssm_c_re[0], ssm_c_im[0], ssm_d[0])
    mod3 = mod.reshape(BATCH, N_MOD, D_MODEL)

    u, yb, w_up_b, w_out_b, glu_w_b, w_down_b = _premix(
        x, mod3, row(g_pre_mix[0]), w_in[0], conv_w[0], row(g_out_conv[0]),
        _head_mean_matrix(D_CONV, D_CONV // CONV_HEADS), w_up[0], w_out[0], glu_w[0], w_down[0])

    ya = _ssm(u, bs_w, lam8, cs_w, tz_w, glu_w_b,
              row(glu_b[0]), row(g_out_ssm[0]), _head_mean_matrix(D_SSM, SSM_GROUP))

    return _ffn(x, ya, yb, mod3, w_out_b, row(g_post_mix[0]), row(g_pre_ffn[0]),
                w_up_b, ffn_conv_w[0], w_down_b, row(g_post_ffn[0]))
```

```python
import math

import jax
import jax.numpy as jnp
import numpy as np
from jax import lax
from jax.experimental import pallas as pl
from jax.experimental.pallas import tpu as pltpu

D_MODEL = 1024
BATCH = 8
SEQ = 4096
D_SSM = 512
D_CONV = 512
SSM_GROUP = 16
N_SSM_GROUPS = 32
SSM_STATE = 64
CONV_HEADS = 8
D_FF = 2816
N_MOD = 6
D_IN_PROJ = 2048
EPS = 1e-6
LAMBDA_RE_MAX = -1e-4

N_STATE = N_SSM_GROUPS * SSM_STATE
SUBLANES = 8
LANES = 128
GROUPS_PER_LANE_TILE = LANES // SSM_GROUP
N_LANE_TILES_U = D_SSM // LANES

T_PRE = 1024
N_PRE_STEPS = BATCH * (SEQ // T_PRE)
WDN_BLOCK_ROWS = 176
N_WDN_BLOCKS = D_FF // WDN_BLOCK_ROWS
T_SSM = 256
T_FFN = 1024
FF_CHUNK = 256
N_FF_CHUNKS = D_FF // FF_CHUNK
FFN_SLABS_PER_CHUNK = 2 * FF_CHUNK // LANES
M_CHUNK = 4
N_CHUNKS = T_SSM // M_CHUNK
CHUNK_ROWS = N_CHUNKS * BATCH
GROUPS_PER_BLOCK = GROUPS_PER_LANE_TILE // 2
BLOCKS_PER_TILE = GROUPS_PER_LANE_TILE // GROUPS_PER_BLOCK
N_BLOCKS = N_SSM_GROUPS // GROUPS_PER_BLOCK
BLOCK_LANES = GROUPS_PER_BLOCK * SSM_GROUP
BLOCK_STATE = GROUPS_PER_BLOCK * SSM_STATE
BLOCK_CHUNK = M_CHUNK * BLOCK_LANES
BLOCK_COLS = 2 * BLOCK_STATE
N_PREP_IN = 8
STEP_PAIRS = M_CHUNK // 2
VMEM_LIMIT = 60 * 1024 * 1024

_f32 = jnp.float32
_bf16 = jnp.bfloat16


def _const_spec(shape):
    nd = len(shape)
    return pl.BlockSpec(shape, lambda *_: (0,) * nd, pipeline_mode=pl.Buffered(1))


def _rms(x, g):
    ms = jnp.mean(x * x, axis=-1, keepdims=True)
    return x * lax.rsqrt(ms + EPS) * g


def _gelu_tanh(y):
    k = math.sqrt(2.0 / math.pi)
    half_y = 0.5 * y
    return half_y + half_y * jnp.tanh(y * (k + (k * 0.044715) * (y * y)))


def _causal_conv3(h, halo, w, slab_ref, slab0):
    rows = h.shape[0]
    outs = []
    for k in range(h.shape[1] // LANES):
        lanes = slice(k * LANES, (k + 1) * LANES)
        slab_ref[slab0 + k, 0:SUBLANES, :] = halo[:, lanes]
        slab_ref[slab0 + k, SUBLANES:SUBLANES + rows, :] = h[:, lanes]
        h1 = slab_ref[slab0 + k, pl.ds(SUBLANES - 1, rows, stride=1), :]
        h2 = slab_ref[slab0 + k, pl.ds(SUBLANES - 2, rows, stride=1), :]
        outs.append(w[0:1, lanes] * h2 + w[1:2, lanes] * h1 + w[2:3, lanes] * h[:, lanes])
    return jnp.concatenate(outs, axis=-1)


def _mod_kernel(c_ref, w_ref, b_ref, o_ref):
    c = c_ref[...]
    c_act = c * jax.nn.sigmoid(c)
    o_ref[...] = jnp.dot(c_act, w_ref[...], preferred_element_type=_f32) + b_ref[...]


def _setup_kernel(c_ref, w_ref, b_ref, *refs):
    mod_ref, prep_refs = refs[N_PREP_IN], refs[:N_PREP_IN] + refs[N_PREP_IN + 1:]
    _mod_kernel(c_ref, w_ref, b_ref, mod_ref)
    _prep_kernel(*prep_refs)


def _prep_kernel(lre_ref, lim_ref, lst_ref, bre_ref, bim_ref, cre_ref, cim_ref, d_ref,
                 lam8_ref, bs_ref, cs_ref, tz_ref):
    lre = jnp.minimum(lre_ref[0], LAMBDA_RE_MAX)
    lim = lim_ref[0]
    step = jnp.exp(lst_ref[0])
    log_mag, ang = lre * step, lim * step

    def lam_pow(k):
        mag = jnp.exp(k * log_mag)
        return mag * jnp.cos(k * ang), mag * jnp.sin(k * ang)

    pw = [lam_pow(float(k)) for k in range(M_CHUNK + 1)]

    def block_diag(blk):
        wide = jnp.concatenate([blk] * GROUPS_PER_BLOCK, axis=-1)
        r = lax.broadcasted_iota(jnp.int32, wide.shape, 0) // SSM_GROUP
        c = lax.broadcasted_iota(jnp.int32, wide.shape, 1) // SSM_STATE
        return jnp.where(r == c, wide, 0.0)

    a_re, a_im = pw[1]
    n_re = a_re - 1.0
    den = lre * lre + lim * lim
    q_re = (n_re * lre + a_im * lim) / den
    q_im = (a_im * lre - n_re * lim) / den
    b_re, b_im = block_diag(bre_ref[0]), block_diag(bim_ref[0])
    bb_re = q_re * b_re - q_im * b_im
    bb_im = q_re * b_im + q_im * b_re
    c_re, c_im = block_diag(cre_ref[0]), block_diag(cim_ref[0])

    lam8_ref[0] = jnp.broadcast_to(pw[M_CHUNK][0], (SUBLANES, BLOCK_STATE))
    lam8_ref[1] = jnp.broadcast_to(pw[M_CHUNK][1], (SUBLANES, BLOCK_STATE))
    for s in range(M_CHUNK):
        rows = slice(s * BLOCK_LANES, (s + 1) * BLOCK_LANES)
        p_re, p_im = pw[M_CHUNK - 1 - s]
        bs_ref[0, rows, 0:BLOCK_STATE] = (p_re * bb_re - p_im * bb_im).astype(_bf16)
        bs_ref[0, rows, BLOCK_STATE:] = (p_re * bb_im + p_im * bb_re).astype(_bf16)
        p_re, p_im = pw[s + 1]
        cs_ref[0, 0:BLOCK_STATE, rows] = (p_re * c_re - p_im * c_im).T.astype(_bf16)
        cs_ref[0, BLOCK_STATE:, rows] = (-(p_re * c_im + p_im * c_re)).T.astype(_bf16)
    tz_ref[...] = jnp.zeros(tz_ref.shape, _bf16)
    contract_p = (((1,), (1,)), ((), ()))
    for tau in range(M_CHUNK):
        p_re, p_im = pw[tau]
        ct_re = p_re * c_re - p_im * c_im
        ct_im = p_re * c_im + p_im * c_re
        k_tau = (lax.dot_general(bb_re, ct_re, contract_p, precision=lax.Precision.HIGHEST,
                                 preferred_element_type=_f32)
                 - lax.dot_general(bb_im, ct_im, contract_p, precision=lax.Precision.HIGHEST,
                                   preferred_element_type=_f32))
        if tau == 0:
            r = lax.broadcasted_iota(jnp.int32, k_tau.shape, 0)
            c = lax.broadcasted_iota(jnp.int32, k_tau.shape, 1)
            k_tau = k_tau + jnp.where(r == c, d_ref[0], 0.0)
        for s0 in range(M_CHUNK - tau):
            tz_ref[0, s0 * BLOCK_LANES:(s0 + 1) * BLOCK_LANES,
                   (s0 + tau) * BLOCK_LANES:(s0 + tau + 1) * BLOCK_LANES] = k_tau.astype(_bf16)


def _setup(c, w_ada, b_ada, lam_re, lam_im, log_step, b_re, b_im, c_re, c_im, d_skip):
    nb = N_BLOCKS
    n_tile = N_MOD * D_MODEL // nb
    lstep = jnp.broadcast_to(log_step[:, None], (N_SSM_GROUPS, SSM_STATE))
    as_row = lambda a: a.reshape(nb, 1, BLOCK_STATE)
    b_t = lambda a: a.transpose(0, 2, 1).reshape(nb, BLOCK_LANES, SSM_STATE)
    c_t = lambda a: a.reshape(nb, BLOCK_LANES, SSM_STATE)
    tile_spec = lambda r, c: pl.BlockSpec((1, r, c), lambda j: (j, 0, 0))
    prep_in = [tile_spec(1, BLOCK_STATE)] * 3 + [tile_spec(BLOCK_LANES, SSM_STATE)] * 4 + [tile_spec(1, BLOCK_LANES)]
    assert len(prep_in) == N_PREP_IN
    return pl.pallas_call(
        _setup_kernel,
        grid=(nb,),
        in_specs=[
            pl.BlockSpec((BATCH, D_MODEL), lambda j: (0, 0)),
            pl.BlockSpec((D_MODEL, n_tile), lambda j: (0, j)),
            pl.BlockSpec((1, n_tile), lambda j: (0, j)),
        ] + prep_in,
        out_specs=[
            pl.BlockSpec((BATCH, n_tile), lambda j: (0, j)),
            pl.BlockSpec((2, SUBLANES, BLOCK_STATE), lambda j: (0, 0, j)),
            tile_spec(BLOCK_CHUNK, BLOCK_COLS),
            tile_spec(BLOCK_COLS, BLOCK_CHUNK),
            tile_spec(BLOCK_CHUNK, BLOCK_CHUNK),
        ],
        out_shape=(
            jax.ShapeDtypeStruct((BATCH, N_MOD * D_MODEL), _f32),
            jax.ShapeDtypeStruct((2, SUBLANES, N_STATE), _f32),
            jax.ShapeDtypeStruct((nb, BLOCK_CHUNK, BLOCK_COLS), _bf16),
            jax.ShapeDtypeStruct((nb, BLOCK_COLS, BLOCK_CHUNK), _bf16),
            jax.ShapeDtypeStruct((nb, BLOCK_CHUNK, BLOCK_CHUNK), _bf16),
        ),
        name="adaln_and_s5_setup",
    )(c, w_ada, b_ada.reshape(1, -1), as_row(lam_re), as_row(lam_im), as_row(lstep), b_t(b_re), b_t(b_im),
      c_t(c_re), c_t(c_im), d_skip.reshape(nb, 1, BLOCK_LANES))


def _premix_kernel(x_ref, mod_ref, g_ref, win32_ref, cw_ref, gconv_ref, ones_ref, wup32_ref, wout32_ref, glu32_ref,
                   wdn32_ref, u_ref, yb_ref, wup_ref, wout_ref, glu_ref, wdn_ref, halo_ref, slab_ref, win_ref):
    ti = pl.program_id(1)
    step = pl.program_id(0) * (SEQ // T_PRE) + ti

    @pl.when(step == 0)
    def _():
        for k in range(D_IN_PROJ // D_CONV):
            cols = slice(k * D_CONV, (k + 1) * D_CONV)
            win_ref[:, cols] = win32_ref[:, cols].astype(_bf16)

    wup_ref[...] = wup32_ref[...].astype(_bf16)
    wout_ref[...] = wout32_ref[...].astype(_bf16)
    glu_ref[...] = glu32_ref[...].astype(_bf16)

    @pl.when(step < N_WDN_BLOCKS)
    def _():
        wdn_ref[...] = wdn32_ref[...].astype(_bf16)

    @pl.when(ti == 0)
    def _():
        halo_ref[...] = jnp.zeros(halo_ref.shape, _f32)

    x = x_ref[0]
    sh = mod_ref[0, 0:1, :]
    sc = mod_ref[0, 1:2, :]
    h = _rms(x, g_ref[...] * (1.0 + sc)) + sh
    hb = h.astype(_bf16)
    proj = jnp.dot(hb, win_ref[:, D_SSM:], preferred_element_type=_f32)
    u_ref[0] = jnp.dot(hb, win_ref[:, 0:D_SSM], preferred_element_type=_f32).astype(_bf16)
    bg = proj[:, 0:D_CONV]
    cg = proj[:, D_CONV:2 * D_CONV]
    v = proj[:, 2 * D_CONV:]
    cv = cg * v
    halo = halo_ref[...]
    halo_ref[...] = cv[T_PRE - SUBLANES:, :]
    yb = bg * _causal_conv3(cv, halo, cw_ref[0], slab_ref, 0)
    ms = jnp.dot((yb * yb).astype(_bf16), ones_ref[...], preferred_element_type=_f32)
    yb_ref[0] = (yb * lax.rsqrt(ms + EPS) * gconv_ref[...]).astype(_bf16)


def _premix(x, mod3, g_pre, w_in, conv_w, g_out_conv, ones_conv, w_up, w_out, glu_w, w_down):
    n_t = SEQ // T_PRE
    step = lambda b, t: b * n_t + t
    up_rows = D_MODEL // N_PRE_STEPS
    glu_rows = D_SSM // N_PRE_STEPS
    by_step = lambda b, t: (step(b, t), 0)
    wdn_block = lambda b, t: (jnp.minimum(step(b, t), N_WDN_BLOCKS - 1), 0)
    return pl.pallas_call(
        _premix_kernel,
        grid=(BATCH, n_t),
        in_specs=[
            pl.BlockSpec((1, T_PRE, D_MODEL), lambda b, t: (b, t, 0)),
            pl.BlockSpec((1, N_MOD, D_MODEL), lambda b, t: (b, 0, 0)),
            _const_spec((1, D_MODEL)),
            _const_spec((D_MODEL, D_IN_PROJ)),
            _const_spec((1, 3, D_CONV)),
            _const_spec((1, D_CONV)),
            _const_spec((D_CONV, D_CONV)),
            pl.BlockSpec((up_rows, 2 * D_FF), by_step),
            pl.BlockSpec((up_rows, D_MODEL), by_step),
            pl.BlockSpec((glu_rows, D_SSM), by_step),
            pl.BlockSpec((WDN_BLOCK_ROWS, D_MODEL), wdn_block),
        ],
        out_specs=[
            pl.BlockSpec((1, T_PRE, D_SSM), lambda b, t: (b, t, 0)),
            pl.BlockSpec((1, T_PRE, D_CONV), lambda b, t: (b, t, 0)),
            pl.BlockSpec((up_rows, 2 * D_FF), by_step),
            pl.BlockSpec((up_rows, D_MODEL), by_step),
            pl.BlockSpec((glu_rows, D_SSM), by_step),
            pl.BlockSpec((WDN_BLOCK_ROWS, D_MODEL), wdn_block),
        ],
        out_shape=[
            jax.ShapeDtypeStruct((BATCH, SEQ, D_SSM), _bf16),
            jax.ShapeDtypeStruct((BATCH, SEQ, D_CONV), _bf16),
            jax.ShapeDtypeStruct((D_MODEL, 2 * D_FF), _bf16),
            jax.ShapeDtypeStruct((D_MODEL, D_MODEL), _bf16),
            jax.ShapeDtypeStruct((D_SSM, D_SSM), _bf16),
            jax.ShapeDtypeStruct((D_FF, D_MODEL), _bf16),
        ],
        scratch_shapes=[
            pltpu.VMEM((SUBLANES, D_CONV), _f32),
            pltpu.VMEM((D_CONV // LANES, T_PRE + SUBLANES, LANES), _f32),
            pltpu.VMEM((D_MODEL, D_IN_PROJ), _bf16),
        ],
        compiler_params=pltpu.CompilerParams(
            dimension_semantics=("arbitrary", "arbitrary"), vmem_limit_bytes=VMEM_LIMIT),
        name="premix",
    )(x, mod3, g_pre, w_in, conv_w, g_out_conv, ones_conv, w_up, w_out, glu_w, w_down)


def _ssm_kernel(u_ref, bs_ref, lam_ref, cs_ref, tz_ref, gluw_ref, glub_ref, g_ref, ones_ref,
                y_ref, st_ref, u32_ref, ufl_ref, s_ref, yfl_ref, y32_ref):
    i = pl.program_id(0)

    @pl.when(i == 0)
    def _():
        st_ref[...] = jnp.zeros(st_ref.shape, _f32)

    low_half = lax.broadcasted_iota(jnp.int32, (N_CHUNKS, LANES), 1) < BLOCK_LANES
    swap_halves = lambda v: pltpu.roll(v, BLOCK_LANES, axis=1)

    u = u_ref[...].reshape(BATCH * T_SSM, D_SSM)
    for j in range(N_LANE_TILES_U):
        u32_ref[j] = u[:, j * LANES:(j + 1) * LANES].astype(_f32)
    for j in range(N_LANE_TILES_U):
        for q in range(STEP_PAIRS):
            for b in range(BATCH):
                v0 = u32_ref[j, pl.ds(b * T_SSM + 2 * q, N_CHUNKS, stride=M_CHUNK), :]
                v1 = u32_ref[j, pl.ds(b * T_SSM + 2 * q + 1, N_CHUNKS, stride=M_CHUNK), :]
                dst = pl.ds(b, N_CHUNKS, stride=BATCH)
                ufl_ref[BLOCKS_PER_TILE * j, q, dst, :] = jnp.where(low_half, v0, swap_halves(v1))
                ufl_ref[BLOCKS_PER_TILE * j + 1, q, dst, :] = jnp.where(low_half, swap_halves(v0), v1)

    def chunk_lhs(blk):
        return jnp.concatenate([ufl_ref[blk, q] for q in range(STEP_PAIRS)], axis=-1).astype(_bf16)

    def bproj(j):
        for blk in range(BLOCKS_PER_TILE * j, BLOCKS_PER_TILE * (j + 1)):
            s_ref[:, blk * BLOCK_COLS:(blk + 1) * BLOCK_COLS] = jnp.dot(
                chunk_lhs(blk), bs_ref[blk], preferred_element_type=_f32)

    def scan(j):
        re, lam = [], []
        for blk in range(BLOCKS_PER_TILE * j, BLOCKS_PER_TILE * (j + 1)):
            for k in range(BLOCK_STATE // LANES):
                re.append(slice(blk * BLOCK_COLS + k * LANES, blk * BLOCK_COLS + (k + 1) * LANES))
                lam.append(slice(blk * BLOCK_STATE + k * LANES, blk * BLOCK_STATE + (k + 1) * LANES))
        im = [slice(r.start + BLOCK_STATE, r.stop + BLOCK_STATE) for r in re]
        n = len(re)
        lre = [lam_ref[0, :, ln] for ln in lam]
        lim = [lam_ref[1, :, ln] for ln in lam]
        xre = [st_ref[:, r] for r in re]
        xim = [st_ref[:, r] for r in im]
        for c in range(N_CHUNKS):
            rows = slice(c * BATCH, (c + 1) * BATCH)
            for k in range(n):
                loc_re, loc_im = s_ref[rows, re[k]], s_ref[rows, im[k]]
                s_ref[rows, re[k]] = xre[k]
                s_ref[rows, im[k]] = xim[k]
                xre[k], xim[k] = (lre[k] * xre[k] - lim[k] * xim[k] + loc_re,
                                  lre[k] * xim[k] + lim[k] * xre[k] + loc_im)
        for k in range(n):
            st_ref[:, re[k]] = xre[k]
            st_ref[:, im[k]] = xim[k]

    def cproj(j):
        for blk in range(BLOCKS_PER_TILE * j, BLOCKS_PER_TILE * (j + 1)):
            x_in = s_ref[:, blk * BLOCK_COLS:(blk + 1) * BLOCK_COLS].astype(_bf16)
            yf = jnp.dot(x_in, cs_ref[blk], preferred_element_type=_f32)
            yf = yf + jnp.dot(chunk_lhs(blk), tz_ref[blk], preferred_element_type=_f32)
            for q in range(STEP_PAIRS):
                yfl_ref[blk, q] = yf[:, q * LANES:(q + 1) * LANES]

    bproj(0)
    for j in range(N_LANE_TILES_U):
        if j + 1 < N_LANE_TILES_U:
            bproj(j + 1)
        if j > 0:
            cproj(j - 1)
        scan(j)
    cproj(N_LANE_TILES_U - 1)

    for j in range(N_LANE_TILES_U):
        for q in range(STEP_PAIRS):
            for b in range(BATCH):
                src = pl.ds(b, N_CHUNKS, stride=BATCH)
                lo = yfl_ref[BLOCKS_PER_TILE * j, q, src, :]
                hi = yfl_ref[BLOCKS_PER_TILE * j + 1, q, src, :]
                y32_ref[j, pl.ds(b * T_SSM + 2 * q, N_CHUNKS, stride=M_CHUNK), :] = (
                    jnp.where(low_half, lo, swap_halves(hi)))
                y32_ref[j, pl.ds(b * T_SSM + 2 * q + 1, N_CHUNKS, stride=M_CHUNK), :] = (
                    jnp.where(low_half, swap_halves(lo), hi))
    y = jnp.concatenate([y32_ref[j] for j in range(N_LANE_TILES_U)], axis=-1)
    z = _gelu_tanh(y).astype(_bf16)
    gate = jnp.dot(z, gluw_ref[...], preferred_element_type=_f32) + glub_ref[...]
    z = z * jax.nn.sigmoid(gate.astype(_bf16))
    ms = jnp.dot(z * z, ones_ref[...], preferred_element_type=_f32)
    out = z * lax.rsqrt(ms + EPS).astype(_bf16) * g_ref[...].astype(_bf16)
    y_ref[...] = out.reshape(BATCH, T_SSM, D_SSM)


def _ssm(u, bs_w, lam8, cs_w, tz_w, glu_w_b, glu_b, g_out_ssm, ones_ssm):
    rows = BATCH * T_SSM
    return pl.pallas_call(
        _ssm_kernel,
        grid=(SEQ // T_SSM,),
        in_specs=[
            pl.BlockSpec((BATCH, T_SSM, D_SSM), lambda i: (0, i, 0)),
            _const_spec((N_BLOCKS, BLOCK_CHUNK, BLOCK_COLS)),
            _const_spec((2, SUBLANES, N_STATE)),
            _const_spec((N_BLOCKS, BLOCK_COLS, BLOCK_CHUNK)),
            _const_spec((N_BLOCKS, BLOCK_CHUNK, BLOCK_CHUNK)),
            _const_spec((D_SSM, D_SSM)),
            _const_spec((1, D_SSM)),
            _const_spec((1, D_SSM)),
            _const_spec((D_SSM, D_SSM)),
        ],
        out_specs=pl.BlockSpec((BATCH, T_SSM, D_SSM), lambda i: (0, i, 0)),
        out_shape=jax.ShapeDtypeStruct((BATCH, SEQ, D_SSM), _bf16),
        scratch_shapes=[
            pltpu.VMEM((SUBLANES, N_BLOCKS * BLOCK_COLS), _f32),
            pltpu.VMEM((N_LANE_TILES_U, rows, LANES), _f32),
            pltpu.VMEM((N_BLOCKS, STEP_PAIRS, CHUNK_ROWS, LANES), _f32),
            pltpu.VMEM((CHUNK_ROWS, N_BLOCKS * BLOCK_COLS), _f32),
            pltpu.VMEM((N_BLOCKS, STEP_PAIRS, CHUNK_ROWS, LANES), _f32),
            pltpu.VMEM((N_LANE_TILES_U, rows, LANES), _f32),
        ],
        compiler_params=pltpu.CompilerParams(
            dimension_semantics=("arbitrary",), vmem_limit_bytes=VMEM_LIMIT),
        name="s5_mixer",
    )(u, bs_w, lam8, cs_w, tz_w, glu_w_b, glu_b, g_out_ssm, ones_ssm)


def _ffn_kernel(x_ref, ya_ref, yb_ref, mod_ref, wout_ref, gpm_ref, gpf_ref, wup_ref, fcw_ref, wdn_ref, gpo_ref,
                o_ref, halo_ref, act_ref, slab_ref):
    ti = pl.program_id(1)

    @pl.when(ti == 0)
    def _():
        halo_ref[...] = jnp.zeros(halo_ref.shape, _f32)

    x = x_ref[0]
    gt1 = mod_ref[0, 2:3, :]
    sh2 = mod_ref[0, 3:4, :]
    sc2 = mod_ref[0, 4:5, :]
    gt2 = mod_ref[0, 5:6, :]
    mix = jnp.dot(jnp.concatenate([ya_ref[0], yb_ref[0]], axis=-1), wout_ref[...], preferred_element_type=_f32)
    x1 = x + _rms(mix, gpm_ref[...] * gt1)
    h2 = (_rms(x1, gpf_ref[...] * (1.0 + sc2)) + sh2).astype(_bf16)

    def up_chunk(c):
        cols_a = slice(c * FF_CHUNK, (c + 1) * FF_CHUNK)
        cols_v = slice(D_FF + c * FF_CHUNK, D_FF + (c + 1) * FF_CHUNK)
        return (jnp.dot(h2, wup_ref[:, cols_a], preferred_element_type=_f32), cols_a,
                jnp.dot(h2, wup_ref[:, cols_v], preferred_element_type=_f32), cols_v)

    def conv(hc, cols, slab0):
        halo = halo_ref[:, cols]
        halo_ref[:, cols] = hc[T_FFN - SUBLANES:, :]
        return _causal_conv3(hc, halo, fcw_ref[0, :, cols], slab_ref, slab0)

    nxt = up_chunk(0)
    for c in range(N_FF_CHUNKS):
        ha, cols_a, hv, cols_v = nxt
        if c + 1 < N_FF_CHUNKS:
            nxt = up_chunk(c + 1)
        slab0 = (c % 2) * FFN_SLABS_PER_CHUNK
        a = conv(ha, cols_a, slab0)
        v = conv(hv, cols_v, slab0 + FF_CHUNK // LANES)
        act_ref[:, cols_a] = (a * jax.nn.sigmoid(a) * v).astype(_bf16)
    down = jnp.dot(act_ref[...], wdn_ref[...], preferred_element_type=_f32)
    o_ref[0] = x1 + _rms(down, gpo_ref[...] * gt2)


def _ffn(x, ya, yb, mod3, w_out_b, g_post_mix, g_pre_ffn, w_up_b, ffn_cw, w_down_b, g_post_ffn):
    n_t = SEQ // T_FFN
    return pl.pallas_call(
        _ffn_kernel,
        grid=(BATCH, n_t),
        in_specs=[
            pl.BlockSpec((1, T_FFN, D_MODEL), lambda b, t: (b, t, 0)),
            pl.BlockSpec((1, T_FFN, D_SSM), lambda b, t: (b, t, 0)),
            pl.BlockSpec((1, T_FFN, D_CONV), lambda b, t: (b, t, 0)),
            pl.BlockSpec((1, N_MOD, D_MODEL), lambda b, t: (b, 0, 0)),
            _const_spec((D_MODEL, D_MODEL)),
            _const_spec((1, D_MODEL)),
            _const_spec((1, D_MODEL)),
            _const_spec((D_MODEL, 2 * D_FF)),
            _const_spec((1, 3, 2 * D_FF)),
            _const_spec((D_FF, D_MODEL)),
            _const_spec((1, D_MODEL)),
        ],
        out_specs=pl.BlockSpec((1, T_FFN, D_MODEL), lambda b, t: (b, t, 0)),
        out_shape=jax.ShapeDtypeStruct((BATCH, SEQ, D_MODEL), _f32),
        scratch_shapes=[
            pltpu.VMEM((SUBLANES, 2 * D_FF), _f32),
            pltpu.VMEM((T_FFN, D_FF), _bf16),
            pltpu.VMEM((2 * FFN_SLABS_PER_CHUNK, T_FFN + SUBLANES, LANES), _f32),
        ],
        compiler_params=pltpu.CompilerParams(
            dimension_semantics=("arbitrary", "arbitrary"), vmem_limit_bytes=VMEM_LIMIT),
        name="outproj_convffn",
    )(x, ya, yb, mod3, w_out_b, g_post_mix, g_pre_ffn, w_up_b, ffn_cw, w_down_b, g_post_ffn)


def _head_mean_matrix(width, head):
    idx = np.arange(width) // head
    return jnp.asarray(np.where(idx[:, None] == idx[None, :], 1.0 / head, 0.0), dtype=_bf16)


def kernel(x, c, w_ada, b_ada, g_pre_mix, g_post_mix, w_in, ssm_lam_re, ssm_lam_im, ssm_log_step, ssm_b_re, ssm_b_im, ssm_c_re, ssm_c_im, ssm_d, glu_w, glu_b, g_out_ssm, conv_w, g_out_conv, w_out, g_pre_ffn, g_post_ffn, w_up, ffn_conv_w, w_down):
    assert x.shape == (BATCH, SEQ, D_MODEL) and w_ada.shape[0] == 1
    row = lambda a: a.reshape(1, -1)

    mod, lam8, bs_w, cs_w, tz_w = _setup(
        c, w_ada[0], b_ada[0], ssm_lam_re[0], ssm_lam_im[0], ssm_log_step[0], ssm_b_re[0], ssm_b_im[0],
        ssm_c_re[0], ssm_c_im[0], ssm_d[0])
    mod3 = mod.reshape(BATCH, N_MOD, D_MODEL)

    u, yb, w_up_b, w_out_b, glu_w_b, w_down_b = _premix(
        x, mod3, row(g_pre_mix[0]), w_in[0], conv_w, row(g_out_conv[0]),
        _head_mean_matrix(D_CONV, D_CONV // CONV_HEADS), w_up[0], w_out[0], glu_w[0], w_down[0])

    ya = _ssm(u, bs_w, lam8, cs_w, tz_w, glu_w_b,
              row(glu_b[0]), row(g_out_ssm[0]), _head_mean_matrix(D_SSM, SSM_GROUP))

    return _ffn(x, ya, yb, mod3, w_out_b, row(g_post_mix[0]), row(g_pre_ffn[0]),
                w_up_b, ffn_conv_w, w_down_b, row(g_post_ffn[0]))
```

```python
import math

import jax
import jax.numpy as jnp
import numpy as np
from jax import lax
from jax.experimental import pallas as pl
from jax.experimental.pallas import tpu as pltpu

D_MODEL = 1024
BATCH = 8
SEQ = 4096
D_SSM = 512
D_CONV = 512
SSM_GROUP = 16
N_SSM_GROUPS = 32
SSM_STATE = 64
CONV_HEADS = 8
D_FF = 2816
N_MOD = 6
D_IN_PROJ = 2048
EPS = 1e-6
LAMBDA_RE_MAX = -1e-4

N_STATE = N_SSM_GROUPS * SSM_STATE
SUBLANES = 8
LANES = 128
GROUPS_PER_LANE_TILE = LANES // SSM_GROUP
N_LANE_TILES_U = D_SSM // LANES

T_PRE = 1024
N_PRE_STEPS = BATCH * (SEQ // T_PRE)
WDN_BLOCK_ROWS = 176
N_WDN_BLOCKS = D_FF // WDN_BLOCK_ROWS
T_SSM = 256
T_FFN = 1024
FF_CHUNK = 256
N_FF_CHUNKS = D_FF // FF_CHUNK
FFN_SLABS_PER_CHUNK = 2 * FF_CHUNK // LANES
M_CHUNK = 4
N_CHUNKS = T_SSM // M_CHUNK
CHUNK_ROWS = N_CHUNKS * BATCH
GROUPS_PER_BLOCK = GROUPS_PER_LANE_TILE // 2
BLOCKS_PER_TILE = GROUPS_PER_LANE_TILE // GROUPS_PER_BLOCK
N_BLOCKS = N_SSM_GROUPS // GROUPS_PER_BLOCK
BLOCK_LANES = GROUPS_PER_BLOCK * SSM_GROUP
BLOCK_STATE = GROUPS_PER_BLOCK * SSM_STATE
BLOCK_CHUNK = M_CHUNK * BLOCK_LANES
BLOCK_COLS = 2 * BLOCK_STATE
MOD_TILE = N_MOD * D_MODEL // N_BLOCKS
N_PREP_IN = 8
STEP_PAIRS = M_CHUNK // 2
VMEM_LIMIT = 60 * 1024 * 1024

_f32 = jnp.float32
_bf16 = jnp.bfloat16


def _const_spec(shape):
    nd = len(shape)
    return pl.BlockSpec(shape, lambda *_: (0,) * nd, pipeline_mode=pl.Buffered(1))


def _rms(x, g):
    ms = jnp.mean(x * x, axis=-1, keepdims=True)
    return x * lax.rsqrt(ms + EPS) * g


def _gelu_tanh(y):
    k = math.sqrt(2.0 / math.pi)
    half_y = 0.5 * y
    return half_y + half_y * jnp.tanh(y * (k + (k * 0.044715) * (y * y)))


def _causal_conv3(h, halo, w, slab_ref, slab0):
    rows = h.shape[0]
    outs = []
    for k in range(h.shape[1] // LANES):
        lanes = slice(k * LANES, (k + 1) * LANES)
        slab_ref[slab0 + k, 0:SUBLANES, :] = halo[:, lanes]
        slab_ref[slab0 + k, SUBLANES:SUBLANES + rows, :] = h[:, lanes]
        h1 = slab_ref[slab0 + k, pl.ds(SUBLANES - 1, rows, stride=1), :]
        h2 = slab_ref[slab0 + k, pl.ds(SUBLANES - 2, rows, stride=1), :]
        outs.append(w[0:1, lanes] * h2 + w[1:2, lanes] * h1 + w[2:3, lanes] * h[:, lanes])
    return jnp.concatenate(outs, axis=-1)


def _mod_kernel(c_ref, w_ref, b_ref, o_ref):
    c = c_ref[...]
    c_act = c * jax.nn.sigmoid(c)
    o_ref[...] = jnp.dot(c_act, w_ref[...], preferred_element_type=_f32) + b_ref[...]


def _setup_kernel(c_ref, w_hbm, b_ref, *refs):
    prep_in = refs[:N_PREP_IN]
    mod_ref, lam8_ref, bs_ref, cs_ref, tz_ref, wbuf_ref, sem = refs[N_PREP_IN:]
    tile_copy = lambda j: pltpu.make_async_copy(
        w_hbm.at[:, pl.ds(j * MOD_TILE, MOD_TILE)], wbuf_ref.at[j], sem.at[j])
    for j in range(N_BLOCKS):
        tile_copy(j).start()
    for j in range(N_BLOCKS):
        blk = pl.ds(j, 1)
        _prep_kernel(*[r.at[blk] for r in prep_in],
                     lam8_ref.at[:, :, pl.ds(j * BLOCK_STATE, BLOCK_STATE)],
                     bs_ref.at[blk], cs_ref.at[blk], tz_ref.at[blk])
        tile_copy(j).wait()
        cols = pl.ds(j * MOD_TILE, MOD_TILE)
        _mod_kernel(c_ref, wbuf_ref.at[j], b_ref.at[:, cols], mod_ref.at[:, cols])


def _prep_kernel(lre_ref, lim_ref, lst_ref, bre_ref, bim_ref, cre_ref, cim_ref, d_ref,
                 lam8_ref, bs_ref, cs_ref, tz_ref):
    lre = jnp.minimum(lre_ref[0], LAMBDA_RE_MAX)
    lim = lim_ref[0]
    step = jnp.exp(lst_ref[0])
    log_mag, ang = lre * step, lim * step

    def lam_pow(k):
        mag = jnp.exp(k * log_mag)
        return mag * jnp.cos(k * ang), mag * jnp.sin(k * ang)

    pw = [lam_pow(float(k)) for k in range(M_CHUNK + 1)]

    def block_diag(blk):
        wide = jnp.concatenate([blk] * GROUPS_PER_BLOCK, axis=-1)
        r = lax.broadcasted_iota(jnp.int32, wide.shape, 0) // SSM_GROUP
        c = lax.broadcasted_iota(jnp.int32, wide.shape, 1) // SSM_STATE
        return jnp.where(r == c, wide, 0.0)

    a_re, a_im = pw[1]
    n_re = a_re - 1.0
    den = lre * lre + lim * lim
    q_re = (n_re * lre + a_im * lim) / den
    q_im = (a_im * lre - n_re * lim) / den
    b_re, b_im = block_diag(bre_ref[0]), block_diag(bim_ref[0])
    bb_re = q_re * b_re - q_im * b_im
    bb_im = q_re * b_im + q_im * b_re
    c_re, c_im = block_diag(cre_ref[0]), block_diag(cim_ref[0])

    lam8_ref[0] = jnp.broadcast_to(pw[M_CHUNK][0], (SUBLANES, BLOCK_STATE))
    lam8_ref[1] = jnp.broadcast_to(pw[M_CHUNK][1], (SUBLANES, BLOCK_STATE))
    for s in range(M_CHUNK):
        rows = slice(s * BLOCK_LANES, (s + 1) * BLOCK_LANES)
        p_re, p_im = pw[M_CHUNK - 1 - s]
        bs_ref[0, rows, 0:BLOCK_STATE] = (p_re * bb_re - p_im * bb_im).astype(_bf16)
        bs_ref[0, rows, BLOCK_STATE:] = (p_re * bb_im + p_im * bb_re).astype(_bf16)
        p_re, p_im = pw[s + 1]
        cs_ref[0, 0:BLOCK_STATE, rows] = (p_re * c_re - p_im * c_im).T.astype(_bf16)
        cs_ref[0, BLOCK_STATE:, rows] = (-(p_re * c_im + p_im * c_re)).T.astype(_bf16)
    tz_ref[...] = jnp.zeros(tz_ref.shape, _bf16)
    contract_p = (((1,), (1,)), ((), ()))
    for tau in range(M_CHUNK):
        p_re, p_im = pw[tau]
        ct_re = p_re * c_re - p_im * c_im
        ct_im = p_re * c_im + p_im * c_re
        k_tau = (lax.dot_general(bb_re, ct_re, contract_p, precision=lax.Precision.HIGHEST,
                                 preferred_element_type=_f32)
                 - lax.dot_general(bb_im, ct_im, contract_p, precision=lax.Precision.HIGHEST,
                                   preferred_element_type=_f32))
        if tau == 0:
            r = lax.broadcasted_iota(jnp.int32, k_tau.shape, 0)
            c = lax.broadcasted_iota(jnp.int32, k_tau.shape, 1)
            k_tau = k_tau + jnp.where(r == c, d_ref[0], 0.0)
        for s0 in range(M_CHUNK - tau):
            tz_ref[0, s0 * BLOCK_LANES:(s0 + 1) * BLOCK_LANES,
                   (s0 + tau) * BLOCK_LANES:(s0 + tau + 1) * BLOCK_LANES] = k_tau.astype(_bf16)


def _setup(c, w_ada, b_ada, lam_re, lam_im, log_step, b_re, b_im, c_re, c_im, d_skip):
    nb = N_BLOCKS
    lstep = jnp.broadcast_to(log_step[:, None], (N_SSM_GROUPS, SSM_STATE))
    as_row = lambda a: a.reshape(nb, 1, BLOCK_STATE)
    b_t = lambda a: a.transpose(0, 2, 1).reshape(nb, BLOCK_LANES, SSM_STATE)
    c_t = lambda a: a.reshape(nb, BLOCK_LANES, SSM_STATE)
    in_vmem = pl.BlockSpec(memory_space=pltpu.VMEM)
    return pl.pallas_call(
        _setup_kernel,
        in_specs=[in_vmem, pl.BlockSpec(memory_space=pl.ANY), in_vmem] + [in_vmem] * N_PREP_IN,
        out_specs=[in_vmem] * 5,
        scratch_shapes=[
            pltpu.VMEM((nb, D_MODEL, MOD_TILE), _f32),
            pltpu.SemaphoreType.DMA((nb,)),
        ],
        compiler_params=pltpu.CompilerParams(vmem_limit_bytes=VMEM_LIMIT),
        out_shape=(
            jax.ShapeDtypeStruct((BATCH, N_MOD * D_MODEL), _f32),
            jax.ShapeDtypeStruct((2, SUBLANES, N_STATE), _f32),
            jax.ShapeDtypeStruct((nb, BLOCK_CHUNK, BLOCK_COLS), _bf16),
            jax.ShapeDtypeStruct((nb, BLOCK_COLS, BLOCK_CHUNK), _bf16),
            jax.ShapeDtypeStruct((nb, BLOCK_CHUNK, BLOCK_CHUNK), _bf16),
        ),
        name="adaln_and_s5_setup",
    )(c, w_ada, b_ada.reshape(1, -1), as_row(lam_re), as_row(lam_im), as_row(lstep), b_t(b_re), b_t(b_im),
      c_t(c_re), c_t(c_im), d_skip.reshape(nb, 1, BLOCK_LANES))


def _premix_kernel(x_ref, mod_ref, g_ref, win32_ref, cw_ref, gconv_ref, ones_ref, wup32_ref, wout32_ref, glu32_ref,
                   wdn32_ref, u_ref, yb_ref, wup_ref, wout_ref, glu_ref, wdn_ref, halo_ref, slab_ref, win_ref):
    ti = pl.program_id(1)
    step = pl.program_id(0) * (SEQ // T_PRE) + ti

    @pl.when(step == 0)
    def _():
        for k in range(D_IN_PROJ // D_CONV):
            cols = slice(k * D_CONV, (k + 1) * D_CONV)
            win_ref[:, cols] = win32_ref[:, cols].astype(_bf16)

    wup_ref[...] = wup32_ref[...].astype(_bf16)
    wout_ref[...] = wout32_ref[...].astype(_bf16)
    glu_ref[...] = glu32_ref[...].astype(_bf16)

    @pl.when(step < N_WDN_BLOCKS)
    def _():
        wdn_ref[...] = wdn32_ref[...].astype(_bf16)

    @pl.when(ti == 0)
    def _():
        halo_ref[...] = jnp.zeros(halo_ref.shape, _f32)

    x = x_ref[0]
    sh = mod_ref[0, 0:1, :]
    sc = mod_ref[0, 1:2, :]
    h = _rms(x, g_ref[...] * (1.0 + sc)) + sh
    hb = h.astype(_bf16)
    proj = jnp.dot(hb, win_ref[:, D_SSM:], preferred_element_type=_f32)
    u_ref[0] = jnp.dot(hb, win_ref[:, 0:D_SSM], preferred_element_type=_f32).astype(_bf16)
    bg = proj[:, 0:D_CONV]
    cg = proj[:, D_CONV:2 * D_CONV]
    v = proj[:, 2 * D_CONV:]
    cv = cg * v
    halo = halo_ref[...]
    halo_ref[...] = cv[T_PRE - SUBLANES:, :]
    yb = bg * _causal_conv3(cv, halo, cw_ref[0], slab_ref, 0)
    ms = jnp.dot((yb * yb).astype(_bf16), ones_ref[...], preferred_element_type=_f32)
    yb_ref[0] = (yb * lax.rsqrt(ms + EPS) * gconv_ref[...]).astype(_bf16)


def _premix(x, mod3, g_pre, w_in, conv_w, g_out_conv, ones_conv, w_up, w_out, glu_w, w_down):
    n_t = SEQ // T_PRE
    step = lambda b, t: b * n_t + t
    up_rows = D_MODEL // N_PRE_STEPS
    glu_rows = D_SSM // N_PRE_STEPS
    by_step = lambda b, t: (step(b, t), 0)
    wdn_block = lambda b, t: (jnp.minimum(step(b, t), N_WDN_BLOCKS - 1), 0)
    return pl.pallas_call(
        _premix_kernel,
        grid=(BATCH, n_t),
        in_specs=[
            pl.BlockSpec((1, T_PRE, D_MODEL), lambda b, t: (b, t, 0)),
            pl.BlockSpec((1, N_MOD, D_MODEL), lambda b, t: (b, 0, 0)),
            _const_spec((1, D_MODEL)),
            _const_spec((D_MODEL, D_IN_PROJ)),
            _const_spec((1, 3, D_CONV)),
            _const_spec((1, D_CONV)),
            _const_spec((D_CONV, D_CONV)),
            pl.BlockSpec((up_rows, 2 * D_FF), by_step),
            pl.BlockSpec((up_rows, D_MODEL), by_step),
            pl.BlockSpec((glu_rows, D_SSM), by_step),
            pl.BlockSpec((WDN_BLOCK_ROWS, D_MODEL), wdn_block),
        ],
        out_specs=[
            pl.BlockSpec((1, T_PRE, D_SSM), lambda b, t: (b, t, 0)),
            pl.BlockSpec((1, T_PRE, D_CONV), lambda b, t: (b, t, 0)),
            pl.BlockSpec((up_rows, 2 * D_FF), by_step),
            pl.BlockSpec((up_rows, D_MODEL), by_step),
            pl.BlockSpec((glu_rows, D_SSM), by_step),
            pl.BlockSpec((WDN_BLOCK_ROWS, D_MODEL), wdn_block),
        ],
        out_shape=[
            jax.ShapeDtypeStruct((BATCH, SEQ, D_SSM), _bf16),
            jax.ShapeDtypeStruct((BATCH, SEQ, D_CONV), _bf16),
            jax.ShapeDtypeStruct((D_MODEL, 2 * D_FF), _bf16),
            jax.ShapeDtypeStruct((D_MODEL, D_MODEL), _bf16),
            jax.ShapeDtypeStruct((D_SSM, D_SSM), _bf16),
            jax.ShapeDtypeStruct((D_FF, D_MODEL), _bf16),
        ],
        scratch_shapes=[
            pltpu.VMEM((SUBLANES, D_CONV), _f32),
            pltpu.VMEM((D_CONV // LANES, T_PRE + SUBLANES, LANES), _f32),
            pltpu.VMEM((D_MODEL, D_IN_PROJ), _bf16),
        ],
        compiler_params=pltpu.CompilerParams(
            dimension_semantics=("arbitrary", "arbitrary"), vmem_limit_bytes=VMEM_LIMIT),
        name="premix",
    )(x, mod3, g_pre, w_in, conv_w, g_out_conv, ones_conv, w_up, w_out, glu_w, w_down)


def _ssm_kernel(u_ref, bs_ref, lam_ref, cs_ref, tz_ref, gluw_ref, glub_ref, g_ref, ones_ref,
                y_ref, st_ref, u32_ref, ufl_ref, s_ref, yfl_ref, y32_ref):
    i = pl.program_id(0)

    @pl.when(i == 0)
    def _():
        st_ref[...] = jnp.zeros(st_ref.shape, _f32)

    low_half = lax.broadcasted_iota(jnp.int32, (N_CHUNKS, LANES), 1) < BLOCK_LANES
    swap_halves = lambda v: pltpu.roll(v, BLOCK_LANES, axis=1)

    u = u_ref[...].reshape(BATCH * T_SSM, D_SSM)
    for j in range(N_LANE_TILES_U):
        u32_ref[j] = u[:, j * LANES:(j + 1) * LANES].astype(_f32)
    for j in range(N_LANE_TILES_U):
        for q in range(STEP_PAIRS):
            for b in range(BATCH):
                v0 = u32_ref[j, pl.ds(b * T_SSM + 2 * q, N_CHUNKS, stride=M_CHUNK), :]
                v1 = u32_ref[j, pl.ds(b * T_SSM + 2 * q + 1, N_CHUNKS, stride=M_CHUNK), :]
                dst = pl.ds(b, N_CHUNKS, stride=BATCH)
                ufl_ref[BLOCKS_PER_TILE * j, q, dst, :] = jnp.where(low_half, v0, swap_halves(v1))
                ufl_ref[BLOCKS_PER_TILE * j + 1, q, dst, :] = jnp.where(low_half, swap_halves(v0), v1)

    def chunk_lhs(blk):
        return jnp.concatenate([ufl_ref[blk, q] for q in range(STEP_PAIRS)], axis=-1).astype(_bf16)

    def bproj(j):
        for blk in range(BLOCKS_PER_TILE * j, BLOCKS_PER_TILE * (j + 1)):
            s_ref[:, blk * BLOCK_COLS:(blk + 1) * BLOCK_COLS] = jnp.dot(
                chunk_lhs(blk), bs_ref[blk], preferred_element_type=_f32)

    def scan(j):
        re, lam = [], []
        for blk in range(BLOCKS_PER_TILE * j, BLOCKS_PER_TILE * (j + 1)):
            for k in range(BLOCK_STATE // LANES):
                re.append(slice(blk * BLOCK_COLS + k * LANES, blk * BLOCK_COLS + (k + 1) * LANES))
                lam.append(slice(blk * BLOCK_STATE + k * LANES, blk * BLOCK_STATE + (k + 1) * LANES))
        im = [slice(r.start + BLOCK_STATE, r.stop + BLOCK_STATE) for r in re]
        n = len(re)
        lre = [lam_ref[0, :, ln] for ln in lam]
        lim = [lam_ref[1, :, ln] for ln in lam]
        xre = [st_ref[:, r] for r in re]
        xim = [st_ref[:, r] for r in im]
        for c in range(N_CHUNKS):
            rows = slice(c * BATCH, (c + 1) * BATCH)
            for k in range(n):
                loc_re, loc_im = s_ref[rows, re[k]], s_ref[rows, im[k]]
                s_ref[rows, re[k]] = xre[k]
                s_ref[rows, im[k]] = xim[k]
                xre[k], xim[k] = (lre[k] * xre[k] - lim[k] * xim[k] + loc_re,
                                  lre[k] * xim[k] + lim[k] * xre[k] + loc_im)
        for k in range(n):
            st_ref[:, re[k]] = xre[k]
            st_ref[:, im[k]] = xim[k]

    def cproj(j):
        for blk in range(BLOCKS_PER_TILE * j, BLOCKS_PER_TILE * (j + 1)):
            x_in = s_ref[:, blk * BLOCK_COLS:(blk + 1) * BLOCK_COLS].astype(_bf16)
            yf = jnp.dot(x_in, cs_ref[blk], preferred_element_type=_f32)
            yf = yf + jnp.dot(chunk_lhs(blk), tz_ref[blk], preferred_element_type=_f32)
            for q in range(STEP_PAIRS):
                yfl_ref[blk, q] = yf[:, q * LANES:(q + 1) * LANES]

    bproj(0)
    for j in range(N_LANE_TILES_U):
        if j + 1 < N_LANE_TILES_U:
            bproj(j + 1)
        if j > 0:
            cproj(j - 1)
        scan(j)
    cproj(N_LANE_TILES_U - 1)

    for j in range(N_LANE_TILES_U):
        for q in range(STEP_PAIRS):
            for b in range(BATCH):
                src = pl.ds(b, N_CHUNKS, stride=BATCH)
                lo = yfl_ref[BLOCKS_PER_TILE * j, q, src, :]
                hi = yfl_ref[BLOCKS_PER_TILE * j + 1, q, src, :]
                y32_ref[j, pl.ds(b * T_SSM + 2 * q, N_CHUNKS, stride=M_CHUNK), :] = (
                    jnp.where(low_half, lo, swap_halves(hi)))
                y32_ref[j, pl.ds(b * T_SSM + 2 * q + 1, N_CHUNKS, stride=M_CHUNK), :] = (
                    jnp.where(low_half, swap_halves(lo), hi))
    y = jnp.concatenate([y32_ref[j] for j in range(N_LANE_TILES_U)], axis=-1)
    z = _gelu_tanh(y).astype(_bf16)
    gate = jnp.dot(z, gluw_ref[...], preferred_element_type=_f32) + glub_ref[...]
    z = z * jax.nn.sigmoid(gate.astype(_bf16))
    ms = jnp.dot(z * z, ones_ref[...], preferred_element_type=_f32)
    out = z * lax.rsqrt(ms + EPS).astype(_bf16) * g_ref[...].astype(_bf16)
    y_ref[...] = out.reshape(BATCH, T_SSM, D_SSM)


def _ssm(u, bs_w, lam8, cs_w, tz_w, glu_w_b, glu_b, g_out_ssm, ones_ssm):
    rows = BATCH * T_SSM
    return pl.pallas_call(
        _ssm_kernel,
        grid=(SEQ // T_SSM,),
        in_specs=[
            pl.BlockSpec((BATCH, T_SSM, D_SSM), lambda i: (0, i, 0)),
            _const_spec((N_BLOCKS, BLOCK_CHUNK, BLOCK_COLS)),
            _const_spec((2, SUBLANES, N_STATE)),
            _const_spec((N_BLOCKS, BLOCK_COLS, BLOCK_CHUNK)),
            _const_spec((N_BLOCKS, BLOCK_CHUNK, BLOCK_CHUNK)),
            _const_spec((D_SSM, D_SSM)),
            _const_spec((1, D_SSM)),
            _const_spec((1, D_SSM)),
            _const_spec((D_SSM, D_SSM)),
        ],
        out_specs=pl.BlockSpec((BATCH, T_SSM, D_SSM), lambda i: (0, i, 0)),
        out_shape=jax.ShapeDtypeStruct((BATCH, SEQ, D_SSM), _bf16),
        scratch_shapes=[
            pltpu.VMEM((SUBLANES, N_BLOCKS * BLOCK_COLS), _f32),
            pltpu.VMEM((N_LANE_TILES_U, rows, LANES), _f32),
            pltpu.VMEM((N_BLOCKS, STEP_PAIRS, CHUNK_ROWS, LANES), _f32),
            pltpu.VMEM((CHUNK_ROWS, N_BLOCKS * BLOCK_COLS), _f32),
            pltpu.VMEM((N_BLOCKS, STEP_PAIRS, CHUNK_ROWS, LANES), _f32),
            pltpu.VMEM((N_LANE_TILES_U, rows, LANES), _f32),
        ],
        compiler_params=pltpu.CompilerParams(
            dimension_semantics=("arbitrary",), vmem_limit_bytes=VMEM_LIMIT),
        name="s5_mixer",
    )(u, bs_w, lam8, cs_w, tz_w, glu_w_b, glu_b, g_out_ssm, ones_ssm)


def _ffn_kernel(x_ref, ya_ref, yb_ref, mod_ref, wout_ref, gpm_ref, gpf_ref, wup_ref, fcw_ref, wdn_ref, gpo_ref,
                o_ref, halo_ref, act_ref, slab_ref):
    ti = pl.program_id(1)

    @pl.when(ti == 0)
    def _():
        halo_ref[...] = jnp.zeros(halo_ref.shape, _f32)

    x = x_ref[0]
    gt1 = mod_ref[0, 2:3, :]
    sh2 = mod_ref[0, 3:4, :]
    sc2 = mod_ref[0, 4:5, :]
    gt2 = mod_ref[0, 5:6, :]
    mix = jnp.dot(jnp.concatenate([ya_ref[0], yb_ref[0]], axis=-1), wout_ref[...], preferred_element_type=_f32)
    x1 = x + _rms(mix, gpm_ref[...] * gt1)
    h2 = (_rms(x1, gpf_ref[...] * (1.0 + sc2)) + sh2).astype(_bf16)

    def up_chunk(c):
        cols_a = slice(c * FF_CHUNK, (c + 1) * FF_CHUNK)
        cols_v = slice(D_FF + c * FF_CHUNK, D_FF + (c + 1) * FF_CHUNK)
        return (jnp.dot(h2, wup_ref[:, cols_a], preferred_element_type=_f32), cols_a,
                jnp.dot(h2, wup_ref[:, cols_v], preferred_element_type=_f32), cols_v)

    def conv(hc, cols, slab0):
        halo = halo_ref[:, cols]
        halo_ref[:, cols] = hc[T_FFN - SUBLANES:, :]
        return _causal_conv3(hc, halo, fcw_ref[0, :, cols], slab_ref, slab0)

    nxt = up_chunk(0)
    for c in range(N_FF_CHUNKS):
        ha, cols_a, hv, cols_v = nxt
        if c + 1 < N_FF_CHUNKS:
            nxt = up_chunk(c + 1)
        slab0 = (c % 2) * FFN_SLABS_PER_CHUNK
        a = conv(ha, cols_a, slab0)
        v = conv(hv, cols_v, slab0 + FF_CHUNK // LANES)
        act_ref[:, cols_a] = (a * jax.nn.sigmoid(a) * v).astype(_bf16)
    down = jnp.dot(act_ref[...], wdn_ref[...], preferred_element_type=_f32)
    o_ref[0] = x1 + _rms(down, gpo_ref[...] * gt2)


def _ffn(x, ya, yb, mod3, w_out_b, g_post_mix, g_pre_ffn, w_up_b, ffn_cw, w_down_b, g_post_ffn):
    n_t = SEQ // T_FFN
    return pl.pallas_call(
        _ffn_kernel,
        grid=(BATCH, n_t),
        in_specs=[
            pl.BlockSpec((1, T_FFN, D_MODEL), lambda b, t: (b, t, 0)),
            pl.BlockSpec((1, T_FFN, D_SSM), lambda b, t: (b, t, 0)),
            pl.BlockSpec((1, T_FFN, D_CONV), lambda b, t: (b, t, 0)),
            pl.BlockSpec((1, N_MOD, D_MODEL), lambda b, t: (b, 0, 0)),
            _const_spec((D_MODEL, D_MODEL)),
            _const_spec((1, D_MODEL)),
            _const_spec((1, D_MODEL)),
            _const_spec((D_MODEL, 2 * D_FF)),
            _const_spec((1, 3, 2 * D_FF)),
            _const_spec((D_FF, D_MODEL)),
            _const_spec((1, D_MODEL)),
        ],
        out_specs=pl.BlockSpec((1, T_FFN, D_MODEL), lambda b, t: (b, t, 0)),
        out_shape=jax.ShapeDtypeStruct((BATCH, SEQ, D_MODEL), _f32),
        scratch_shapes=[
            pltpu.VMEM((SUBLANES, 2 * D_FF), _f32),
            pltpu.VMEM((T_FFN, D_FF), _bf16),
            pltpu.VMEM((2 * FFN_SLABS_PER_CHUNK, T_FFN + SUBLANES, LANES), _f32),
        ],
        compiler_params=pltpu.CompilerParams(
            dimension_semantics=("arbitrary", "arbitrary"), vmem_limit_bytes=VMEM_LIMIT),
        name="outproj_convffn",
    )(x, ya, yb, mod3, w_out_b, g_post_mix, g_pre_ffn, w_up_b, ffn_cw, w_down_b, g_post_ffn)


def _head_mean_matrix(width, head):
    idx = np.arange(width) // head
    return jnp.asarray(np.where(idx[:, None] == idx[None, :], 1.0 / head, 0.0), dtype=_bf16)


def kernel(x, c, w_ada, b_ada, g_pre_mix, g_post_mix, w_in, ssm_lam_re, ssm_lam_im, ssm_log_step, ssm_b_re, ssm_b_im, ssm_c_re, ssm_c_im, ssm_d, glu_w, glu_b, g_out_ssm, conv_w, g_out_conv, w_out, g_pre_ffn, g_post_ffn, w_up, ffn_conv_w, w_down):
    assert x.shape == (BATCH, SEQ, D_MODEL) and w_ada.shape[0] == 1
    row = lambda a: a.reshape(1, -1)

    mod, lam8, bs_w, cs_w, tz_w = _setup(
        c, w_ada[0], b_ada[0], ssm_lam_re[0], ssm_lam_im[0], ssm_log_step[0], ssm_b_re[0], ssm_b_im[0],
        ssm_c_re[0], ssm_c_im[0], ssm_d[0])
    mod3 = mod.reshape(BATCH, N_MOD, D_MODEL)

    u, yb, w_up_b, w_out_b, glu_w_b, w_down_b = _premix(
        x, mod3, row(g_pre_mix[0]), w_in[0], conv_w, row(g_out_conv[0]),
        _head_mean_matrix(D_CONV, D_CONV // CONV_HEADS), w_up[0], w_out[0], glu_w[0], w_down[0])

    ya = _ssm(u, bs_w, lam8, cs_w, tz_w, glu_w_b,
              row(glu_b[0]), row(g_out_ssm[0]), _head_mean_matrix(D_SSM, SSM_GROUP))

    return _ffn(x, ya, yb, mod3, w_out_b, row(g_post_mix[0]), row(g_pre_ffn[0]),
                w_up_b, ffn_conv_w, w_down_b, row(g_post_ffn[0]))
```
